```python
import math
import jax
import jax.numpy as jnp
from jax import lax
import numpy as np

D_MODEL = 4096
BATCH = 1
SEQ = 8192
DEPTH = 4

GRID_W = 64
CTX_LEN = 256
N_MIXERS = 4
ADA_RANK = D_MODEL // 4
D_FF = ((8 * D_MODEL + 3 * 256 - 1) // (3 * 256)) * 256
RMS_EPS = 1e-6
NEG_INF = -1e30
HEAD_DIM = 128
ROPE_BASE = 10000.0
POOL_WINDOWS = (2, 4, 8, 16)
POOL_GROUP = D_MODEL // len(POOL_WINDOWS)
S5_GROUP = 16
S5_GROUPS = D_MODEL // S5_GROUP
S5_STATE = 64
S5_CHUNK = 128
SWA_HEADS = D_MODEL // HEAD_DIM
SWA_KV_HEADS = SWA_HEADS // 4
SWA_WINDOW = 128
SWA_BLOCK = 128
NA_HEADS = D_MODEL // HEAD_DIM
NA_WIN_ROWS = 8
NA_WIN_COLS = 16

kernel_name = 'hybrid_interleaved_diffusion_block'


def _rmsnorm(x, gain):
    xf = x.astype(jnp.float32)
    xf = xf * lax.rsqrt(jnp.mean(xf * xf, axis=-1, keepdims=True) + RMS_EPS)
    return (xf * gain.astype(jnp.float32)).astype(x.dtype)


def _adaln(cvec, w_down, w_up, b):
    m = (jax.nn.silu(cvec) @ w_down) @ w_up + b
    return m.reshape(m.shape[:-1] + (6, cvec.shape[-1]))


def _modulate(u, shift, scale):
    return u * (1 + scale) + shift


def _swiglu(u, w_gate_up, w_down):
    a, g = jnp.split(u @ w_gate_up, 2, axis=-1)
    return (jax.nn.silu(a) * g) @ w_down


def _axial_rope(x, rows, cols):
    hd = x.shape[-1]
    half = hd // 2
    quarter = half // 2
    inv = jnp.power(ROPE_BASE, -jnp.arange(quarter, dtype=jnp.float32) / quarter)
    xf = x.astype(jnp.float32)

    def rot(xs, pos):
        ang = pos.astype(jnp.float32)[:, None] * inv
        cos = jnp.cos(ang)[None, :, None, :]
        sin = jnp.sin(ang)[None, :, None, :]
        x1, x2 = xs[..., :quarter], xs[..., quarter:]
        return jnp.concatenate([x1 * cos - x2 * sin, x2 * cos + x1 * sin], axis=-1)

    return jnp.concatenate([rot(xf[..., :half], rows), rot(xf[..., half:], cols)], axis=-1).astype(x.dtype)


def _pool_mixer(u, pool_w, pool_scale):
    bsz, length, width = u.shape
    uf = u.astype(jnp.float32)
    cs = jnp.concatenate([jnp.zeros((bsz, 1, width), jnp.float32), jnp.cumsum(uf, axis=1)], axis=1)
    t = jnp.arange(length)
    groups = []
    for gi, w in enumerate(POOL_WINDOWS):
        lo = jnp.clip(t - w // 2, 0, length)
        hi = jnp.clip(t + w // 2, 0, length)
        sl = slice(gi * POOL_GROUP, (gi + 1) * POOL_GROUP)
        csg = cs[:, :, sl]
        mean = (csg[:, hi] - csg[:, lo]) / (hi - lo).astype(jnp.float32)[None, :, None]
        groups.append(mean - uf[:, :, sl])
    p = jnp.stack(groups, axis=2)
    y = jnp.einsum('blgc,gcd->blgd', p, pool_w.astype(jnp.float32)).reshape(bsz, length, width)
    return (y * pool_scale.astype(jnp.float32)).astype(u.dtype)


def _complex_affine_combine(e1, e2):
    a1r, a1i, b1r, b1i = e1
    a2r, a2i, b2r, b2i = e2
    return (a2r * a1r - a2i * a1i, a2r * a1i + a2i * a1r,
            a2r * b1r - a2i * b1i + b2r, a2r * b1i + a2i * b1r + b2i)


def _s5_scan(u, lam_re, lam_im, log_step, b_re, b_im, c_re, c_im, h_re, h_im):
    bsz, length = u.shape[:2]
    dt = jnp.exp(log_step)[:, None]
    z_re, z_im = lam_re * dt, lam_im * dt
    mag = jnp.exp(z_re)
    a_re, a_im = mag * jnp.cos(z_im), mag * jnp.sin(z_im)
    den = lam_re * lam_re + lam_im * lam_im
    k_re = ((a_re - 1.0) * lam_re + a_im * lam_im) / den
    k_im = (a_im * lam_re - (a_re - 1.0) * lam_im) / den
    bb_re = k_re[..., None] * b_re - k_im[..., None] * b_im
    bb_im = k_re[..., None] * b_im + k_im[..., None] * b_re
    chunk = min(S5_CHUNK, length)
    u_chunks = jnp.moveaxis(u.reshape(bsz, length // chunk, chunk, S5_GROUPS, S5_GROUP), 1, 0)

    def step(carry, uk):
        hr, hi = carry
        xr = jnp.einsum('gph,btgh->btgp', bb_re, uk)
        xi = jnp.einsum('gph,btgh->btgp', bb_im, uk)
        ar = jnp.broadcast_to(a_re, xr.shape)
        ai = jnp.broadcast_to(a_im, xr.shape)
        cr, ci, sr, si = lax.associative_scan(_complex_affine_combine, (ar, ai, xr, xi), axis=1)
        st_re = sr + cr * hr[:, None] - ci * hi[:, None]
        st_im = si + cr * hi[:, None] + ci * hr[:, None]
        y = jnp.einsum('ghp,btgp->btgh', c_re, st_re) - jnp.einsum('ghp,btgp->btgh', c_im, st_im)
        return (st_re[:, -1], st_im[:, -1]), y

    (h_re, h_im), ys = lax.scan(step, (h_re, h_im), u_chunks)
    y = jnp.moveaxis(ys, 0, 1).reshape(bsz, length, S5_GROUPS, S5_GROUP)
    return y, h_re, h_im


def _s5_mixer(u_lat, u_ctx, lam_re, lam_im, log_step, b_re, b_im, c_re, c_im, d, w_glu, ctx_out):
    f32 = jnp.float32
    bsz, length, width = u_lat.shape
    ul = u_lat.astype(f32).reshape(bsz, length, S5_GROUPS, S5_GROUP)
    uc = u_ctx.astype(f32).reshape(bsz, u_ctx.shape[1], S5_GROUPS, S5_GROUP)
    h0 = jnp.zeros((bsz, S5_GROUPS, S5_STATE), f32)

    def direction(k):
        return tuple(p[k].astype(f32) for p in (lam_re, lam_im, log_step, b_re, b_im, c_re, c_im))

    fwd, bwd = direction(0), direction(1)
    yc_f, hr, hi = _s5_scan(uc, *fwd, h0, h0)
    yl_f, _, _ = _s5_scan(ul, *fwd, hr, hi)
    yc_b, hr, hi = _s5_scan(uc[:, ::-1], *bwd, h0, h0)
    yl_b, _, _ = _s5_scan(ul[:, ::-1], *bwd, hr, hi)
    d_f = d.astype(f32).reshape(S5_GROUPS, S5_GROUP)

    def glu(y, u, dtype):
        g = jax.nn.gelu(y + d_f * u).reshape(bsz, -1, width).astype(dtype)
        a, gate = jnp.split(g @ w_glu, 2, axis=-1)
        return a * jax.nn.sigmoid(gate)

    y_lat = glu(yl_f + yl_b[:, ::-1], ul, u_lat.dtype)
    y_ctx = glu(yc_f + yc_b[:, ::-1], uc, u_ctx.dtype) if ctx_out else None
    return y_lat, y_ctx


def _project_heads(u, w_qkv, n_q, n_kv, q_gain, k_gain, with_q=True):
    bsz, length, _ = u.shape
    q_width = n_q * HEAD_DIM
    if with_q:
        h = u @ w_qkv
        q = _rmsnorm(h[..., :q_width].reshape(bsz, length, n_q, HEAD_DIM), q_gain)
        kv = h[..., q_width:]
    else:
        q = None
        kv = u @ w_qkv[:, q_width:]
    k, v = jnp.split(kv, 2, axis=-1)
    k = _rmsnorm(k.reshape(bsz, length, n_kv, HEAD_DIM), k_gain)
    v = v.reshape(bsz, length, n_kv, HEAD_DIM)
    return q, k, v


def _context_attention(qc, kc, vc, sinks):
    s = jnp.einsum('bqhgd,bkhd->bhgqk', qc, kc).astype(jnp.float32) * HEAD_DIM ** -0.5
    if sinks is not None:
        sink = jnp.broadcast_to(sinks.astype(jnp.float32)[None, :, :, None, None], s.shape[:-1] + (1,))
        s = jnp.concatenate([s, sink], axis=-1)
    p = jax.nn.softmax(s, axis=-1)
    if sinks is not None:
        p = p[..., :-1]
    o = jnp.einsum('bhgqk,bkhd->bqhgd', p.astype(vc.dtype), vc)
    return o.reshape(o.shape[0], o.shape[1], -1)


def _swa_mixer(u_lat, u_ctx, w_qkv, w_o, q_gain, k_gain, sinks, ctx_out):
    bsz, length, _ = u_lat.shape
    grp = SWA_HEADS // SWA_KV_HEADS
    win, blk = SWA_WINDOW, SWA_BLOCK
    n_keys = blk + 2 * win
    scale = HEAD_DIM ** -0.5
    q, k, v = _project_heads(u_lat, w_qkv, SWA_HEADS, SWA_KV_HEADS, q_gain, k_gain)
    t = jnp.arange(length)
    q = _axial_rope(q, t // GRID_W, t % GRID_W)
    k = _axial_rope(k, t // GRID_W, t % GRID_W)
    qc, kc, vc = _project_heads(u_ctx, w_qkv, SWA_HEADS, SWA_KV_HEADS, q_gain, k_gain, with_q=ctx_out)
    q = q.reshape(bsz, length, SWA_KV_HEADS, grp, HEAD_DIM)
    sink = sinks.astype(jnp.float32).reshape(SWA_KV_HEADS, grp)
    pad = ((0, 0), (win, win), (0, 0), (0, 0))
    kp = jnp.pad(k, pad)
    vp = jnp.pad(v, pad)
    kk = jnp.arange(n_keys)
    qq = jnp.arange(blk)
    rel = kk[None, :] - qq[:, None]
    band = (rel >= 0) & (rel <= 2 * win)

    def block(b):
        start = b * blk
        qb = lax.dynamic_slice_in_dim(q, start, blk, axis=1)
        kb = lax.dynamic_slice_in_dim(kp, start, n_keys, axis=1)
        vb = lax.dynamic_slice_in_dim(vp, start, n_keys, axis=1)
        pos = start - win + kk
        valid = band & ((pos >= 0) & (pos < length))[None, :]
        s_loc = jnp.einsum('bqhgd,bkhd->bhgqk', qb, kb).astype(jnp.float32) * scale
        s_loc = jnp.where(valid, s_loc, NEG_INF)
        s_ctx = jnp.einsum('bqhgd,bchd->bhgqc', qb, kc).astype(jnp.float32) * scale
        s_sink = jnp.broadcast_to(sink[None, :, :, None, None], s_ctx.shape[:-1] + (1,))
        p = jax.nn.softmax(jnp.concatenate([s_loc, s_ctx, s_sink], axis=-1), axis=-1).astype(vb.dtype)
        return (jnp.einsum('bhgqk,bkhd->bqhgd', p[..., :n_keys], vb)
                + jnp.einsum('bhgqc,bchd->bqhgd', p[..., n_keys:-1], vc))

    o = lax.map(block, jnp.arange(length // blk))
    o = jnp.moveaxis(o, 0, 1).reshape(bsz, length, SWA_HEADS * HEAD_DIM)
    y_lat = o @ w_o
    y_ctx = None
    if ctx_out:
        qc = qc.reshape(bsz, qc.shape[1], SWA_KV_HEADS, grp, HEAD_DIM)
        y_ctx = _context_attention(qc, kc, vc, sink) @ w_o
    return y_lat, y_ctx


def _na_mixer(u_lat, u_ctx, w_qkv, w_o, q_gain, k_gain, rpb, ctx_out):
    bsz, length, _ = u_lat.shape
    n_heads = NA_HEADS
    n_rows = length // GRID_W
    kr_size = min(NA_WIN_ROWS, n_rows)
    kc_size = NA_WIN_COLS
    scale = HEAD_DIM ** -0.5
    q, k, v = _project_heads(u_lat, w_qkv, n_heads, n_heads, q_gain, k_gain)
    qc, kc, vc = _project_heads(u_ctx, w_qkv, n_heads, n_heads, q_gain, k_gain, with_q=ctx_out)
    qg = q.reshape(bsz, n_rows, GRID_W, n_heads, HEAD_DIM)
    kg = k.reshape(bsz, n_rows, GRID_W, n_heads, HEAD_DIM)
    vg = v.reshape(bsz, n_rows, GRID_W, n_heads, HEAD_DIM)
    col = jnp.arange(GRID_W)
    c0 = jnp.clip(col - kc_size // 2, 0, GRID_W - kc_size)
    col_mask = (col[None, :] >= c0[:, None]) & (col[None, :] < c0[:, None] + kc_size)
    col_idx = jnp.clip(col[None, :] - col[:, None] + kc_size - 1, 0, 2 * kc_size - 2)
    bias_cols = rpb.astype(jnp.float32)[:, :, col_idx]
    mask5 = col_mask[None, None, :, None, :]
    n_loc = kr_size * GRID_W

    def row_block(r):
        r0 = jnp.clip(r - kr_size // 2, 0, n_rows - kr_size)
        qr = lax.dynamic_index_in_dim(qg, r, axis=1, keepdims=False)
        kr = lax.dynamic_slice_in_dim(kg, r0, kr_size, axis=1)
        vr = lax.dynamic_slice_in_dim(vg, r0, kr_size, axis=1)
        row_idx = r0 + jnp.arange(kr_size) - r + NA_WIN_ROWS - 1
        bias = jnp.transpose(jnp.take(bias_cols, row_idx, axis=1), (0, 2, 1, 3))
        s_loc = jnp.einsum('bqhd,bakhd->bhqak', qr, kr).astype(jnp.float32) * scale + bias[None]
        s_loc = jnp.where(mask5, s_loc, NEG_INF).reshape(bsz, n_heads, GRID_W, n_loc)
        s_ctx = jnp.einsum('bqhd,bchd->bhqc', qr, kc).astype(jnp.float32) * scale
        p = jax.nn.softmax(jnp.concatenate([s_loc, s_ctx], axis=-1), axis=-1).astype(vr.dtype)
        p_loc = p[..., :n_loc].reshape(bsz, n_heads, GRID_W, kr_size, GRID_W)
        return (jnp.einsum('bhqak,bakhd->bqhd', p_loc, vr)
                + jnp.einsum('bhqc,bchd->bqhd', p[..., n_loc:], vc))

    o = lax.map(row_block, jnp.arange(n_rows))
    o = jnp.moveaxis(o, 0, 1).reshape(bsz, length, n_heads * HEAD_DIM)
    y_lat = o @ w_o
    y_ctx = None
    if ctx_out:
        qc = qc.reshape(bsz, qc.shape[1], n_heads, 1, HEAD_DIM)
        y_ctx = _context_attention(qc, kc, vc, None) @ w_o
    return y_lat, y_ctx


def setup_inputs(seed: int = 0) -> dict:
    key = jax.random.key(seed)
    ks = iter(jax.random.split(key, 40))
    f32 = jnp.float32

    def nrm(shape, s):
        return jax.random.normal(next(ks), shape, f32) * s

    D = D_MODEL
    G, P, H = S5_GROUPS, S5_STATE, S5_GROUP
    swa_cols = (SWA_HEADS + 2 * SWA_KV_HEADS) * HEAD_DIM
    na_cols = 3 * NA_HEADS * HEAD_DIM
    return {
        'x': nrm((BATCH, SEQ, D), 1.0),
        'c': nrm((BATCH, D), 1.0),
        'ctx': nrm((BATCH, CTX_LEN, D), 1.0),
        'c_ctx': nrm((D,), 1.0),
        'ada_down': nrm((DEPTH, D, ADA_RANK), D ** -0.5),
        'ada_up': nrm((DEPTH, ADA_RANK, 6 * D), 0.5 * ADA_RANK ** -0.5),
        'ada_b': nrm((DEPTH, 6 * D), 0.02),
        'norm_mix': 1.0 + nrm((DEPTH, D), 0.02),
        'norm_ffn': 1.0 + nrm((DEPTH, D), 0.02),
        'ffn_w_gate_up': nrm((DEPTH, D, 2 * D_FF), D ** -0.5),
        'ffn_w_down': nrm((DEPTH, D_FF, D), D_FF ** -0.5),
        'pool_w': nrm((len(POOL_WINDOWS), POOL_GROUP, POOL_GROUP), POOL_GROUP ** -0.5),
        'pool_scale': 1.0 + nrm((D,), 0.02),
        's5_lam_re': -0.5 + nrm((2, G, P), 0.01),
        's5_lam_im': math.pi * jnp.arange(P, dtype=f32) + nrm((2, G, P), 0.01),
        's5_log_step': jax.random.uniform(next(ks), (2, G), f32, math.log(1e-3), math.log(1e-1)),
        's5_b_re': nrm((2, G, P, H), (2 * H) ** -0.5),
        's5_b_im': nrm((2, G, P, H), (2 * H) ** -0.5),
        's5_c_re': nrm((2, G, H, P), P ** -0.5),
        's5_c_im': nrm((2, G, H, P), P ** -0.5),
        's5_d': nrm((D,), 1.0),
        's5_w_glu': nrm((D, 2 * D), D ** -0.5),
        'swa_w_qkv': nrm((D, swa_cols), D ** -0.5),
        'swa_w_o': nrm((SWA_HEADS * HEAD_DIM, D), (SWA_HEADS * HEAD_DIM) ** -0.5),
        'swa_q_gain': 1.0 + nrm((HEAD_DIM,), 0.02),
        'swa_k_gain': 1.0 + nrm((HEAD_DIM,), 0.02),
        'swa_sinks': nrm((SWA_HEADS,), 1.0),
        'na_w_qkv': nrm((D, na_cols), D ** -0.5),
        'na_w_o': nrm((NA_HEADS * HEAD_DIM, D), (NA_HEADS * HEAD_DIM) ** -0.5),
        'na_q_gain': 1.0 + nrm((HEAD_DIM,), 0.02),
        'na_k_gain': 1.0 + nrm((HEAD_DIM,), 0.02),
        'na_rpb': nrm((NA_HEADS, 2 * NA_WIN_ROWS - 1, 2 * NA_WIN_COLS - 1), 0.1),
    }


def reference(x, c, ctx, c_ctx, ada_down, ada_up, ada_b, norm_mix, norm_ffn, ffn_w_gate_up, ffn_w_down,
              pool_w, pool_scale, s5_lam_re, s5_lam_im, s5_log_step, s5_b_re, s5_b_im, s5_c_re, s5_c_im,
              s5_d, s5_w_glu, swa_w_qkv, swa_w_o, swa_q_gain, swa_k_gain, swa_sinks,
              na_w_qkv, na_w_o, na_q_gain, na_k_gain, na_rpb):
    for i in range(DEPTH):
        kind = i % N_MIXERS
        last = i == DEPTH - 1
        mod_x = _adaln(c, ada_down[i], ada_up[i], ada_b[i])[:, :, None, :]
        mod_c = _adaln(c_ctx, ada_down[i], ada_up[i], ada_b[i])
        ul = _modulate(_rmsnorm(x, norm_mix[i]), mod_x[:, 0], mod_x[:, 1])
        uc = _modulate(_rmsnorm(ctx, norm_mix[i]), mod_c[0], mod_c[1])
        if kind == 0:
            yl = _pool_mixer(ul, pool_w, pool_scale)
            yc = None if last else _pool_mixer(uc, pool_w, pool_scale)
        elif kind == 1:
            yl, yc = _s5_mixer(ul, uc, s5_lam_re, s5_lam_im, s5_log_step, s5_b_re, s5_b_im,
                               s5_c_re, s5_c_im, s5_d, s5_w_glu, not last)
        elif kind == 2:
            yl, yc = _swa_mixer(ul, uc, swa_w_qkv, swa_w_o, swa_q_gain, swa_k_gain, swa_sinks, not last)
        else:
            yl, yc = _na_mixer(ul, uc, na_w_qkv, na_w_o, na_q_gain, na_k_gain, na_rpb, not last)
        x = x + mod_x[:, 2] * yl
        hx = _modulate(_rmsnorm(x, norm_ffn[i]), mod_x[:, 3], mod_x[:, 4])
        x = x + mod_x[:, 5] * _swiglu(hx, ffn_w_gate_up[i], ffn_w_down[i])
        if not last:
            ctx = ctx + mod_c[2] * yc
            hc = _modulate(_rmsnorm(ctx, norm_ffn[i]), mod_c[3], mod_c[4])
            ctx = ctx + mod_c[5] * _swiglu(hc, ffn_w_gate_up[i], ffn_w_down[i])
    return x
```

```python
import functools
import math

import jax
import jax.numpy as jnp
from jax import lax
from jax.experimental import pallas as pl
from jax.experimental.pallas import tpu as pltpu

F32 = jnp.float32
BF16 = jnp.bfloat16

RMS_EPS = 1e-6
NEG_INF = -1e30
HEAD_DIM = 128
GRID_W = 64
ROPE_BASE = 10000.0
POOL_WINDOWS = (2, 4, 8, 16)
POOL_HALO = 8
S5_GROUP = 16
S5_STATE = 64
S5_CHUNK = 128
S5_PITCH = S5_CHUNK + 8
SWA_WINDOW = 128
SWA_GROUP = 4
NA_WIN_ROWS = 8
NA_WIN_COLS = 16
NA_QROWS = 4

LANES = 128
ROW_TILE = 256
V7X_VMEM_LIMIT = 56 * 1024 * 1024


def _pick(n, candidates):
    for c in candidates:
        if n % c == 0:
            return c
    raise ValueError(f"no tile for {n} in {candidates}")


def _params(sem, vmem=V7X_VMEM_LIMIT):
    return pltpu.CompilerParams(dimension_semantics=sem, vmem_limit_bytes=vmem)


def _dot(a, b):
    return jnp.dot(a, b, preferred_element_type=F32)


def _dot_nt(a, b):
    return lax.dot_general(a, b, (((1,), (1,)), ((), ())), preferred_element_type=F32)


def _small_mm_kernel(a_ref, w_ref, b_ref, o_ref, *, silu_in, kc):
    a = a_ref[0]
    if silu_in:
        a = a * jax.nn.sigmoid(a)
    a = a.astype(BF16)
    k = a.shape[1]
    acc = jnp.zeros(o_ref.shape[1:], F32)
    for k0 in range(0, k, kc):
        acc += _dot(a[:, k0:k0 + kc], w_ref[0, k0:k0 + kc, :].astype(BF16))
    o_ref[0] = acc + b_ref[0]


def _small_mm(a, w, b, *, silu_in, tn):
    depth, k, n = w.shape
    shared = a.shape[0] == 1
    kern = functools.partial(_small_mm_kernel, silu_in=silu_in, kc=min(k, 512))
    return pl.pallas_call(
        kern,
        grid=(depth, n // tn),
        in_specs=[
            pl.BlockSpec((1, 8, k), lambda l, j: (0 if shared else l, 0, 0)),
            pl.BlockSpec((1, k, tn), lambda l, j: (l, 0, j)),
            pl.BlockSpec((1, 1, tn), lambda l, j: (l, 0, j)),
        ],
        out_specs=pl.BlockSpec((1, 8, tn), lambda l, j: (l, 0, j)),
        out_shape=jax.ShapeDtypeStruct((depth, 8, n), F32),
        compiler_params=_params(("arbitrary", "arbitrary")),
        name="adaln_mm",
    )(a, w, b)


def _adaln_all(c, c_ctx, ada_down, ada_up, ada_b):
    depth, d, rank = ada_down.shape
    cvec = jnp.zeros((1, 8, d), F32).at[0, 0].set(c[0]).at[0, 1].set(c_ctx)
    t = _small_mm(cvec, ada_down, jnp.zeros((depth, 1, rank), F32), silu_in=True, tn=_pick(rank, (512, 256, 128)))
    m = _small_mm(t, ada_up, ada_b.reshape(depth, 1, 6 * d), silu_in=False, tn=_pick(6 * d, (2048, 1024, 512)))
    return m[:, :2].reshape(depth, 2, 6, d)


def _rms_mod(x, gain, shift, scale):
    ms = jnp.mean(x * x, axis=-1, keepdims=True)
    return (x * lax.rsqrt(ms + RMS_EPS) * gain) * (1.0 + scale) + shift


def _norm_mod_kernel(x_ref, g_ref, mod_ref, o_ref, *, shift_idx):
    u = _rms_mod(x_ref[...], g_ref[...], mod_ref[0, shift_idx:shift_idx + 1, :],
                 mod_ref[0, shift_idx + 1:shift_idx + 2, :])
    o_ref[...] = u.astype(o_ref.dtype)


def _norm_mod(xs, gain, mod, *, shift_idx, rows, n_lat):
    d = xs.shape[1]
    tr = ROW_TILE
    return pl.pallas_call(
        functools.partial(_norm_mod_kernel, shift_idx=shift_idx),
        grid=(rows // tr,),
        in_specs=[
            pl.BlockSpec((tr, d), lambda i: (i, 0)),
            pl.BlockSpec((1, d), lambda i: (0, 0)),
            pl.BlockSpec((1, 6, d), lambda i: (jnp.where(i * tr >= n_lat, 1, 0), 0, 0)),
        ],
        out_specs=pl.BlockSpec((tr, d), lambda i: (i, 0)),
        out_shape=jax.ShapeDtypeStruct((rows, d), BF16),
        compiler_params=_params(("arbitrary",)),
        name="norm_mod",
    )(xs, gain.reshape(1, d), mod)


def _mm(a, w, *, rows, tm, tn, out_cols, out_dtype, col_maps, epilogue, extras=(), w_outer, name):
    k = a.shape[1]
    n_i, n_j = rows // tm, out_cols // tn
    cast = w.dtype != BF16
    assert w_outer or not cast
    n_parts = len(col_maps)
    kc = _pick(k, (512, 256, 128))

    if w_outer:
        grid = (n_j, n_i)
        ij = lambda g0, g1: (g1, g0)
    else:
        grid = (n_i, n_j)
        ij = lambda g0, g1: (g0, g1)

    in_specs = [pl.BlockSpec((tm, k), lambda g0, g1: (ij(g0, g1)[0], 0))]
    for cm in col_maps:
        in_specs.append(pl.BlockSpec((k, tn), lambda g0, g1, cm=cm: (0, cm(ij(g0, g1)[1]))))
    for _, shape, fn in extras:
        in_specs.append(pl.BlockSpec(shape, lambda g0, g1, fn=fn: fn(*ij(g0, g1))))
    n_ex = len(extras)

    def body(*refs):
        a_ref = refs[0]
        w_refs = refs[1:1 + n_parts]
        ex_refs = refs[1 + n_parts:1 + n_parts + n_ex]
        o_ref = refs[1 + n_parts + n_ex]
        scratch = refs[2 + n_parts + n_ex:]
        i, j = ij(pl.program_id(0), pl.program_id(1))
        if cast:
            @pl.when(i == 0)
            def _():
                def cp(c, carry):
                    sl = pl.ds(pl.multiple_of(c * kc, kc), kc)
                    for p in range(n_parts):
                        scratch[p][sl, :] = w_refs[p][sl, :].astype(BF16)
                    return carry
                lax.fori_loop(0, k // kc, cp, 0)
            ws = scratch
        else:
            ws = w_refs
        av = a_ref[...]
        accs = [_dot(av, ws[p][...]) for p in range(n_parts)]
        epilogue(accs, i, j, ex_refs, o_ref)

    return pl.pallas_call(
        body,
        grid=grid,
        in_specs=in_specs,
        out_specs=pl.BlockSpec((tm, tn), lambda g0, g1: ij(g0, g1)),
        out_shape=jax.ShapeDtypeStruct((rows, out_cols), out_dtype),
        scratch_shapes=[pltpu.VMEM((k, tn), BF16) for _ in range(n_parts)] if cast else [],
        compiler_params=_params(("arbitrary", "arbitrary")),
        name=name,
    )(a, *([w] * n_parts), *[e[0] for e in extras])


def _row_gate(mod_ref, idx, i, tm, n_lat):
    rows = i * tm + lax.broadcasted_iota(jnp.int32, (tm, 1), 0)
    return jnp.where(rows < n_lat, mod_ref[0, idx:idx + 1, :], mod_ref[1, idx:idx + 1, :])


def _epi_swiglu(accs, i, j, ex, o_ref):
    a, g = accs
    o_ref[...] = (a * jax.nn.sigmoid(a) * g).astype(o_ref.dtype)


def _epi_residual(accs, i, j, ex, o_ref, *, gate_idx, tm, n_lat):
    res_ref, mod_ref = ex
    o_ref[...] = res_ref[...] + _row_gate(mod_ref, gate_idx, i, tm, n_lat) * accs[0]


def _epi_glu_residual(accs, i, j, ex, o_ref, *, gate_idx, tm, n_lat):
    res_ref, mod_ref = ex
    a, g = accs
    o_ref[...] = res_ref[...] + _row_gate(mod_ref, gate_idx, i, tm, n_lat) * (a * jax.nn.sigmoid(g))


def _residual_extras(xs, mod, tm, tn):
    return ((xs, (tm, tn), lambda i, j: (i, j)), (mod, (2, 6, tn), lambda i, j: (0, 0, j)))


def _mm_residual(a, w, xs, mod, *, gate_idx, rows, n_lat, w_outer, name):
    n = w.shape[1]
    tm = _pick(rows, (768, 512, 640, 256))
    tn = _pick(n, (512, 256) if w_outer else (256, 128))
    return _mm(a, w, rows=rows, tm=tm, tn=tn, out_cols=n, out_dtype=F32, col_maps=(lambda j: j,),
               epilogue=functools.partial(_epi_residual, gate_idx=gate_idx, tm=tm, n_lat=n_lat),
               extras=_residual_extras(xs, mod, tm, tn), w_outer=w_outer, name=name)


def _pool_kernel(xp_ref, x_ref, xn_ref, g_ref, mod_ref, pw_ref, ps_ref, o_ref, *, tr, n_lat, n_tot):
    i = pl.program_id(0)
    halo = POOL_HALO
    gain = g_ref[...]
    shift, scale, gate = mod_ref[0, 0:1, :], mod_ref[0, 1:2, :], mod_ref[0, 2:3, :]
    x = x_ref[...]
    ue = jnp.concatenate([_rms_mod(xp_ref[...], gain, shift, scale), _rms_mod(x, gain, shift, scale),
                          _rms_mod(xn_ref[...], gain, shift, scale)], axis=0)
    is_lat = i * tr < n_lat
    seq_lo = jnp.where(is_lat, 0, n_lat)
    seq_hi = jnp.where(is_lat, n_lat, n_tot)
    t_g = i * tr + lax.broadcasted_iota(jnp.int32, (tr, 1), 0)
    s_g = i * tr - halo + lax.broadcasted_iota(jnp.int32, (1, tr + 2 * halo), 1)
    s_ok = (s_g >= seq_lo) & (s_g < seq_hi)
    pg = ue.shape[1] // len(POOL_WINDOWS)
    for gi, w in enumerate(POOL_WINDOWS):
        cols = slice(gi * pg, (gi + 1) * pg)
        ug = ue[:, cols]
        band = (s_g >= t_g - w // 2) & (s_g < t_g + w // 2) & s_ok
        ssum = _dot(jnp.where(band, 1.0, 0.0).astype(BF16), ug.astype(BF16))
        cnt = jnp.minimum(t_g + w // 2, seq_hi) - jnp.maximum(t_g - w // 2, seq_lo)
        p = ssum / cnt.astype(F32) - ug[halo:halo + tr]
        y = _dot(p.astype(BF16), pw_ref[gi]) * ps_ref[:, cols]
        o_ref[:, cols] = x[:, cols] + gate[:, cols] * y


def _pool_layer(xs, gain, mod, pool_w, pool_scale, *, rows, n_lat):
    n_tot, d = xs.shape
    tr, halo = ROW_TILE, POOL_HALO
    hb = tr // halo
    last_hb = n_tot // halo - 1
    return pl.pallas_call(
        functools.partial(_pool_kernel, tr=tr, n_lat=n_lat, n_tot=n_tot),
        grid=(rows // tr,),
        in_specs=[
            pl.BlockSpec((halo, d), lambda i: (jnp.maximum(i * hb - 1, 0), 0)),
            pl.BlockSpec((tr, d), lambda i: (i, 0)),
            pl.BlockSpec((halo, d), lambda i: (jnp.minimum((i + 1) * hb, last_hb), 0)),
            pl.BlockSpec((1, d), lambda i: (0, 0)),
            pl.BlockSpec((1, 6, d), lambda i: (jnp.where(i * tr >= n_lat, 1, 0), 0, 0)),
            pl.BlockSpec(pool_w.shape, lambda i: (0, 0, 0)),
            pl.BlockSpec((1, d), lambda i: (0, 0)),
        ],
        out_specs=pl.BlockSpec((tr, d), lambda i: (i, 0)),
        out_shape=jax.ShapeDtypeStruct((rows, d), F32),
        compiler_params=_params(("arbitrary",)),
        name="pool_mixer",
    )(xs, xs, xs, gain.reshape(1, d), mod, pool_w.astype(BF16), pool_scale.reshape(1, d))


def _s5_prep_kernel(lre_ref, lim_ref, ls_ref, bre_ref, bim_ref, are_ref, aim_ref, bbre_ref, bbim_ref):
    lam_re, lam_im = lre_ref[0], lim_ref[0]
    dt = jnp.exp(ls_ref[0])
    z_re, z_im = lam_re * dt, lam_im * dt
    mag = jnp.exp(z_re)
    a_re, a_im = mag * jnp.cos(z_im), mag * jnp.sin(z_im)
    den = lam_re * lam_re + lam_im * lam_im
    k_re = ((a_re - 1.0) * lam_re + a_im * lam_im) / den
    k_im = (a_im * lam_re - (a_re - 1.0) * lam_im) / den
    are_ref[0] = a_re
    aim_ref[0] = a_im
    b_re, b_im = bre_ref[0], bim_ref[0]
    bbre_ref[0] = k_re[:, None, :] * b_re - k_im[:, None, :] * b_im
    bbim_ref[0] = k_re[:, None, :] * b_im + k_im[:, None, :] * b_re


def _s5_prep(lam_re, lam_im, log_step, b_re, b_im):
    _, g, p, h = b_re.shape
    spec2 = pl.BlockSpec((1, g, p), lambda d: (d, 0, 0))
    spec3 = pl.BlockSpec((1, g, h, p), lambda d: (d, 0, 0, 0))
    return pl.pallas_call(
        _s5_prep_kernel,
        grid=(2,),
        in_specs=[spec2, spec2, pl.BlockSpec((1, g, 1), lambda d: (d, 0, 0)), spec3, spec3],
        out_specs=[spec2, spec2, spec3, spec3],
        out_shape=[jax.ShapeDtypeStruct((2, g, p), F32)] * 2 + [jax.ShapeDtypeStruct((2, g, h, p), F32)] * 2,
        compiler_params=_params(("arbitrary",)),
        name="s5_prep",
    )(lam_re, lam_im, log_step[..., None], jnp.swapaxes(b_re, 2, 3), jnp.swapaxes(b_im, 2, 3))


def _s5_pack(a_re, a_im, bb_re, bb_im, c_re, c_im):
    _, g, h, p = bb_re.shape
    r = g // 2
    eye2 = jnp.eye(2, dtype=F32)
    eye4 = jnp.eye(4, dtype=F32)
    slot = eye4[jnp.arange(r) % 4]

    def win_part(bb):
        t = bb.reshape(2, r, 2, h, p)
        t = t[:, :, :, :, None, :] * eye2[None, None, :, None, :, None]
        t = t.reshape(2, r, 2 * h, 2 * p)
        t = slot[None, :, :, None, None] * t[:, :, None, :, :]
        return t.reshape(2, r, 4 * 2 * h, 2 * p)

    def c_part(c):
        t = jnp.swapaxes(c, 2, 3).reshape(2, r, 2, p, h)
        t = t[:, :, :, :, None, :] * eye2[None, None, :, None, :, None]
        t = t.reshape(2, r, 2 * p, 2 * h)
        t = t[:, :, :, None, :] * slot[None, :, None, :, None]
        return t.reshape(2, r, 2 * p, 4 * 2 * h)

    win = jnp.concatenate([win_part(bb_re), win_part(bb_im)], axis=-1).astype(BF16)
    cre = c_part(c_re).astype(BF16)
    cim = c_part(-c_im).astype(BF16)
    return win, cre, cim, a_re.reshape(2, r, 2 * p), a_im.reshape(2, r, 2 * p)


def _s5_scan_kernel(u_ref, win_ref, cre_ref, cim_ref, are_ref, aim_ref, y_ref, xre, xim, hre, him, *, nsub, rb):
    g0, c = pl.program_id(0), pl.program_id(1)
    rev = g0 // nsub == 1
    tc, pitch = S5_CHUNK, S5_PITCH

    @pl.when(c == 0)
    def _():
        hre[...] = jnp.zeros_like(hre)
        him[...] = jnp.zeros_like(him)

    def inproj(q, carry):
        ut = u_ref[:, pl.ds(pl.multiple_of(q * LANES, LANES), LANES)]
        for s in range(4):
            r = q * 4 + s
            x = _dot(ut, win_ref[0, r])
            rows = pl.ds(pl.multiple_of(r * pitch, 8), tc)
            xre[rows, :] = x[:, :LANES]
            xim[rows, :] = x[:, LANES:]
        return carry

    lax.fori_loop(0, rb // 4, inproj, 0)

    a_re, a_im = are_ref[0], aim_ref[0]

    def step(t, carry):
        h_re, h_im = carry
        tt = jnp.where(rev, tc - 1 - t, t)
        rows = pl.ds(tt, rb, stride=pitch)
        n_re = a_re * h_re - a_im * h_im + xre[rows, :]
        n_im = a_re * h_im + a_im * h_re + xim[rows, :]
        xre[rows, :] = n_re
        xim[rows, :] = n_im
        return n_re, n_im

    h_re, h_im = lax.fori_loop(0, tc, step, (hre[...], him[...]))
    hre[...] = h_re
    him[...] = h_im

    def outproj(q, carry):
        acc = jnp.zeros((tc, LANES), F32)
        for s in range(4):
            r = q * 4 + s
            rows = pl.ds(pl.multiple_of(r * pitch, 8), tc)
            acc += _dot(xre[rows, :].astype(BF16), cre_ref[0, r]) + _dot(xim[rows, :].astype(BF16), cim_ref[0, r])
        y_ref[0, :, pl.ds(pl.multiple_of(q * LANES, LANES), LANES)] = acc
        return carry

    lax.fori_loop(0, rb // 4, outproj, 0)


def _s5_scan(u, win, cre, cim, a_re, a_im, *, n_lat):
    n_tot, d = u.shape
    r_all = win.shape[1]
    rb = min(64, r_all)
    nsub = r_all // rb
    tc = S5_CHUNK
    n_c, n_lc = n_tot // tc, n_lat // tc
    cols = rb * 2 * S5_GROUP

    def split(w):
        return w.reshape((2 * nsub, rb) + w.shape[2:])

    def chunk(g0, c):
        return jnp.where(g0 // nsub == 0, (c + n_lc) % n_c, n_c - 1 - c)

    wspec = lambda shape: pl.BlockSpec((1,) + shape, lambda g0, c: (g0,) + (0,) * len(shape))
    return pl.pallas_call(
        functools.partial(_s5_scan_kernel, nsub=nsub, rb=rb),
        grid=(2 * nsub, n_c),
        in_specs=[
            pl.BlockSpec((tc, cols), lambda g0, c: (chunk(g0, c), g0 % nsub)),
            wspec((rb, LANES, 2 * LANES)), wspec((rb, LANES, LANES)), wspec((rb, LANES, LANES)),
            wspec((rb, LANES)), wspec((rb, LANES)),
        ],
        out_specs=pl.BlockSpec((1, tc, cols), lambda g0, c: (g0 // nsub, chunk(g0, c), g0 % nsub)),
        out_shape=jax.ShapeDtypeStruct((2, n_tot, d), F32),
        scratch_shapes=[pltpu.VMEM((rb * S5_PITCH, LANES), F32), pltpu.VMEM((rb * S5_PITCH, LANES), F32),
                        pltpu.VMEM((rb, LANES), F32), pltpu.VMEM((rb, LANES), F32)],
        compiler_params=_params(("arbitrary", "arbitrary")),
        name="s5_scan",
    )(u, split(win), split(cre), split(cim), split(a_re), split(a_im))


def _s5_gelu_kernel(y_ref, u_ref, d_ref, o_ref):
    v = y_ref[0] + y_ref[1] + d_ref[...] * u_ref[...].astype(F32)
    o_ref[...] = jax.nn.gelu(v, approximate=True).astype(o_ref.dtype)


def _s5_gelu(y, u, dvec, *, rows):
    d = u.shape[1]
    tr = ROW_TILE
    return pl.pallas_call(
        _s5_gelu_kernel,
        grid=(rows // tr,),
        in_specs=[pl.BlockSpec((2, tr, d), lambda i: (0, i, 0)), pl.BlockSpec((tr, d), lambda i: (i, 0)),
                  pl.BlockSpec((1, d), lambda i: (0, 0))],
        out_specs=pl.BlockSpec((tr, d), lambda i: (i, 0)),
        out_shape=jax.ShapeDtypeStruct((rows, d), BF16),
        compiler_params=_params(("arbitrary",)),
        name="s5_gelu",
    )(y, u, dvec.reshape(1, d))


def _rope_tables(n_lat, n_ctx):
    quarter = HEAD_DIM // 4
    inv = jnp.power(ROPE_BASE, -jnp.arange(quarter, dtype=F32) / quarter)
    t = jnp.arange(n_lat)
    ang_r = (t // GRID_W).astype(F32)[:, None] * inv
    ang_c = (t % GRID_W).astype(F32)[:, None] * inv
    ang = jnp.concatenate([ang_r, ang_r, ang_c, ang_c], axis=-1)
    sign = jnp.where((jnp.arange(HEAD_DIM) % (2 * quarter)) < quarter, -1.0, 1.0)
    cos = jnp.concatenate([jnp.cos(ang), jnp.ones((n_ctx, HEAD_DIM), F32)], axis=0)
    sin = jnp.concatenate([jnp.sin(ang) * sign, jnp.zeros((n_ctx, HEAD_DIM), F32)], axis=0)
    return cos, sin


def _epi_qkv(accs, i, j, ex, o_ref, *, n_norm_tiles, rope, tn):
    acc = accs[0]
    gain_ref = ex[0]

    @pl.when(j < n_norm_tiles)
    def _():
        quarter = HEAD_DIM // 4
        lane = lax.broadcasted_iota(jnp.int32, (1, HEAD_DIM), 1)
        first = (lane % (2 * quarter)) < quarter
        for h in range(tn // HEAD_DIM):
            cols = slice(h * HEAD_DIM, (h + 1) * HEAD_DIM)
            x = acc[:, cols]
            ms = jnp.mean(x * x, axis=-1, keepdims=True)
            xn = x * lax.rsqrt(ms + RMS_EPS) * gain_ref[:, cols]
            if rope:
                cos_ref, sin_ref = ex[1], ex[2]
                partner = jnp.where(first, pltpu.roll(xn, HEAD_DIM - quarter, 1), pltpu.roll(xn, quarter, 1))
                xn = xn * cos_ref[...] + partner * sin_ref[...]
            o_ref[:, cols] = xn.astype(o_ref.dtype)

    @pl.when(j >= n_norm_tiles)
    def _():
        o_ref[...] = acc.astype(o_ref.dtype)


def _qkv_proj(u, w_qkv, q_gain, k_gain, *, n_q, n_kv, rope_tables, rows):
    n = w_qkv.shape[1]
    tm = _pick(rows, (768, 512, 640, 256))
    tn = _pick(n_kv * HEAD_DIM, (512, 256, 128))
    gain = jnp.concatenate([jnp.tile(q_gain * HEAD_DIM ** -0.5, n_q), jnp.tile(k_gain, n_kv),
                            jnp.ones((n_kv * HEAD_DIM,), F32)]).reshape(1, n)
    extras = [(gain, (1, tn), lambda i, j: (0, j))]
    if rope_tables is not None:
        extras += [(t, (tm, HEAD_DIM), lambda i, j: (i, 0)) for t in rope_tables]
    epi = functools.partial(_epi_qkv, n_norm_tiles=(n_q + n_kv) * HEAD_DIM // tn, rope=rope_tables is not None, tn=tn)
    return _mm(u, w_qkv, rows=rows, tm=tm, tn=tn, out_cols=n, out_dtype=BF16, col_maps=(lambda j: j,),
               epilogue=epi, extras=tuple(extras), w_outer=True, name="qkv_proj")


def _softmax_pv(scores, values, extra_logit=None):
    m = functools.reduce(jnp.maximum, [jnp.max(s, axis=-1, keepdims=True) for s in scores])
    if extra_logit is not None:
        m = jnp.maximum(m, extra_logit)
    ps = [jnp.exp(s - m) for s in scores]
    den = functools.reduce(jnp.add, [jnp.sum(p, axis=-1, keepdims=True) for p in ps])
    if extra_logit is not None:
        den = den + jnp.exp(extra_logit - m)
    o = functools.reduce(jnp.add, [_dot(p.astype(BF16), v) for p, v in zip(ps, values)])
    return o / den


def _swa_kernel(sink_ref, q_ref, kp_ref, kc_ref, kn_ref, vp_ref, vc_ref, vn_ref, kx_ref, vx_ref, o_ref, *, n_lat_blocks):
    h, b = pl.program_id(0), pl.program_id(1)
    blk = SWA_WINDOW
    is_lat = b < n_lat_blocks
    qi = lax.broadcasted_iota(jnp.int32, (blk, 1), 0)
    kj = lax.broadcasted_iota(jnp.int32, (1, blk), 1)
    off = lambda ok: jnp.where(ok, 0, blk + 1)
    m_prev = kj >= qi + off(is_lat & (b > 0))
    m_cur = kj >= qi * 0 + off(is_lat)
    m_next = kj <= qi - off(is_lat & (b < n_lat_blocks - 1))
    kp, kc, kn, kx = kp_ref[...], kc_ref[...], kn_ref[...], kx_ref[...]
    vals = [vp_ref[...], vc_ref[...], vn_ref[...], vx_ref[...]]
    for g in range(SWA_GROUP):
        cols = slice(g * HEAD_DIM, (g + 1) * HEAD_DIM)
        q = q_ref[:, cols]
        scores = [jnp.where(m_prev, _dot_nt(q, kp), NEG_INF), jnp.where(m_cur, _dot_nt(q, kc), NEG_INF),
                  jnp.where(m_next, _dot_nt(q, kn), NEG_INF), _dot_nt(q, kx)]
        o_ref[:, cols] = _softmax_pv(scores, vals, sink_ref[h * SWA_GROUP + g]).astype(o_ref.dtype)


def _swa_attention(qkv, sinks, *, n_q, n_kv, rows, n_lat):
    blk = SWA_WINDOW
    nlb = n_lat // blk
    ctx_rows = (qkv.shape[0] - n_lat)
    kcol = lambda h: n_q + h
    vcol = lambda h: n_q + n_kv + h
    prev = lambda b: jnp.clip(b - 1, 0, nlb - 1)
    cur = lambda b: jnp.minimum(b, nlb - 1)
    nxt = lambda b: jnp.clip(b + 1, 0, nlb - 1)
    kv = lambda rowf, colf: pl.BlockSpec((blk, HEAD_DIM), lambda h, b: (rowf(b), colf(h)))
    ctx = lambda colf: pl.BlockSpec((ctx_rows, HEAD_DIM), lambda h, b: (n_lat // ctx_rows, colf(h)))
    return pl.pallas_call(
        functools.partial(_swa_kernel, n_lat_blocks=nlb),
        grid=(n_kv, rows // blk),
        in_specs=[
            pl.BlockSpec(memory_space=pltpu.SMEM),
            pl.BlockSpec((blk, SWA_GROUP * HEAD_DIM), lambda h, b: (b, h)),
            kv(prev, kcol), kv(cur, kcol), kv(nxt, kcol), kv(prev, vcol), kv(cur, vcol), kv(nxt, vcol),
            ctx(kcol), ctx(vcol),
        ],
        out_specs=pl.BlockSpec((blk, SWA_GROUP * HEAD_DIM), lambda h, b: (b, h)),
        out_shape=jax.ShapeDtypeStruct((rows, n_q * HEAD_DIM), BF16),
        compiler_params=_params(("arbitrary", "arbitrary")),
        name="swa_attention",
    )(sinks, *([qkv] * 9))


def _na_bias_kernel(rpb_ref, o_ref):
    h = pl.program_id(0)
    n_ri, n_ci = 2 * NA_WIN_ROWS - 1, 2 * NA_WIN_COLS - 1
    cq = lax.broadcasted_iota(jnp.int32, (GRID_W, 2 * GRID_W), 0)
    ck = lax.broadcasted_iota(jnp.int32, (GRID_W, 2 * GRID_W), 1)
    second = ck >= GRID_W
    ck = jnp.where(second, ck - GRID_W, ck)
    c0 = jnp.clip(cq - NA_WIN_COLS // 2, 0, GRID_W - NA_WIN_COLS)
    in_win = (ck >= c0) & (ck < c0 + NA_WIN_COLS)
    cidx = jnp.clip(ck - cq + NA_WIN_COLS - 1, 0, n_ci - 1)
    for k in range(n_ri + 1):
        tile = jnp.full((GRID_W, 2 * GRID_W), NEG_INF, F32)
        for half, ri in ((False, k - 1), (True, k)):
            if 0 <= ri < n_ri:
                sel = in_win & (second if half else jnp.logical_not(second))
                for dci in range(n_ci):
                    tile = jnp.where(sel & (cidx == dci), rpb_ref[(h * n_ri + ri) * n_ci + dci], tile)
        o_ref[0, k] = tile


def _na_bias(rpb):
    n_heads = rpb.shape[0]
    n_tiles = 2 * NA_WIN_ROWS
    return pl.pallas_call(
        _na_bias_kernel,
        grid=(n_heads,),
        in_specs=[pl.BlockSpec(memory_space=pltpu.SMEM)],
        out_specs=pl.BlockSpec((1, n_tiles, GRID_W, 2 * GRID_W), lambda h: (h, 0, 0, 0)),
        out_shape=jax.ShapeDtypeStruct((n_heads, n_tiles, GRID_W, 2 * GRID_W), F32),
        compiler_params=_params(("arbitrary",)),
        name="na_bias",
    )(rpb.reshape(-1))


def _na_kernel(q_ref, kp_ref, kc_ref, kn_ref, vp_ref, vc_ref, vn_ref, kx_ref, vx_ref, bias_ref, o_ref, *, n_grid_rows):
    b = pl.program_id(1)
    qr, w = NA_QROWS, GRID_W
    is_lat = b * qr < n_grid_rows
    q = q_ref[...]
    lane = lax.broadcasted_iota(jnp.int32, (1, 2 * w), 1)
    blocks = []
    for kb, k_ref in enumerate((kp_ref, kc_ref, kn_ref)):
        s = _dot_nt(q, k_ref[...])
        rows = []
        for dq in range(qr):
            r = b * qr + dq
            r0 = jnp.clip(r - NA_WIN_ROWS // 2, 0, n_grid_rows - NA_WIN_ROWS)
            tiles = []
            for jt in range(qr // 2):
                kr = b * qr + (kb - 1) * qr + 2 * jt
                ok0 = (kr >= r0) & (kr < r0 + NA_WIN_ROWS) & is_lat
                ok1 = (kr + 1 >= r0) & (kr + 1 < r0 + NA_WIN_ROWS) & is_lat
                ri0 = (kb - 1) * qr + 2 * jt - dq + NA_WIN_ROWS - 1
                t = s[dq * w:(dq + 1) * w, jt * 2 * w:(jt + 1) * 2 * w] + bias_ref[0, ri0 + 1]
                lo, hi = jnp.where(ok0, 0, w), jnp.where(ok1, 2 * w, w)
                tiles.append(jnp.where((lane >= lo) & (lane < hi), t, NEG_INF))
            rows.append(jnp.concatenate(tiles, axis=1))
        blocks.append(jnp.concatenate(rows, axis=0))
    blocks.append(_dot_nt(q, kx_ref[...]))
    vals = [vp_ref[...], vc_ref[...], vn_ref[...], vx_ref[...]]
    o_ref[...] = _softmax_pv(blocks, vals).astype(o_ref.dtype)


def _na_attention(qkv, bias, *, n_heads, rows, n_lat):
    blk = NA_QROWS * GRID_W
    assert blk == qkv.shape[0] - n_lat
    nlb = n_lat // blk
    kcol = lambda h: n_heads + h
    vcol = lambda h: 2 * n_heads + h
    prev = lambda b: jnp.clip(b - 1, 0, nlb - 1)
    cur = lambda b: jnp.minimum(b, nlb - 1)
    nxt = lambda b: jnp.clip(b + 1, 0, nlb - 1)
    spec = lambda rowf, colf: pl.BlockSpec((blk, HEAD_DIM), lambda h, b: (rowf(b), colf(h)))
    ctx = lambda b: nlb
    return pl.pallas_call(
        functools.partial(_na_kernel, n_grid_rows=n_lat // GRID_W),
        grid=(n_heads, rows // blk),
        in_specs=[
            spec(lambda b: b, lambda h: h),
            spec(prev, kcol), spec(cur, kcol), spec(nxt, kcol), spec(prev, vcol), spec(cur, vcol), spec(nxt, vcol),
            spec(ctx, kcol), spec(ctx, vcol),
            pl.BlockSpec((1,) + bias.shape[1:], lambda h, b: (h, 0, 0, 0)),
        ],
        out_specs=spec(lambda b: b, lambda h: h),
        out_shape=jax.ShapeDtypeStruct((rows, n_heads * HEAD_DIM), BF16),
        compiler_params=_params(("arbitrary", "arbitrary")),
        name="na_attention",
    )(*([qkv] * 9), bias)


def kernel(x, c, ctx, c_ctx, ada_down, ada_up, ada_b, norm_mix, norm_ffn, ffn_w_gate_up, ffn_w_down, pool_w, pool_scale, s5_lam_re, s5_lam_im, s5_log_step, s5_b_re, s5_b_im, s5_c_re, s5_c_im, s5_d, s5_w_glu, swa_w_qkv, swa_w_o, swa_q_gain, swa_k_gain, swa_sinks, na_w_qkv, na_w_o, na_q_gain, na_k_gain, na_rpb):
    bsz, n_lat, d = x.shape
    n_ctx = ctx.shape[1]
    assert bsz == 1 and n_lat % ROW_TILE == 0 and n_ctx == ROW_TILE
    n_tot = n_lat + n_ctx
    depth = ada_down.shape[0]
    d_ff = ffn_w_down.shape[1]
    n_heads = d // HEAD_DIM

    xs = jnp.concatenate([x[0], ctx[0]], axis=0)
    mods = _adaln_all(c, c_ctx, ada_down, ada_up, ada_b)

    for layer in range(depth):
        kind = layer % 4
        last = layer == depth - 1
        rows = n_lat if last else n_tot
        mod = mods[layer]
        res = dict(rows=rows, n_lat=n_lat)

        if kind == 0:
            xs = _pool_layer(xs, norm_mix[layer], mod, pool_w, pool_scale, **res)
        else:
            u = _norm_mod(xs, norm_mix[layer], mod, shift_idx=0, rows=n_tot, n_lat=n_lat)
            if kind == 1:
                a_re, a_im, bb_re, bb_im = _s5_prep(s5_lam_re, s5_lam_im, s5_log_step, s5_b_re, s5_b_im)
                y = _s5_scan(u, *_s5_pack(a_re, a_im, bb_re, bb_im, s5_c_re, s5_c_im), n_lat=n_lat)
                g = _s5_gelu(y, u, s5_d, rows=rows)
                half = s5_w_glu.shape[1] // 2
                tm = _pick(rows, (768, 512, 640, 256))
                tn = _pick(half, (256, 128))
                xs = _mm(g, s5_w_glu, rows=rows, tm=tm, tn=tn, out_cols=half, out_dtype=F32,
                         col_maps=(lambda j: j, lambda j, o=half // tn: j + o),
                         epilogue=functools.partial(_epi_glu_residual, gate_idx=2, tm=tm, n_lat=n_lat),
                         extras=_residual_extras(xs, mod, tm, tn), w_outer=True, name="s5_glu")
            elif kind == 2:
                n_kv = n_heads // SWA_GROUP
                qkv = _qkv_proj(u, swa_w_qkv, swa_q_gain, swa_k_gain, n_q=n_heads, n_kv=n_kv,
                                rope_tables=_rope_tables(n_lat, n_ctx), rows=n_tot)
                o = _swa_attention(qkv, swa_sinks, n_q=n_heads, n_kv=n_kv, **res)
                xs = _mm_residual(o, swa_w_o, xs, mod, gate_idx=2, w_outer=True, name="swa_out", **res)
            else:
                qkv = _qkv_proj(u, na_w_qkv, na_q_gain, na_k_gain, n_q=n_heads, n_kv=n_heads,
                                rope_tables=None, rows=n_tot)
                o = _na_attention(qkv, _na_bias(na_rpb), n_heads=n_heads, **res)
                xs = _mm_residual(o, na_w_o, xs, mod, gate_idx=2, w_outer=True, name="na_out", **res)

        hx = _norm_mod(xs, norm_ffn[layer], mod, shift_idx=3, **res)
        tm = _pick(rows, (768, 512, 640, 256))
        tn = _pick(d_ff, (256, 128))
        hh = _mm(hx, ffn_w_gate_up[layer], rows=rows, tm=tm, tn=tn, out_cols=d_ff, out_dtype=BF16,
                 col_maps=(lambda j: j, lambda j, o=d_ff // tn: j + o), epilogue=_epi_swiglu,
                 w_outer=True, name="ffn_gate_up")
        xs = _mm_residual(hh, ffn_w_down[layer].astype(BF16), xs, mod, gate_idx=5, w_outer=False, name="ffn_down", **res)

    return xs[:n_lat][None]
```

```python
import functools
import math

import jax
import jax.numpy as jnp
from jax import lax
from jax.experimental import pallas as pl
from jax.experimental.pallas import tpu as pltpu

F32 = jnp.float32
BF16 = jnp.bfloat16

RMS_EPS = 1e-6
NEG_INF = -1e30
HEAD_DIM = 128
GRID_W = 64
ROPE_BASE = 10000.0
POOL_WINDOWS = (2, 4, 8, 16)
POOL_HALO = 8
S5_GROUP = 16
S5_STATE = 64
S5_CHUNK = 256
S5_ROW_PAD = 8
SWA_WINDOW = 128
SWA_GROUP = 4
NA_WIN_ROWS = 8
NA_WIN_COLS = 16
NA_QROWS = 4

LANES = 128
ROW_TILE = 256
V7X_VMEM_LIMIT = 56 * 1024 * 1024


def _pick(n, candidates):
    for c in candidates:
        if n % c == 0:
            return c
    raise ValueError(f"no tile for {n} in {candidates}")


def _params(sem, vmem=V7X_VMEM_LIMIT):
    return pltpu.CompilerParams(dimension_semantics=sem, vmem_limit_bytes=vmem)


def _dot(a, b):
    return jnp.dot(a, b, preferred_element_type=F32)


def _dot_nt(a, b):
    return lax.dot_general(a, b, (((1,), (1,)), ((), ())), preferred_element_type=F32)


def _small_mm_kernel(a_ref, w_ref, b_ref, o_ref, *, silu_in, kc):
    a = a_ref[0]
    if silu_in:
        a = a * jax.nn.sigmoid(a)
    a = a.astype(BF16)
    k = a.shape[1]
    acc = jnp.zeros(o_ref.shape[1:], F32)
    for k0 in range(0, k, kc):
        acc += _dot(a[:, k0:k0 + kc], w_ref[0, k0:k0 + kc, :].astype(BF16))
    o_ref[0] = acc + b_ref[0]


def _small_mm(a, w, b, *, silu_in, tn):
    depth, k, n = w.shape
    shared = a.shape[0] == 1
    kern = functools.partial(_small_mm_kernel, silu_in=silu_in, kc=min(k, 512))
    return pl.pallas_call(
        kern,
        grid=(depth, n // tn),
        in_specs=[
            pl.BlockSpec((1, 8, k), lambda l, j: (0 if shared else l, 0, 0)),
            pl.BlockSpec((1, k, tn), lambda l, j: (l, 0, j)),
            pl.BlockSpec((1, 1, tn), lambda l, j: (l, 0, j)),
        ],
        out_specs=pl.BlockSpec((1, 8, tn), lambda l, j: (l, 0, j)),
        out_shape=jax.ShapeDtypeStruct((depth, 8, n), F32),
        compiler_params=_params(("arbitrary", "arbitrary")),
        name="adaln_mm",
    )(a, w, b)


def _adaln_all(c, c_ctx, ada_down, ada_up, ada_b):
    depth, d, rank = ada_down.shape
    cvec = jnp.zeros((1, 8, d), F32).at[0, 0].set(c[0]).at[0, 1].set(c_ctx)
    t = _small_mm(cvec, ada_down, jnp.zeros((depth, 1, rank), F32), silu_in=True, tn=_pick(rank, (512, 256, 128)))
    m = _small_mm(t, ada_up, ada_b.reshape(depth, 1, 6 * d), silu_in=False, tn=_pick(6 * d, (2048, 1024, 512)))
    return m[:, :2].reshape(depth, 2, 6, d)


def _rms_mod(x, gain, shift, scale):
    ms = jnp.mean(x * x, axis=-1, keepdims=True)
    return (x * lax.rsqrt(ms + RMS_EPS) * gain) * (1.0 + scale) + shift


def _norm_mod_kernel(x_ref, g_ref, mod_ref, o_ref, *, shift_idx):
    u = _rms_mod(x_ref[...], g_ref[...], mod_ref[0, shift_idx:shift_idx + 1, :],
                 mod_ref[0, shift_idx + 1:shift_idx + 2, :])
    o_ref[...] = u.astype(o_ref.dtype)


def _norm_mod(xs, gain, mod, *, shift_idx, rows, n_lat):
    d = xs.shape[1]
    tr = ROW_TILE
    return pl.pallas_call(
        functools.partial(_norm_mod_kernel, shift_idx=shift_idx),
        grid=(rows // tr,),
        in_specs=[
            pl.BlockSpec((tr, d), lambda i: (i, 0)),
            pl.BlockSpec((1, d), lambda i: (0, 0)),
            pl.BlockSpec((1, 6, d), lambda i: (jnp.where(i * tr >= n_lat, 1, 0), 0, 0)),
        ],
        out_specs=pl.BlockSpec((tr, d), lambda i: (i, 0)),
        out_shape=jax.ShapeDtypeStruct((rows, d), BF16),
        compiler_params=_params(("arbitrary",)),
        name="norm_mod",
    )(xs, gain.reshape(1, d), mod)


def _mm(a, w, *, rows, tm, tn, out_cols, out_dtype, col_maps, epilogue, extras=(), w_outer, name,
        layer=None, to_bf16=None):
    k = a.shape[1]
    n_i, n_j = rows // tm, out_cols // tn
    cast = w.dtype != BF16
    assert w_outer or not cast
    n_parts = len(col_maps)
    kc = _pick(k, (512, 256, 128))

    if w_outer:
        grid = (n_j, n_i)
        ij = lambda g0, g1: (g1, g0)
    else:
        grid = (n_i, n_j)
        ij = lambda g0, g1: (g0, g1)

    if w.ndim == 3:
        w_block, w_index = (None, k, tn), lambda col: (layer, 0, col)
    else:
        w_block, w_index = (k, tn), lambda col: (0, col)
    in_specs = [pl.BlockSpec((tm, k), lambda g0, g1: (ij(g0, g1)[0], 0))]
    for cm in col_maps:
        in_specs.append(pl.BlockSpec(w_block, lambda g0, g1, cm=cm: w_index(cm(ij(g0, g1)[1]))))
    for _, shape, fn in extras:
        in_specs.append(pl.BlockSpec(shape, lambda g0, g1, fn=fn: fn(*ij(g0, g1))))
    n_ex = len(extras)
    out_specs = [pl.BlockSpec((tm, tn), lambda g0, g1: ij(g0, g1))]
    out_shape = [jax.ShapeDtypeStruct((rows, out_cols), out_dtype)]
    n_side = 0 if to_bf16 is None else 1
    if n_side:
        _, side_r, side_c = to_bf16.shape
        slab = side_r // (grid[0] * grid[1])
        assert slab * grid[0] * grid[1] == side_r and slab % 16 == 0
        in_specs.append(pl.BlockSpec((None, slab, side_c), lambda g0, g1: (layer, g0 * grid[1] + g1, 0)))
        out_specs.append(pl.BlockSpec((slab, side_c), lambda g0, g1: (g0 * grid[1] + g1, 0)))
        out_shape.append(jax.ShapeDtypeStruct((side_r, side_c), BF16))

    def body(*refs):
        a_ref = refs[0]
        w_refs = refs[1:1 + n_parts]
        ex_refs = refs[1 + n_parts:1 + n_parts + n_ex]
        n_in = 1 + n_parts + n_ex + n_side
        o_ref = refs[n_in]
        scratch = refs[n_in + 1 + n_side:]
        i, j = ij(pl.program_id(0), pl.program_id(1))
        if n_side:
            refs[n_in + 1][...] = refs[n_in - 1][...].astype(BF16)
        if cast:
            @pl.when(i == 0)
            def _():
                def cp(c, carry):
                    sl = pl.ds(pl.multiple_of(c * kc, kc), kc)
                    for p in range(n_parts):
                        scratch[p][sl, :] = w_refs[p][sl, :].astype(BF16)
                    return carry
                lax.fori_loop(0, k // kc, cp, 0)
            ws = scratch
        else:
            ws = w_refs
        av = a_ref[...]
        accs = [_dot(av, ws[p][...]) for p in range(n_parts)]
        epilogue(accs, i, j, ex_refs, o_ref)

    outs = pl.pallas_call(
        body,
        grid=grid,
        in_specs=in_specs,
        out_specs=out_specs,
        out_shape=out_shape,
        scratch_shapes=[pltpu.VMEM((k, tn), BF16) for _ in range(n_parts)] if cast else [],
        compiler_params=_params(("arbitrary", "arbitrary")),
        name=name,
    )(a, *([w] * n_parts), *[e[0] for e in extras], *([to_bf16] if n_side else []))
    return outs if n_side else outs[0]


def _row_gate(mod_ref, idx, i, tm, n_lat):
    rows = i * tm + lax.broadcasted_iota(jnp.int32, (tm, 1), 0)
    return jnp.where(rows < n_lat, mod_ref[0, idx:idx + 1, :], mod_ref[1, idx:idx + 1, :])


def _epi_swiglu(accs, i, j, ex, o_ref):
    a, g = accs
    o_ref[...] = (a * jax.nn.sigmoid(a) * g).astype(o_ref.dtype)


def _epi_residual(accs, i, j, ex, o_ref, *, gate_idx, tm, n_lat):
    res_ref, mod_ref = ex
    o_ref[...] = res_ref[...] + _row_gate(mod_ref, gate_idx, i, tm, n_lat) * accs[0]


def _epi_glu_residual(accs, i, j, ex, o_ref, *, gate_idx, tm, n_lat):
    res_ref, mod_ref = ex
    a, g = accs
    o_ref[...] = res_ref[...] + _row_gate(mod_ref, gate_idx, i, tm, n_lat) * (a * jax.nn.sigmoid(g))


def _residual_extras(xs, mod, tm, tn):
    return ((xs, (tm, tn), lambda i, j: (i, j)), (mod, (2, 6, tn), lambda i, j: (0, 0, j)))


def _mm_residual(a, w, xs, mod, *, gate_idx, rows, n_lat, w_outer, name):
    n = w.shape[1]
    tm = _pick(rows, (768, 512, 640, 256))
    tn = _pick(n, (512, 256) if w_outer else (256, 128))
    return _mm(a, w, rows=rows, tm=tm, tn=tn, out_cols=n, out_dtype=F32, col_maps=(lambda j: j,),
               epilogue=functools.partial(_epi_residual, gate_idx=gate_idx, tm=tm, n_lat=n_lat),
               extras=_residual_extras(xs, mod, tm, tn), w_outer=w_outer, name=name)


def _pool_kernel(xp_ref, x_ref, xn_ref, g_ref, mod_ref, pw_ref, ps_ref, o_ref, *, tr, n_lat, n_tot):
    i = pl.program_id(0)
    halo = POOL_HALO
    gain = g_ref[...]
    shift, scale, gate = mod_ref[0, 0:1, :], mod_ref[0, 1:2, :], mod_ref[0, 2:3, :]
    x = x_ref[...]
    ue = jnp.concatenate([_rms_mod(xp_ref[...], gain, shift, scale), _rms_mod(x, gain, shift, scale),
                          _rms_mod(xn_ref[...], gain, shift, scale)], axis=0)
    is_lat = i * tr < n_lat
    seq_lo = jnp.where(is_lat, 0, n_lat)
    seq_hi = jnp.where(is_lat, n_lat, n_tot)
    t_g = i * tr + lax.broadcasted_iota(jnp.int32, (tr, 1), 0)
    s_g = i * tr - halo + lax.broadcasted_iota(jnp.int32, (1, tr + 2 * halo), 1)
    s_ok = (s_g >= seq_lo) & (s_g < seq_hi)
    pg = ue.shape[1] // len(POOL_WINDOWS)
    for gi, w in enumerate(POOL_WINDOWS):
        cols = slice(gi * pg, (gi + 1) * pg)
        ug = ue[:, cols]
        band = (s_g >= t_g - w // 2) & (s_g < t_g + w // 2) & s_ok
        ssum = _dot(jnp.where(band, 1.0, 0.0).astype(BF16), ug.astype(BF16))
        cnt = jnp.minimum(t_g + w // 2, seq_hi) - jnp.maximum(t_g - w // 2, seq_lo)
        p = ssum / cnt.astype(F32) - ug[halo:halo + tr]
        y = _dot(p.astype(BF16), pw_ref[gi]) * ps_ref[:, cols]
        o_ref[:, cols] = x[:, cols] + gate[:, cols] * y


def _pool_layer(xs, gain, mod, pool_w, pool_scale, *, rows, n_lat):
    n_tot, d = xs.shape
    tr, halo = ROW_TILE, POOL_HALO
    hb = tr // halo
    last_hb = n_tot // halo - 1
    return pl.pallas_call(
        functools.partial(_pool_kernel, tr=tr, n_lat=n_lat, n_tot=n_tot),
        grid=(rows // tr,),
        in_specs=[
            pl.BlockSpec((halo, d), lambda i: (jnp.maximum(i * hb - 1, 0), 0)),
            pl.BlockSpec((tr, d), lambda i: (i, 0)),
            pl.BlockSpec((halo, d), lambda i: (jnp.minimum((i + 1) * hb, last_hb), 0)),
            pl.BlockSpec((1, d), lambda i: (0, 0)),
            pl.BlockSpec((1, 6, d), lambda i: (jnp.where(i * tr >= n_lat, 1, 0), 0, 0)),
            pl.BlockSpec(pool_w.shape, lambda i: (0, 0, 0)),
            pl.BlockSpec((1, d), lambda i: (0, 0)),
        ],
        out_specs=pl.BlockSpec((tr, d), lambda i: (i, 0)),
        out_shape=jax.ShapeDtypeStruct((rows, d), F32),
        compiler_params=_params(("arbitrary",)),
        name="pool_mixer",
    )(xs, xs, xs, gain.reshape(1, d), mod, pool_w.astype(BF16), pool_scale.reshape(1, d))


def _s5_prep_kernel(lre_ref, lim_ref, ls_ref, bre_ref, bim_ref, are_ref, aim_ref, bbre_ref, bbim_ref):
    lam_re, lam_im = lre_ref[0], lim_ref[0]
    dt = jnp.exp(ls_ref[0])
    z_re, z_im = lam_re * dt, lam_im * dt
    mag = jnp.exp(z_re)
    a_re, a_im = mag * jnp.cos(z_im), mag * jnp.sin(z_im)
    den = lam_re * lam_re + lam_im * lam_im
    k_re = ((a_re - 1.0) * lam_re + a_im * lam_im) / den
    k_im = (a_im * lam_re - (a_re - 1.0) * lam_im) / den
    are_ref[0] = a_re
    aim_ref[0] = a_im
    b_re, b_im = bre_ref[0], bim_ref[0]
    bbre_ref[0] = k_re[:, None, :] * b_re - k_im[:, None, :] * b_im
    bbim_ref[0] = k_re[:, None, :] * b_im + k_im[:, None, :] * b_re


def _s5_prep(lam_re, lam_im, log_step, b_re, b_im):
    _, g, p, h = b_re.shape
    spec2 = pl.BlockSpec((1, g, p), lambda d: (d, 0, 0))
    spec3 = pl.BlockSpec((1, g, h, p), lambda d: (d, 0, 0, 0))
    return pl.pallas_call(
        _s5_prep_kernel,
        grid=(2,),
        in_specs=[spec2, spec2, pl.BlockSpec((1, g, 1), lambda d: (d, 0, 0)), spec3, spec3],
        out_specs=[spec2, spec2, spec3, spec3],
        out_shape=[jax.ShapeDtypeStruct((2, g, p), F32)] * 2 + [jax.ShapeDtypeStruct((2, g, h, p), F32)] * 2,
        compiler_params=_params(("arbitrary",)),
        name="s5_prep",
    )(lam_re, lam_im, log_step[..., None], jnp.swapaxes(b_re, 2, 3), jnp.swapaxes(b_im, 2, 3))


def _s5_pack(a_re, a_im, bb_re, bb_im, c_re, c_im):
    _, g, h, p = bb_re.shape
    r = g // 2
    eye2 = jnp.eye(2, dtype=F32)
    eye4 = jnp.eye(4, dtype=F32)
    slot = eye4[jnp.arange(r) % 4]

    def win_part(bb):
        t = bb.reshape(2, r, 2, h, p)
        t = t[:, :, :, :, None, :] * eye2[None, None, :, None, :, None]
        t = t.reshape(2, r, 2 * h, 2 * p)
        t = slot[None, :, :, None, None] * t[:, :, None, :, :]
        return t.reshape(2, r, 4 * 2 * h, 2 * p)

    def c_part(c):
        t = jnp.swapaxes(c, 2, 3).reshape(2, r, 2, p, h)
        t = t[:, :, :, :, None, :] * eye2[None, None, :, None, :, None]
        t = t.reshape(2, r, 2 * p, 2 * h)
        t = t[:, :, :, None, :] * slot[None, :, None, :, None]
        return t.reshape(2, r, 2 * p, 4 * 2 * h)

    win = jnp.concatenate([win_part(bb_re), win_part(bb_im)], axis=-1).astype(BF16)
    cre = c_part(c_re).astype(BF16)
    cim = c_part(-c_im).astype(BF16)
    return win, cre, cim, a_re.reshape(2, r, 2 * p), a_im.reshape(2, r, 2 * p)


def _s5_scan_kernel(u_ref, win_ref, cre_ref, cim_ref, are_ref, aim_ref, y_ref, xre, xim, hre, him, *, nsub, rb):
    g0, c = pl.program_id(0), pl.program_id(1)
    rev = g0 // nsub == 1
    tc, pitch = S5_CHUNK, rb + S5_ROW_PAD
    block_rows = lambda r: pl.ds(r, tc, stride=pitch)

    @pl.when(c == 0)
    def _():
        hre[...] = jnp.zeros_like(hre)
        him[...] = jnp.zeros_like(him)

    for q in range(rb // 4):
        ut = u_ref[:, q * LANES:(q + 1) * LANES]
        for s in range(4):
            r = q * 4 + s
            x = _dot(ut, win_ref[0, r])
            xre[block_rows(r), :] = x[:, :LANES]
            xim[block_rows(r), :] = x[:, LANES:]

    a_re, a_im = are_ref[0], aim_ref[0]

    def step(t, carry):
        h_re, h_im = carry
        tt = jnp.where(rev, tc - 1 - t, t)
        rows = pl.ds(pl.multiple_of(tt * pitch, 8), rb)
        n_re = a_re * h_re - a_im * h_im + xre[rows, :]
        n_im = a_re * h_im + a_im * h_re + xim[rows, :]
        xre[rows, :] = n_re
        xim[rows, :] = n_im
        return n_re, n_im

    h_re, h_im = lax.fori_loop(0, tc, step, (hre[...], him[...]), unroll=8)
    hre[...] = h_re
    him[...] = h_im

    for q in range(rb // 4):
        acc = jnp.zeros((tc, LANES), F32)
        for s in range(4):
            r = q * 4 + s
            acc += (_dot(xre[block_rows(r), :].astype(BF16), cre_ref[0, r])
                    + _dot(xim[block_rows(r), :].astype(BF16), cim_ref[0, r]))
        y_ref[0, :, q * LANES:(q + 1) * LANES] = acc


def _s5_scan(u, win, cre, cim, a_re, a_im, *, n_lat):
    n_tot, d = u.shape
    r_all = win.shape[1]
    rb = min(64, r_all)
    nsub = r_all // rb
    tc = S5_CHUNK
    n_c, n_lc = n_tot // tc, n_lat // tc
    cols = rb * 2 * S5_GROUP

    def split(w):
        return w.reshape((2 * nsub, rb) + w.shape[2:])

    def chunk(g0, c):
        return jnp.where(g0 // nsub == 0, (c + n_lc) % n_c, n_c - 1 - c)

    wspec = lambda shape: pl.BlockSpec((1,) + shape, lambda g0, c: (g0,) + (0,) * len(shape))
    return pl.pallas_call(
        functools.partial(_s5_scan_kernel, nsub=nsub, rb=rb),
        grid=(2 * nsub, n_c),
        in_specs=[
            pl.BlockSpec((tc, cols), lambda g0, c: (chunk(g0, c), g0 % nsub)),
            wspec((rb, LANES, 2 * LANES)), wspec((rb, LANES, LANES)), wspec((rb, LANES, LANES)),
            wspec((rb, LANES)), wspec((rb, LANES)),
        ],
        out_specs=pl.BlockSpec((1, tc, cols), lambda g0, c: (g0 // nsub, chunk(g0, c), g0 % nsub)),
        out_shape=jax.ShapeDtypeStruct((2, n_tot, d), F32),
        scratch_shapes=[pltpu.VMEM((tc * (rb + S5_ROW_PAD), LANES), F32)] * 2 + [pltpu.VMEM((rb, LANES), F32)] * 2,
        compiler_params=_params(("arbitrary", "arbitrary")),
        name="s5_scan",
    )(u, split(win), split(cre), split(cim), split(a_re), split(a_im))


def _s5_gelu_kernel(y_ref, u_ref, d_ref, o_ref):
    v = y_ref[0] + y_ref[1] + d_ref[...] * u_ref[...].astype(F32)
    o_ref[...] = jax.nn.gelu(v, approximate=True).astype(o_ref.dtype)


def _s5_gelu(y, u, dvec, *, rows):
    d = u.shape[1]
    tr = ROW_TILE
    return pl.pallas_call(
        _s5_gelu_kernel,
        grid=(rows // tr,),
        in_specs=[pl.BlockSpec((2, tr, d), lambda i: (0, i, 0)), pl.BlockSpec((tr, d), lambda i: (i, 0)),
                  pl.BlockSpec((1, d), lambda i: (0, 0))],
        out_specs=pl.BlockSpec((tr, d), lambda i: (i, 0)),
        out_shape=jax.ShapeDtypeStruct((rows, d), BF16),
        compiler_params=_params(("arbitrary",)),
        name="s5_gelu",
    )(y, u, dvec.reshape(1, d))


def _rope_tables(n_lat, n_ctx):
    quarter = HEAD_DIM // 4
    inv = jnp.power(ROPE_BASE, -jnp.arange(quarter, dtype=F32) / quarter)
    t = jnp.arange(n_lat)
    ang_r = (t // GRID_W).astype(F32)[:, None] * inv
    ang_c = (t % GRID_W).astype(F32)[:, None] * inv
    ang = jnp.concatenate([ang_r, ang_r, ang_c, ang_c], axis=-1)
    sign = jnp.where((jnp.arange(HEAD_DIM) % (2 * quarter)) < quarter, -1.0, 1.0)
    cos = jnp.concatenate([jnp.cos(ang), jnp.ones((n_ctx, HEAD_DIM), F32)], axis=0)
    sin = jnp.concatenate([jnp.sin(ang) * sign, jnp.zeros((n_ctx, HEAD_DIM), F32)], axis=0)
    return cos, sin


def _epi_qkv(accs, i, j, ex, o_ref, *, n_norm_tiles, rope, tn):
    acc = accs[0]
    gain_ref = ex[0]

    @pl.when(j < n_norm_tiles)
    def _():
        quarter = HEAD_DIM // 4
        lane = lax.broadcasted_iota(jnp.int32, (1, HEAD_DIM), 1)
        first = (lane % (2 * quarter)) < quarter
        for h in range(tn // HEAD_DIM):
            cols = slice(h * HEAD_DIM, (h + 1) * HEAD_DIM)
            x = acc[:, cols]
            ms = jnp.mean(x * x, axis=-1, keepdims=True)
            xn = x * lax.rsqrt(ms + RMS_EPS) * gain_ref[:, cols]
            if rope:
                cos_ref, sin_ref = ex[1], ex[2]
                partner = jnp.where(first, pltpu.roll(xn, HEAD_DIM - quarter, 1), pltpu.roll(xn, quarter, 1))
                xn = xn * cos_ref[...] + partner * sin_ref[...]
            o_ref[:, cols] = xn.astype(o_ref.dtype)

    @pl.when(j >= n_norm_tiles)
    def _():
        o_ref[...] = acc.astype(o_ref.dtype)


def _qkv_proj(u, w_qkv, q_gain, k_gain, *, n_q, n_kv, rope_tables, rows):
    n = w_qkv.shape[1]
    tm = _pick(rows, (768, 512, 640, 256))
    tn = _pick(n_kv * HEAD_DIM, (512, 256, 128))
    gain = jnp.concatenate([jnp.tile(q_gain * HEAD_DIM ** -0.5, n_q), jnp.tile(k_gain, n_kv),
                            jnp.ones((n_kv * HEAD_DIM,), F32)]).reshape(1, n)
    extras = [(gain, (1, tn), lambda i, j: (0, j))]
    if rope_tables is not None:
        extras += [(t, (tm, HEAD_DIM), lambda i, j: (i, 0)) for t in rope_tables]
    epi = functools.partial(_epi_qkv, n_norm_tiles=(n_q + n_kv) * HEAD_DIM // tn, rope=rope_tables is not None, tn=tn)
    return _mm(u, w_qkv, rows=rows, tm=tm, tn=tn, out_cols=n, out_dtype=BF16, col_maps=(lambda j: j,),
               epilogue=epi, extras=tuple(extras), w_outer=True, name="qkv_proj")


def _softmax_pv(scores, values, extra_logit=None):
    lane_tiles = lambda blocks: [b[:, k * LANES:(k + 1) * LANES] for b in blocks for k in range(b.shape[1] // LANES)]
    m = jnp.max(functools.reduce(jnp.maximum, lane_tiles(scores)), axis=-1, keepdims=True)
    if extra_logit is not None:
        m = jnp.maximum(m, extra_logit)
    ps = [jnp.exp(s - m) for s in scores]
    den = jnp.sum(functools.reduce(jnp.add, lane_tiles(ps)), axis=-1, keepdims=True)
    if extra_logit is not None:
        den = den + jnp.exp(extra_logit - m)
    o = functools.reduce(jnp.add, [_dot(p.astype(BF16), v) for p, v in zip(ps, values)])
    return o / den


def _swa_kernel(sink_ref, q_ref, kp_ref, kc_ref, kn_ref, vp_ref, vc_ref, vn_ref, kx_ref, vx_ref, o_ref, *, n_lat_blocks):
    h, b = pl.program_id(0), pl.program_id(1)
    blk = SWA_WINDOW
    is_lat = b < n_lat_blocks
    qi = lax.broadcasted_iota(jnp.int32, (SWA_GROUP * blk, 1), 0) & (blk - 1)
    kj = lax.broadcasted_iota(jnp.int32, (1, blk), 1)
    off = lambda ok: jnp.where(ok, 0, blk + 1)
    m_prev = kj >= qi + off(is_lat & (b > 0))
    m_cur = kj >= qi * 0 + off(is_lat)
    m_next = kj <= qi - off(is_lat & (b < n_lat_blocks - 1))
    kp, kc, kn, kx = kp_ref[...], kc_ref[...], kn_ref[...], kx_ref[...]
    vals = [vp_ref[...], vc_ref[...], vn_ref[...], vx_ref[...]]
    q = jnp.concatenate([q_ref[:, g * HEAD_DIM:(g + 1) * HEAD_DIM] for g in range(SWA_GROUP)], axis=0)
    sink = jnp.concatenate([jnp.full((blk, 1), sink_ref[h * SWA_GROUP + g], F32) for g in range(SWA_GROUP)], axis=0)
    scores = [jnp.where(m_prev, _dot_nt(q, kp), NEG_INF), jnp.where(m_cur, _dot_nt(q, kc), NEG_INF),
              jnp.where(m_next, _dot_nt(q, kn), NEG_INF), _dot_nt(q, kx)]
    o = _softmax_pv(scores, vals, sink).astype(o_ref.dtype)
    for g in range(SWA_GROUP):
        o_ref[:, g * HEAD_DIM:(g + 1) * HEAD_DIM] = o[g * blk:(g + 1) * blk]


def _swa_attention(qkv, sinks, *, n_q, n_kv, rows, n_lat):
    blk = SWA_WINDOW
    nlb = n_lat // blk
    ctx_rows = (qkv.shape[0] - n_lat)
    kcol = lambda h: n_q + h
    vcol = lambda h: n_q + n_kv + h
    prev = lambda b: jnp.clip(b - 1, 0, nlb - 1)
    cur = lambda b: jnp.minimum(b, nlb - 1)
    nxt = lambda b: jnp.clip(b + 1, 0, nlb - 1)
    kv = lambda rowf, colf: pl.BlockSpec((blk, HEAD_DIM), lambda h, b: (rowf(b), colf(h)))
    ctx = lambda colf: pl.BlockSpec((ctx_rows, HEAD_DIM), lambda h, b: (n_lat // ctx_rows, colf(h)))
    return pl.pallas_call(
        functools.partial(_swa_kernel, n_lat_blocks=nlb),
        grid=(n_kv, rows // blk),
        in_specs=[
            pl.BlockSpec(memory_space=pltpu.SMEM),
            pl.BlockSpec((blk, SWA_GROUP * HEAD_DIM), lambda h, b: (b, h)),
            kv(prev, kcol), kv(cur, kcol), kv(nxt, kcol), kv(prev, vcol), kv(cur, vcol), kv(nxt, vcol),
            ctx(kcol), ctx(vcol),
        ],
        out_specs=pl.BlockSpec((blk, SWA_GROUP * HEAD_DIM), lambda h, b: (b, h)),
        out_shape=jax.ShapeDtypeStruct((rows, n_q * HEAD_DIM), BF16),
        compiler_params=_params(("arbitrary", "arbitrary")),
        name="swa_attention",
    )(sinks, *([qkv] * 9))


def _na_bias_kernel(rpb_ref, o_ref):
    h = pl.program_id(0)
    n_ri, n_ci = 2 * NA_WIN_ROWS - 1, 2 * NA_WIN_COLS - 1
    cq = lax.broadcasted_iota(jnp.int32, (GRID_W, 2 * GRID_W), 0)
    ck = lax.broadcasted_iota(jnp.int32, (GRID_W, 2 * GRID_W), 1)
    second = ck >= GRID_W
    ck = jnp.where(second, ck - GRID_W, ck)
    c0 = jnp.clip(cq - NA_WIN_COLS // 2, 0, GRID_W - NA_WIN_COLS)
    in_win = (ck >= c0) & (ck < c0 + NA_WIN_COLS)
    cidx = jnp.clip(ck - cq + NA_WIN_COLS - 1, 0, n_ci - 1)
    for k in range(n_ri + 1):
        tile = jnp.full((GRID_W, 2 * GRID_W), NEG_INF, F32)
        for half, ri in ((False, k - 1), (True, k)):
            if 0 <= ri < n_ri:
                sel = in_win & (second if half else jnp.logical_not(second))
                for dci in range(n_ci):
                    tile = jnp.where(sel & (cidx == dci), rpb_ref[(h * n_ri + ri) * n_ci + dci], tile)
        o_ref[0, k] = tile


def _na_bias(rpb):
    n_heads = rpb.shape[0]
    n_tiles = 2 * NA_WIN_ROWS
    return pl.pallas_call(
        _na_bias_kernel,
        grid=(n_heads,),
        in_specs=[pl.BlockSpec(memory_space=pltpu.SMEM)],
        out_specs=pl.BlockSpec((1, n_tiles, GRID_W, 2 * GRID_W), lambda h: (h, 0, 0, 0)),
        out_shape=jax.ShapeDtypeStruct((n_heads, n_tiles, GRID_W, 2 * GRID_W), F32),
        compiler_params=_params(("arbitrary",)),
        name="na_bias",
    )(rpb.reshape(-1))


def _na_kernel(q_ref, kp_ref, kc_ref, kn_ref, vp_ref, vc_ref, vn_ref, kx_ref, vx_ref, bias_ref, o_ref, *, n_grid_rows):
    b = pl.program_id(1)
    qr, w = NA_QROWS, GRID_W
    is_lat = b * qr < n_grid_rows
    q = q_ref[...]
    lane = lax.broadcasted_iota(jnp.int32, (1, 2 * w), 1)
    blocks = []
    for kb, k_ref in enumerate((kp_ref, kc_ref, kn_ref)):
        s = _dot_nt(q, k_ref[...])
        rows = []
        for dq in range(qr):
            r = b * qr + dq
            r0 = jnp.clip(r - NA_WIN_ROWS // 2, 0, n_grid_rows - NA_WIN_ROWS)
            tiles = []
            for jt in range(qr // 2):
                kr = b * qr + (kb - 1) * qr + 2 * jt
                ok0 = (kr >= r0) & (kr < r0 + NA_WIN_ROWS) & is_lat
                ok1 = (kr + 1 >= r0) & (kr + 1 < r0 + NA_WIN_ROWS) & is_lat
                ri0 = (kb - 1) * qr + 2 * jt - dq + NA_WIN_ROWS - 1
                t = s[dq * w:(dq + 1) * w, jt * 2 * w:(jt + 1) * 2 * w] + bias_ref[0, ri0 + 1]
                lo, hi = jnp.where(ok0, 0, w), jnp.where(ok1, 2 * w, w)
                tiles.append(jnp.where((lane >= lo) & (lane < hi), t, NEG_INF))
            rows.append(jnp.concatenate(tiles, axis=1))
        blocks.append(jnp.concatenate(rows, axis=0))
    blocks.append(_dot_nt(q, kx_ref[...]))
    vals = [vp_ref[...], vc_ref[...], vn_ref[...], vx_ref[...]]
    o_ref[...] = _softmax_pv(blocks, vals).astype(o_ref.dtype)


def _na_attention(qkv, bias, *, n_heads, rows, n_lat):
    blk = NA_QROWS * GRID_W
    assert blk == qkv.shape[0] - n_lat
    nlb = n_lat // blk
    kcol = lambda h: n_heads + h
    vcol = lambda h: 2 * n_heads + h
    prev = lambda b: jnp.clip(b - 1, 0, nlb - 1)
    cur = lambda b: jnp.minimum(b, nlb - 1)
    nxt = lambda b: jnp.clip(b + 1, 0, nlb - 1)
    spec = lambda rowf, colf: pl.BlockSpec((blk, HEAD_DIM), lambda h, b: (rowf(b), colf(h)))
    ctx = lambda b: nlb
    return pl.pallas_call(
        functools.partial(_na_kernel, n_grid_rows=n_lat // GRID_W),
        grid=(n_heads, rows // blk),
        in_specs=[
            spec(lambda b: b, lambda h: h),
            spec(prev, kcol), spec(cur, kcol), spec(nxt, kcol), spec(prev, vcol), spec(cur, vcol), spec(nxt, vcol),
            spec(ctx, kcol), spec(ctx, vcol),
            pl.BlockSpec((1,) + bias.shape[1:], lambda h, b: (h, 0, 0, 0)),
        ],
        out_specs=spec(lambda b: b, lambda h: h),
        out_shape=jax.ShapeDtypeStruct((rows, n_heads * HEAD_DIM), BF16),
        compiler_params=_params(("arbitrary", "arbitrary")),
        name="na_attention",
    )(*([qkv] * 9), bias)


def kernel(x, c, ctx, c_ctx, ada_down, ada_up, ada_b, norm_mix, norm_ffn, ffn_w_gate_up, ffn_w_down, pool_w, pool_scale, s5_lam_re, s5_lam_im, s5_log_step, s5_b_re, s5_b_im, s5_c_re, s5_c_im, s5_d, s5_w_glu, swa_w_qkv, swa_w_o, swa_q_gain, swa_k_gain, swa_sinks, na_w_qkv, na_w_o, na_q_gain, na_k_gain, na_rpb):
    bsz, n_lat, d = x.shape
    n_ctx = ctx.shape[1]
    assert bsz == 1 and n_lat % ROW_TILE == 0 and n_ctx == ROW_TILE
    n_tot = n_lat + n_ctx
    depth = ada_down.shape[0]
    d_ff = ffn_w_down.shape[1]
    n_heads = d // HEAD_DIM

    xs = jnp.concatenate([x[0], ctx[0]], axis=0)
    mods = _adaln_all(c, c_ctx, ada_down, ada_up, ada_b)

    for layer in range(depth):
        kind = layer % 4
        last = layer == depth - 1
        rows = n_lat if last else n_tot
        mod = mods[layer]
        res = dict(rows=rows, n_lat=n_lat)

        if kind == 0:
            xs = _pool_layer(xs, norm_mix[layer], mod, pool_w, pool_scale, **res)
        else:
            u = _norm_mod(xs, norm_mix[layer], mod, shift_idx=0, rows=n_tot, n_lat=n_lat)
            if kind == 1:
                a_re, a_im, bb_re, bb_im = _s5_prep(s5_lam_re, s5_lam_im, s5_log_step, s5_b_re, s5_b_im)
                y = _s5_scan(u, *_s5_pack(a_re, a_im, bb_re, bb_im, s5_c_re, s5_c_im), n_lat=n_lat)
                g = _s5_gelu(y, u, s5_d, rows=rows)
                half = s5_w_glu.shape[1] // 2
                tm = _pick(rows, (768, 512, 640, 256))
                tn = _pick(half, (256, 128))
                xs = _mm(g, s5_w_glu, rows=rows, tm=tm, tn=tn, out_cols=half, out_dtype=F32,
                         col_maps=(lambda j: j, lambda j, o=half // tn: j + o),
                         epilogue=functools.partial(_epi_glu_residual, gate_idx=2, tm=tm, n_lat=n_lat),
                         extras=_residual_extras(xs, mod, tm, tn), w_outer=True, name="s5_glu")
            elif kind == 2:
                n_kv = n_heads // SWA_GROUP
                qkv = _qkv_proj(u, swa_w_qkv, swa_q_gain, swa_k_gain, n_q=n_heads, n_kv=n_kv,
                                rope_tables=_rope_tables(n_lat, n_ctx), rows=n_tot)
                o = _swa_attention(qkv, swa_sinks, n_q=n_heads, n_kv=n_kv, **res)
                xs = _mm_residual(o, swa_w_o, xs, mod, gate_idx=2, w_outer=True, name="swa_out", **res)
            else:
                qkv = _qkv_proj(u, na_w_qkv, na_q_gain, na_k_gain, n_q=n_heads, n_kv=n_heads,
                                rope_tables=None, rows=n_tot)
                o = _na_attention(qkv, _na_bias(na_rpb), n_heads=n_heads, **res)
                xs = _mm_residual(o, na_w_o, xs, mod, gate_idx=2, w_outer=True, name="na_out", **res)

        hx = _norm_mod(xs, norm_ffn[layer], mod, shift_idx=3, **res)
        tm = _pick(rows, (1056, 1024, 640, 256))
        tn = _pick(d_ff, (256, 128))
        hh, w_down = _mm(hx, ffn_w_gate_up, layer=layer, rows=rows, tm=tm, tn=tn, out_cols=d_ff, out_dtype=BF16,
                         col_maps=(lambda j: j, lambda j, o=d_ff // tn: j + o), epilogue=_epi_swiglu,
                         w_outer=True, name="ffn_gate_up", to_bf16=ffn_w_down)
        xs = _mm_residual(hh, w_down, xs, mod, gate_idx=5, w_outer=False, name="ffn_down", **res)

    return xs[:n_lat][None]
```

```python
import functools
import math

import jax
import jax.numpy as jnp
from jax import lax
from jax.experimental import pallas as pl
from jax.experimental.pallas import tpu as pltpu

F32 = jnp.float32
BF16 = jnp.bfloat16

RMS_EPS = 1e-6
NEG_INF = -1e30
HEAD_DIM = 128
GRID_W = 64
ROPE_BASE = 10000.0
POOL_WINDOWS = (2, 4, 8, 16)
POOL_HALO = 8
S5_GROUP = 16
S5_STATE = 64
S5_CHUNK = 256
S5_ROW_PAD = 8
SWA_WINDOW = 128
SWA_GROUP = 4
NA_WIN_ROWS = 8
NA_WIN_COLS = 16
NA_QROWS = 4
SWA_KV_PER_STEP = 2
NA_HEADS_PER_STEP = 2

LANES = 128
ROW_TILE = 256
V7X_VMEM_LIMIT = 56 * 1024 * 1024


def _pick(n, candidates):
    for c in candidates:
        if n % c == 0:
            return c
    raise ValueError(f"no tile for {n} in {candidates}")


def _params(sem, vmem=V7X_VMEM_LIMIT):
    return pltpu.CompilerParams(dimension_semantics=sem, vmem_limit_bytes=vmem)


def _dot(a, b):
    return jnp.dot(a, b, preferred_element_type=F32)


def _dot_nt(a, b):
    return lax.dot_general(a, b, (((1,), (1,)), ((), ())), preferred_element_type=F32)


def _small_mm_kernel(a_ref, w_ref, b_ref, o_ref, *, silu_in, kc):
    a = a_ref[0]
    if silu_in:
        a = a * jax.nn.sigmoid(a)
    a = a.astype(BF16)
    k = a.shape[1]
    acc = jnp.zeros(o_ref.shape[1:], F32)
    for k0 in range(0, k, kc):
        acc += _dot(a[:, k0:k0 + kc], w_ref[0, k0:k0 + kc, :].astype(BF16))
    o_ref[0] = acc + b_ref[0]


def _small_mm(a, w, b, *, silu_in, tn):
    depth, k, n = w.shape
    shared = a.shape[0] == 1
    kern = functools.partial(_small_mm_kernel, silu_in=silu_in, kc=min(k, 512))
    return pl.pallas_call(
        kern,
        grid=(depth, n // tn),
        in_specs=[
            pl.BlockSpec((1, 8, k), lambda l, j: (0 if shared else l, 0, 0)),
            pl.BlockSpec((1, k, tn), lambda l, j: (l, 0, j)),
            pl.BlockSpec((1, 1, tn), lambda l, j: (l, 0, j)),
        ],
        out_specs=pl.BlockSpec((1, 8, tn), lambda l, j: (l, 0, j)),
        out_shape=jax.ShapeDtypeStruct((depth, 8, n), F32),
        compiler_params=_params(("arbitrary", "arbitrary")),
        name="adaln_mm",
    )(a, w, b)


def _adaln_all(c, c_ctx, ada_down, ada_up, ada_b):
    depth, d, rank = ada_down.shape
    cvec = jnp.zeros((1, 8, d), F32).at[0, 0].set(c[0]).at[0, 1].set(c_ctx)
    t = _small_mm(cvec, ada_down, jnp.zeros((depth, 1, rank), F32), silu_in=True, tn=_pick(rank, (512, 256, 128)))
    m = _small_mm(t, ada_up, ada_b.reshape(depth, 1, 6 * d), silu_in=False, tn=_pick(6 * d, (2048, 1024, 512)))
    return m[:, :2].reshape(depth, 2, 6, d)


def _rms_mod(x, gain, shift, scale):
    ms = jnp.mean(x * x, axis=-1, keepdims=True)
    return (x * lax.rsqrt(ms + RMS_EPS) * gain) * (1.0 + scale) + shift


def _norm_mod_kernel(x_ref, g_ref, mod_ref, o_ref, *, shift_idx):
    u = _rms_mod(x_ref[...], g_ref[...], mod_ref[0, shift_idx:shift_idx + 1, :],
                 mod_ref[0, shift_idx + 1:shift_idx + 2, :])
    o_ref[...] = u.astype(o_ref.dtype)


def _norm_mod(xs, gain, mod, *, shift_idx, rows, n_lat):
    d = xs.shape[1]
    tr = ROW_TILE
    return pl.pallas_call(
        functools.partial(_norm_mod_kernel, shift_idx=shift_idx),
        grid=(rows // tr,),
        in_specs=[
            pl.BlockSpec((tr, d), lambda i: (i, 0)),
            pl.BlockSpec((1, d), lambda i: (0, 0)),
            pl.BlockSpec((1, 6, d), lambda i: (jnp.where(i * tr >= n_lat, 1, 0), 0, 0)),
        ],
        out_specs=pl.BlockSpec((tr, d), lambda i: (i, 0)),
        out_shape=jax.ShapeDtypeStruct((rows, d), BF16),
        compiler_params=_params(("arbitrary",)),
        name="norm_mod",
    )(xs, gain.reshape(1, d), mod)


def _mm(a, w, *, rows, tm, tn, out_cols, out_dtype, col_maps, epilogue, extras=(), w_outer, name,
        layer=None, to_bf16=None):
    k = a.shape[1]
    n_i, n_j = rows // tm, out_cols // tn
    cast = w.dtype != BF16
    assert w_outer or not cast
    n_parts = len(col_maps)
    kc = _pick(k, (512, 256, 128))

    if w_outer:
        grid = (n_j, n_i)
        ij = lambda g0, g1: (g1, g0)
    else:
        grid = (n_i, n_j)
        ij = lambda g0, g1: (g0, g1)

    if w.ndim == 3:
        w_block, w_index = (None, k, tn), lambda col: (layer, 0, col)
    else:
        w_block, w_index = (k, tn), lambda col: (0, col)
    in_specs = [pl.BlockSpec((tm, k), lambda g0, g1: (ij(g0, g1)[0], 0))]
    for cm in col_maps:
        in_specs.append(pl.BlockSpec(w_block, lambda g0, g1, cm=cm: w_index(cm(ij(g0, g1)[1]))))
    for _, shape, fn in extras:
        in_specs.append(pl.BlockSpec(shape, lambda g0, g1, fn=fn: fn(*ij(g0, g1))))
    n_ex = len(extras)
    out_specs = [pl.BlockSpec((tm, tn), lambda g0, g1: ij(g0, g1))]
    out_shape = [jax.ShapeDtypeStruct((rows, out_cols), out_dtype)]
    n_side = 0 if to_bf16 is None else 1
    if n_side:
        _, side_r, side_c = to_bf16.shape
        slab = side_r // (grid[0] * grid[1])
        assert slab * grid[0] * grid[1] == side_r and slab % 16 == 0
        in_specs.append(pl.BlockSpec((None, slab, side_c), lambda g0, g1: (layer, g0 * grid[1] + g1, 0)))
        out_specs.append(pl.BlockSpec((slab, side_c), lambda g0, g1: (g0 * grid[1] + g1, 0)))
        out_shape.append(jax.ShapeDtypeStruct((side_r, side_c), BF16))

    def body(*refs):
        a_ref = refs[0]
        w_refs = refs[1:1 + n_parts]
        ex_refs = refs[1 + n_parts:1 + n_parts + n_ex]
        n_in = 1 + n_parts + n_ex + n_side
        o_ref = refs[n_in]
        scratch = refs[n_in + 1 + n_side:]
        i, j = ij(pl.program_id(0), pl.program_id(1))
        if n_side:
            refs[n_in + 1][...] = refs[n_in - 1][...].astype(BF16)
        if cast:
            @pl.when(i == 0)
            def _():
                def cp(c, carry):
                    sl = pl.ds(pl.multiple_of(c * kc, kc), kc)
                    for p in range(n_parts):
                        scratch[p][sl, :] = w_refs[p][sl, :].astype(BF16)
                    return carry
                lax.fori_loop(0, k // kc, cp, 0)
            ws = scratch
        else:
            ws = w_refs
        av = a_ref[...]
        accs = [_dot(av, ws[p][...]) for p in range(n_parts)]
        epilogue(accs, i, j, ex_refs, o_ref)

    outs = pl.pallas_call(
        body,
        grid=grid,
        in_specs=in_specs,
        out_specs=out_specs,
        out_shape=out_shape,
        scratch_shapes=[pltpu.VMEM((k, tn), BF16) for _ in range(n_parts)] if cast else [],
        compiler_params=_params(("arbitrary", "arbitrary")),
        name=name,
    )(a, *([w] * n_parts), *[e[0] for e in extras], *([to_bf16] if n_side else []))
    return outs if n_side else outs[0]


def _row_gate(mod_ref, idx, i, tm, n_lat):
    rows = i * tm + lax.broadcasted_iota(jnp.int32, (tm, 1), 0)
    return jnp.where(rows < n_lat, mod_ref[0, idx:idx + 1, :], mod_ref[1, idx:idx + 1, :])


def _epi_swiglu(accs, i, j, ex, o_ref):
    a, g = accs
    o_ref[...] = (a * jax.nn.sigmoid(a) * g).astype(o_ref.dtype)


def _epi_residual(accs, i, j, ex, o_ref, *, gate_idx, tm, n_lat):
    res_ref, mod_ref = ex
    o_ref[...] = res_ref[...] + _row_gate(mod_ref, gate_idx, i, tm, n_lat) * accs[0]


def _epi_glu_residual(accs, i, j, ex, o_ref, *, gate_idx, tm, n_lat):
    res_ref, mod_ref = ex
    a, g = accs
    o_ref[...] = res_ref[...] + _row_gate(mod_ref, gate_idx, i, tm, n_lat) * (a * jax.nn.sigmoid(g))


def _residual_extras(xs, mod, tm, tn):
    return ((xs, (tm, tn), lambda i, j: (i, j)), (mod, (2, 6, tn), lambda i, j: (0, 0, j)))


def _mm_residual(a, w, xs, mod, *, gate_idx, rows, n_lat, w_outer, name):
    n = w.shape[1]
    tm = _pick(rows, (768, 512, 640, 256))
    tn = _pick(n, (512, 256) if w_outer else (256, 128))
    return _mm(a, w, rows=rows, tm=tm, tn=tn, out_cols=n, out_dtype=F32, col_maps=(lambda j: j,),
               epilogue=functools.partial(_epi_residual, gate_idx=gate_idx, tm=tm, n_lat=n_lat),
               extras=_residual_extras(xs, mod, tm, tn), w_outer=w_outer, name=name)


def _pool_kernel(xp_ref, x_ref, xn_ref, g_ref, mod_ref, pw_ref, ps_ref, o_ref, *, tr, n_lat, n_tot):
    i = pl.program_id(0)
    halo = POOL_HALO
    gain = g_ref[...]
    shift, scale, gate = mod_ref[0, 0:1, :], mod_ref[0, 1:2, :], mod_ref[0, 2:3, :]
    x = x_ref[...]
    ue = jnp.concatenate([_rms_mod(xp_ref[...], gain, shift, scale), _rms_mod(x, gain, shift, scale),
                          _rms_mod(xn_ref[...], gain, shift, scale)], axis=0)
    is_lat = i * tr < n_lat
    seq_lo = jnp.where(is_lat, 0, n_lat)
    seq_hi = jnp.where(is_lat, n_lat, n_tot)
    t_g = i * tr + lax.broadcasted_iota(jnp.int32, (tr, 1), 0)
    s_g = i * tr - halo + lax.broadcasted_iota(jnp.int32, (1, tr + 2 * halo), 1)
    s_ok = (s_g >= seq_lo) & (s_g < seq_hi)
    pg = ue.shape[1] // len(POOL_WINDOWS)
    for gi, w in enumerate(POOL_WINDOWS):
        cols = slice(gi * pg, (gi + 1) * pg)
        ug = ue[:, cols]
        band = (s_g >= t_g - w // 2) & (s_g < t_g + w // 2) & s_ok
        ssum = _dot(jnp.where(band, 1.0, 0.0).astype(BF16), ug.astype(BF16))
        cnt = jnp.minimum(t_g + w // 2, seq_hi) - jnp.maximum(t_g - w // 2, seq_lo)
        p = ssum / cnt.astype(F32) - ug[halo:halo + tr]
        y = _dot(p.astype(BF16), pw_ref[gi]) * ps_ref[:, cols]
        o_ref[:, cols] = x[:, cols] + gate[:, cols] * y


def _pool_layer(xs, gain, mod, pool_w, pool_scale, *, rows, n_lat):
    n_tot, d = xs.shape
    tr, halo = ROW_TILE, POOL_HALO
    hb = tr // halo
    last_hb = n_tot // halo - 1
    return pl.pallas_call(
        functools.partial(_pool_kernel, tr=tr, n_lat=n_lat, n_tot=n_tot),
        grid=(rows // tr,),
        in_specs=[
            pl.BlockSpec((halo, d), lambda i: (jnp.maximum(i * hb - 1, 0), 0)),
            pl.BlockSpec((tr, d), lambda i: (i, 0)),
            pl.BlockSpec((halo, d), lambda i: (jnp.minimum((i + 1) * hb, last_hb), 0)),
            pl.BlockSpec((1, d), lambda i: (0, 0)),
            pl.BlockSpec((1, 6, d), lambda i: (jnp.where(i * tr >= n_lat, 1, 0), 0, 0)),
            pl.BlockSpec(pool_w.shape, lambda i: (0, 0, 0)),
            pl.BlockSpec((1, d), lambda i: (0, 0)),
        ],
        out_specs=pl.BlockSpec((tr, d), lambda i: (i, 0)),
        out_shape=jax.ShapeDtypeStruct((rows, d), F32),
        compiler_params=_params(("arbitrary",)),
        name="pool_mixer",
    )(xs, xs, xs, gain.reshape(1, d), mod, pool_w.astype(BF16), pool_scale.reshape(1, d))


def _s5_prep_kernel(lre_ref, lim_ref, ls_ref, bre_ref, bim_ref, are_ref, aim_ref, bbre_ref, bbim_ref):
    lam_re, lam_im = lre_ref[0], lim_ref[0]
    dt = jnp.exp(ls_ref[0])
    z_re, z_im = lam_re * dt, lam_im * dt
    mag = jnp.exp(z_re)
    a_re, a_im = mag * jnp.cos(z_im), mag * jnp.sin(z_im)
    den = lam_re * lam_re + lam_im * lam_im
    k_re = ((a_re - 1.0) * lam_re + a_im * lam_im) / den
    k_im = (a_im * lam_re - (a_re - 1.0) * lam_im) / den
    are_ref[0] = a_re
    aim_ref[0] = a_im
    b_re, b_im = bre_ref[0], bim_ref[0]
    bbre_ref[0] = k_re[:, None, :] * b_re - k_im[:, None, :] * b_im
    bbim_ref[0] = k_re[:, None, :] * b_im + k_im[:, None, :] * b_re


def _s5_prep(lam_re, lam_im, log_step, b_re, b_im):
    _, g, p, h = b_re.shape
    spec2 = pl.BlockSpec((1, g, p), lambda d: (d, 0, 0))
    spec3 = pl.BlockSpec((1, g, h, p), lambda d: (d, 0, 0, 0))
    return pl.pallas_call(
        _s5_prep_kernel,
        grid=(2,),
        in_specs=[spec2, spec2, pl.BlockSpec((1, g, 1), lambda d: (d, 0, 0)), spec3, spec3],
        out_specs=[spec2, spec2, spec3, spec3],
        out_shape=[jax.ShapeDtypeStruct((2, g, p), F32)] * 2 + [jax.ShapeDtypeStruct((2, g, h, p), F32)] * 2,
        compiler_params=_params(("arbitrary",)),
        name="s5_prep",
    )(lam_re, lam_im, log_step[..., None], jnp.swapaxes(b_re, 2, 3), jnp.swapaxes(b_im, 2, 3))


def _s5_pack(a_re, a_im, bb_re, bb_im, c_re, c_im):
    _, g, h, p = bb_re.shape
    r = g // 2
    eye2 = jnp.eye(2, dtype=F32)
    eye4 = jnp.eye(4, dtype=F32)
    slot = eye4[jnp.arange(r) % 4]

    def win_part(bb):
        t = bb.reshape(2, r, 2, h, p)
        t = t[:, :, :, :, None, :] * eye2[None, None, :, None, :, None]
        t = t.reshape(2, r, 2 * h, 2 * p)
        t = slot[None, :, :, None, None] * t[:, :, None, :, :]
        return t.reshape(2, r, 4 * 2 * h, 2 * p)

    def c_part(c):
        t = jnp.swapaxes(c, 2, 3).reshape(2, r, 2, p, h)
        t = t[:, :, :, :, None, :] * eye2[None, None, :, None, :, None]
        t = t.reshape(2, r, 2 * p, 2 * h)
        t = t[:, :, :, None, :] * slot[None, :, None, :, None]
        return t.reshape(2, r, 2 * p, 4 * 2 * h)

    win = jnp.concatenate([win_part(bb_re), win_part(bb_im)], axis=-1).astype(BF16)
    cmat = jnp.concatenate([c_part(c_re), c_part(-c_im)], axis=2).astype(BF16)
    return win, cmat, a_re.reshape(2, r, 2 * p), a_im.reshape(2, r, 2 * p)


def _s5_scan_kernel(u_ref, win_ref, cmat_ref, are_ref, aim_ref, y_ref, xre, xim, hre, him, *, nsub, rb):
    g0, c = pl.program_id(0), pl.program_id(1)
    rev = g0 // nsub == 1
    tc, pitch = S5_CHUNK, rb + S5_ROW_PAD
    block_rows = lambda r: pl.ds(r, tc, stride=pitch)

    @pl.when(c == 0)
    def _():
        hre[...] = jnp.zeros_like(hre)
        him[...] = jnp.zeros_like(him)

    for q in range(rb // 4):
        ut = u_ref[:, q * LANES:(q + 1) * LANES]
        for s in range(4):
            r = q * 4 + s
            x = _dot(ut, win_ref[0, r])
            xre[block_rows(r), :] = x[:, :LANES]
            xim[block_rows(r), :] = x[:, LANES:]

    a_re, a_im = are_ref[0], aim_ref[0]

    def step(t, carry):
        h_re, h_im = carry
        tt = jnp.where(rev, tc - 1 - t, t)
        rows = pl.ds(pl.multiple_of(tt * pitch, 8), rb)
        n_re = a_re * h_re - a_im * h_im + xre[rows, :]
        n_im = a_re * h_im + a_im * h_re + xim[rows, :]
        xre[rows, :] = n_re
        xim[rows, :] = n_im
        return n_re, n_im

    h_re, h_im = lax.fori_loop(0, tc, step, (hre[...], him[...]), unroll=8)
    hre[...] = h_re
    him[...] = h_im

    for q in range(rb // 4):
        acc = jnp.zeros((tc, LANES), F32)
        for s in range(4):
            r = q * 4 + s
            acc += (_dot(xre[block_rows(r), :].astype(BF16), cmat_ref[0, r, :LANES, :])
                    + _dot(xim[block_rows(r), :].astype(BF16), cmat_ref[0, r, LANES:, :]))
        y_ref[0, :, q * LANES:(q + 1) * LANES] = acc.astype(y_ref.dtype)


def _s5_scan(u, win, cmat, a_re, a_im, *, n_lat):
    n_tot, d = u.shape
    r_all = win.shape[1]
    rb = min(64, r_all)
    nsub = r_all // rb
    tc = S5_CHUNK
    n_c, n_lc = n_tot // tc, n_lat // tc
    cols = rb * 2 * S5_GROUP

    def split(w):
        return w.reshape((2 * nsub, rb) + w.shape[2:])

    def chunk(g0, c):
        return jnp.where(g0 // nsub == 0, (c + n_lc) % n_c, n_c - 1 - c)

    wspec = lambda shape: pl.BlockSpec((1,) + shape, lambda g0, c: (g0,) + (0,) * len(shape))
    return pl.pallas_call(
        functools.partial(_s5_scan_kernel, nsub=nsub, rb=rb),
        grid=(2 * nsub, n_c),
        in_specs=[
            pl.BlockSpec((tc, cols), lambda g0, c: (chunk(g0, c), g0 % nsub)),
            wspec((rb, LANES, 2 * LANES)), wspec((rb, 2 * LANES, LANES)), wspec((rb, LANES)), wspec((rb, LANES)),
        ],
        out_specs=pl.BlockSpec((1, tc, cols), lambda g0, c: (g0 // nsub, chunk(g0, c), g0 % nsub)),
        out_shape=jax.ShapeDtypeStruct((2, n_tot, d), BF16),
        scratch_shapes=[pltpu.VMEM((tc * (rb + S5_ROW_PAD), LANES), F32)] * 2 + [pltpu.VMEM((rb, LANES), F32)] * 2,
        compiler_params=_params(("arbitrary", "arbitrary")),
        name="s5_scan",
    )(u, split(win), split(cmat), split(a_re), split(a_im))


def _s5_gelu_kernel(y_ref, u_ref, d_ref, o_ref):
    v = y_ref[0].astype(F32) + y_ref[1].astype(F32) + d_ref[...] * u_ref[...].astype(F32)
    o_ref[...] = jax.nn.gelu(v, approximate=True).astype(o_ref.dtype)


def _s5_gelu(y, u, dvec, *, rows):
    d = u.shape[1]
    tr = ROW_TILE
    return pl.pallas_call(
        _s5_gelu_kernel,
        grid=(rows // tr,),
        in_specs=[pl.BlockSpec((2, tr, d), lambda i: (0, i, 0)), pl.BlockSpec((tr, d), lambda i: (i, 0)),
                  pl.BlockSpec((1, d), lambda i: (0, 0))],
        out_specs=pl.BlockSpec((tr, d), lambda i: (i, 0)),
        out_shape=jax.ShapeDtypeStruct((rows, d), BF16),
        compiler_params=_params(("arbitrary",)),
        name="s5_gelu",
    )(y, u, dvec.reshape(1, d))


def _rope_tables(n_lat, n_ctx):
    quarter = HEAD_DIM // 4
    inv = jnp.power(ROPE_BASE, -jnp.arange(quarter, dtype=F32) / quarter)
    t = jnp.arange(n_lat)
    ang_r = (t // GRID_W).astype(F32)[:, None] * inv
    ang_c = (t % GRID_W).astype(F32)[:, None] * inv
    ang = jnp.concatenate([ang_r, ang_r, ang_c, ang_c], axis=-1)
    sign = jnp.where((jnp.arange(HEAD_DIM) % (2 * quarter)) < quarter, -1.0, 1.0)
    cos = jnp.concatenate([jnp.cos(ang), jnp.ones((n_ctx, HEAD_DIM), F32)], axis=0)
    sin = jnp.concatenate([jnp.sin(ang) * sign, jnp.zeros((n_ctx, HEAD_DIM), F32)], axis=0)
    return cos, sin


def _epi_qkv(accs, i, j, ex, o_ref, *, n_norm_tiles, rope, tn):
    gain_ref = ex[0]
    heads_per_acc = tn // len(accs) // HEAD_DIM

    @pl.when(j < n_norm_tiles)
    def _():
        quarter = HEAD_DIM // 4
        lane = lax.broadcasted_iota(jnp.int32, (1, HEAD_DIM), 1)
        first = (lane % (2 * quarter)) < quarter
        for h in range(tn // HEAD_DIM):
            cols = slice(h * HEAD_DIM, (h + 1) * HEAD_DIM)
            hl = h % heads_per_acc
            x = accs[h // heads_per_acc][:, hl * HEAD_DIM:(hl + 1) * HEAD_DIM]
            ms = jnp.mean(x * x, axis=-1, keepdims=True)
            xn = x * lax.rsqrt(ms + RMS_EPS) * gain_ref[:, cols]
            if rope:
                cos_ref, sin_ref = ex[1], ex[2]
                partner = jnp.where(first, pltpu.roll(xn, HEAD_DIM - quarter, 1), pltpu.roll(xn, quarter, 1))
                xn = xn * cos_ref[...] + partner * sin_ref[...]
            o_ref[:, cols] = xn.astype(o_ref.dtype)

    @pl.when(j >= n_norm_tiles)
    def _():
        sub = tn // len(accs)
        for s, acc in enumerate(accs):
            o_ref[:, s * sub:(s + 1) * sub] = acc.astype(o_ref.dtype)


def _qkv_proj(u, w_qkv, q_gain, k_gain, *, n_q, n_kv, rope_tables, rows):
    n = w_qkv.shape[1]
    tm = _pick(rows, (768, 512, 640, 256))
    tn = _pick(n_kv * HEAD_DIM, (512, 256, 128))
    gain = jnp.concatenate([jnp.tile(q_gain * HEAD_DIM ** -0.5, n_q), jnp.tile(k_gain, n_kv),
                            jnp.ones((n_kv * HEAD_DIM,), F32)]).reshape(1, n)
    extras = [(gain, (1, tn), lambda i, j: (0, j))]
    if rope_tables is not None:
        extras += [(t, (tm, HEAD_DIM), lambda i, j: (i, 0)) for t in rope_tables]
    epi = functools.partial(_epi_qkv, n_norm_tiles=(n_q + n_kv) * HEAD_DIM // tn, rope=rope_tables is not None, tn=tn)
    return _mm(u, w_qkv, rows=rows, tm=tm, tn=tn, out_cols=n, out_dtype=BF16, col_maps=(lambda j: j,),
               epilogue=epi, extras=tuple(extras), w_outer=True, name="qkv_proj")


def _softmax_pv(tiles, values, extra_logit=None):
    rows = tiles[0].shape[0]
    mx = functools.reduce(jnp.maximum, tiles)
    if extra_logit is not None:
        mx = jnp.maximum(mx, extra_logit)
    m = jnp.broadcast_to(jnp.max(mx, axis=-1, keepdims=True), (rows, LANES))
    p = jnp.concatenate([jnp.exp(t - m).astype(BF16) for t in tiles], axis=1)
    ones = jnp.ones((LANES, LANES), BF16)
    v_aug = jnp.concatenate([jnp.concatenate([v, ones], axis=1) for v in values], axis=0)
    acc = _dot(p, v_aug)
    den = acc[:, LANES:]
    if extra_logit is not None:
        den = den + jnp.exp(extra_logit - m)
    return acc[:, :LANES] / den


def _swa_kernel(sink_ref, q_ref, kp_ref, kc_ref, kn_ref, vp_ref, vc_ref, vn_ref, kx_ref, vx_ref, o_ref, *, n_lat_blocks):
    h, b = pl.program_id(0), pl.program_id(1)
    blk = SWA_WINDOW
    is_lat = b < n_lat_blocks
    qi = lax.broadcasted_iota(jnp.int32, (SWA_GROUP * blk, 1), 0) & (blk - 1)
    kj = lax.broadcasted_iota(jnp.int32, (1, blk), 1)
    off = lambda ok: jnp.where(ok, 0, blk + 1)
    m_prev = kj >= qi + off(is_lat & (b > 0))
    m_cur = kj >= qi * 0 + off(is_lat)
    m_next = kj <= qi - off(is_lat & (b < n_lat_blocks - 1))
    n_ctx_tiles = kx_ref.shape[0] // LANES
    for hh in range(SWA_KV_PER_STEP):
        kcols = slice(hh * HEAD_DIM, (hh + 1) * HEAD_DIM)
        qh = lambda g: (hh * SWA_GROUP + g) * HEAD_DIM
        q = jnp.concatenate([q_ref[:, qh(g):qh(g) + HEAD_DIM] for g in range(SWA_GROUP)], axis=0)
        sink = jnp.concatenate([jnp.full((blk, LANES), sink_ref[(h * SWA_KV_PER_STEP + hh) * SWA_GROUP + g], F32)
                                for g in range(SWA_GROUP)], axis=0)
        s_ctx = _dot_nt(q, kx_ref[:, kcols])
        tiles = [jnp.where(m_prev, _dot_nt(q, kp_ref[:, kcols]), NEG_INF),
                 jnp.where(m_cur, _dot_nt(q, kc_ref[:, kcols]), NEG_INF),
                 jnp.where(m_next, _dot_nt(q, kn_ref[:, kcols]), NEG_INF)]
        tiles += [s_ctx[:, t * LANES:(t + 1) * LANES] for t in range(n_ctx_tiles)]
        vals = [vp_ref[:, kcols], vc_ref[:, kcols], vn_ref[:, kcols]]
        vals += [vx_ref[t * LANES:(t + 1) * LANES, kcols] for t in range(n_ctx_tiles)]
        o = _softmax_pv(tiles, vals, sink).astype(o_ref.dtype)
        for g in range(SWA_GROUP):
            o_ref[:, qh(g):qh(g) + HEAD_DIM] = o[g * blk:(g + 1) * blk]


def _swa_attention(qkv, sinks, *, n_q, n_kv, rows, n_lat):
    blk = SWA_WINDOW
    nlb = n_lat // blk
    ctx_rows = (qkv.shape[0] - n_lat)
    hp = SWA_KV_PER_STEP
    assert n_kv % hp == 0 and n_q % hp == 0
    kcol = lambda h: n_q // hp + h
    vcol = lambda h: (n_q + n_kv) // hp + h
    prev = lambda b: jnp.clip(b - 1, 0, nlb - 1)
    cur = lambda b: jnp.minimum(b, nlb - 1)
    nxt = lambda b: jnp.clip(b + 1, 0, nlb - 1)
    kv = lambda rowf, colf: pl.BlockSpec((blk, hp * HEAD_DIM), lambda h, b: (rowf(b), colf(h)))
    ctx = lambda colf: pl.BlockSpec((ctx_rows, hp * HEAD_DIM), lambda h, b: (n_lat // ctx_rows, colf(h)))
    return pl.pallas_call(
        functools.partial(_swa_kernel, n_lat_blocks=nlb),
        grid=(n_kv // hp, rows // blk),
        in_specs=[
            pl.BlockSpec(memory_space=pltpu.SMEM),
            pl.BlockSpec((blk, hp * SWA_GROUP * HEAD_DIM), lambda h, b: (b, h)),
            kv(prev, kcol), kv(cur, kcol), kv(nxt, kcol), kv(prev, vcol), kv(cur, vcol), kv(nxt, vcol),
            ctx(kcol), ctx(vcol),
        ],
        out_specs=pl.BlockSpec((blk, hp * SWA_GROUP * HEAD_DIM), lambda h, b: (b, h)),
        out_shape=jax.ShapeDtypeStruct((rows, n_q * HEAD_DIM), BF16),
        compiler_params=_params(("arbitrary", "arbitrary")),
        name="swa_attention",
    )(sinks, *([qkv] * 9))


def _na_bias_kernel(rpb_ref, o_ref):
    h = pl.program_id(0)
    n_ri, n_ci = 2 * NA_WIN_ROWS - 1, 2 * NA_WIN_COLS - 1
    cq = lax.broadcasted_iota(jnp.int32, (GRID_W, 2 * GRID_W), 0)
    ck = lax.broadcasted_iota(jnp.int32, (GRID_W, 2 * GRID_W), 1)
    second = ck >= GRID_W
    ck = jnp.where(second, ck - GRID_W, ck)
    c0 = jnp.clip(cq - NA_WIN_COLS // 2, 0, GRID_W - NA_WIN_COLS)
    in_win = (ck >= c0) & (ck < c0 + NA_WIN_COLS)
    cidx = jnp.clip(ck - cq + NA_WIN_COLS - 1, 0, n_ci - 1)
    for k in range(n_ri + 1):
        tile = jnp.full((GRID_W, 2 * GRID_W), NEG_INF, F32)
        for half, ri in ((False, k - 1), (True, k)):
            if 0 <= ri < n_ri:
                sel = in_win & (second if half else jnp.logical_not(second))
                for dci in range(n_ci):
                    tile = jnp.where(sel & (cidx == dci), rpb_ref[(h * n_ri + ri) * n_ci + dci], tile)
        o_ref[0, k] = tile


def _na_bias(rpb):
    n_heads = rpb.shape[0]
    n_tiles = 2 * NA_WIN_ROWS
    return pl.pallas_call(
        _na_bias_kernel,
        grid=(n_heads,),
        in_specs=[pl.BlockSpec(memory_space=pltpu.SMEM)],
        out_specs=pl.BlockSpec((1, n_tiles, GRID_W, 2 * GRID_W), lambda h: (h, 0, 0, 0)),
        out_shape=jax.ShapeDtypeStruct((n_heads, n_tiles, GRID_W, 2 * GRID_W), F32),
        compiler_params=_params(("arbitrary",)),
        name="na_bias",
    )(rpb.reshape(-1))


def _na_kernel(q_ref, kp_ref, kc_ref, kn_ref, vp_ref, vc_ref, vn_ref, kx_ref, vx_ref, bias_ref, o_ref, *, n_grid_rows):
    b = pl.program_id(1)
    qr, w = NA_QROWS, GRID_W
    is_lat = b * qr < n_grid_rows
    lane = lax.broadcasted_iota(jnp.int32, (1, 2 * w), 1)
    n_kt = qr // 2
    windows = {}
    for kb in range(3):
        for dq in range(qr):
            r = b * qr + dq
            r0 = jnp.clip(r - NA_WIN_ROWS // 2, 0, n_grid_rows - NA_WIN_ROWS)
            for jt in range(n_kt):
                kr = b * qr + (kb - 1) * qr + 2 * jt
                ok0 = (kr >= r0) & (kr < r0 + NA_WIN_ROWS) & is_lat
                ok1 = (kr + 1 >= r0) & (kr + 1 < r0 + NA_WIN_ROWS) & is_lat
                lo, hi = jnp.where(ok0, 0, w), jnp.where(ok1, 2 * w, w)
                windows[kb, dq, jt] = (lane >= lo) & (lane < hi)
    for hh in range(NA_HEADS_PER_STEP):
        cols = slice(hh * HEAD_DIM, (hh + 1) * HEAD_DIM)
        q = q_ref[:, cols]
        tiles, vals = [], []
        for kb, (k_ref, v_ref) in enumerate(((kp_ref, vp_ref), (kc_ref, vc_ref), (kn_ref, vn_ref))):
            s = _dot_nt(q, k_ref[:, cols])
            for jt in range(n_kt):
                parts = []
                for dq in range(qr):
                    ri0 = (kb - 1) * qr + 2 * jt - dq + NA_WIN_ROWS - 1
                    t = s[dq * w:(dq + 1) * w, jt * 2 * w:(jt + 1) * 2 * w] + bias_ref[hh, ri0 + 1]
                    parts.append(jnp.where(windows[kb, dq, jt], t, NEG_INF))
                tiles.append(jnp.concatenate(parts, axis=0))
                vals.append(v_ref[jt * LANES:(jt + 1) * LANES, cols])
        s_ctx = _dot_nt(q, kx_ref[:, cols])
        for t in range(kx_ref.shape[0] // LANES):
            tiles.append(s_ctx[:, t * LANES:(t + 1) * LANES])
            vals.append(vx_ref[t * LANES:(t + 1) * LANES, cols])
        o_ref[:, cols] = _softmax_pv(tiles, vals).astype(o_ref.dtype)


def _na_attention(qkv, bias, *, n_heads, rows, n_lat):
    blk = NA_QROWS * GRID_W
    assert blk == qkv.shape[0] - n_lat
    nlb = n_lat // blk
    hp = NA_HEADS_PER_STEP
    assert n_heads % hp == 0
    kcol = lambda h: n_heads // hp + h
    vcol = lambda h: 2 * n_heads // hp + h
    prev = lambda b: jnp.clip(b - 1, 0, nlb - 1)
    cur = lambda b: jnp.minimum(b, nlb - 1)
    nxt = lambda b: jnp.clip(b + 1, 0, nlb - 1)
    spec = lambda rowf, colf: pl.BlockSpec((blk, hp * HEAD_DIM), lambda h, b: (rowf(b), colf(h)))
    ctx = lambda b: nlb
    return pl.pallas_call(
        functools.partial(_na_kernel, n_grid_rows=n_lat // GRID_W),
        grid=(n_heads // hp, rows // blk),
        in_specs=[
            spec(lambda b: b, lambda h: h),
            spec(prev, kcol), spec(cur, kcol), spec(nxt, kcol), spec(prev, vcol), spec(cur, vcol), spec(nxt, vcol),
            spec(ctx, kcol), spec(ctx, vcol),
            pl.BlockSpec((hp,) + bias.shape[1:], lambda h, b: (h, 0, 0, 0)),
        ],
        out_specs=spec(lambda b: b, lambda h: h),
        out_shape=jax.ShapeDtypeStruct((rows, n_heads * HEAD_DIM), BF16),
        compiler_params=_params(("arbitrary", "arbitrary")),
        name="na_attention",
    )(*([qkv] * 9), bias)


def kernel(x, c, ctx, c_ctx, ada_down, ada_up, ada_b, norm_mix, norm_ffn, ffn_w_gate_up, ffn_w_down, pool_w, pool_scale, s5_lam_re, s5_lam_im, s5_log_step, s5_b_re, s5_b_im, s5_c_re, s5_c_im, s5_d, s5_w_glu, swa_w_qkv, swa_w_o, swa_q_gain, swa_k_gain, swa_sinks, na_w_qkv, na_w_o, na_q_gain, na_k_gain, na_rpb):
    bsz, n_lat, d = x.shape
    n_ctx = ctx.shape[1]
    assert bsz == 1 and n_lat % ROW_TILE == 0 and n_ctx == ROW_TILE
    n_tot = n_lat + n_ctx
    depth = ada_down.shape[0]
    d_ff = ffn_w_down.shape[1]
    n_heads = d // HEAD_DIM

    xs = jnp.concatenate([x[0], ctx[0]], axis=0)
    mods = _adaln_all(c, c_ctx, ada_down, ada_up, ada_b)

    for layer in range(depth):
        kind = layer % 4
        last = layer == depth - 1
        rows = n_lat if last else n_tot
        mod = mods[layer]
        res = dict(rows=rows, n_lat=n_lat)

        if kind == 0:
            xs = _pool_layer(xs, norm_mix[layer], mod, pool_w, pool_scale, **res)
        else:
            u = _norm_mod(xs, norm_mix[layer], mod, shift_idx=0, rows=n_tot, n_lat=n_lat)
            if kind == 1:
                a_re, a_im, bb_re, bb_im = _s5_prep(s5_lam_re, s5_lam_im, s5_log_step, s5_b_re, s5_b_im)
                y = _s5_scan(u, *_s5_pack(a_re, a_im, bb_re, bb_im, s5_c_re, s5_c_im), n_lat=n_lat)
                g = _s5_gelu(y, u, s5_d, rows=rows)
                half = s5_w_glu.shape[1] // 2
                tm = _pick(rows, (768, 512, 640, 256))
                tn = _pick(half, (256, 128))
                xs = _mm(g, s5_w_glu, rows=rows, tm=tm, tn=tn, out_cols=half, out_dtype=F32,
                         col_maps=(lambda j: j, lambda j, o=half // tn: j + o),
                         epilogue=functools.partial(_epi_glu_residual, gate_idx=2, tm=tm, n_lat=n_lat),
                         extras=_residual_extras(xs, mod, tm, tn), w_outer=True, name="s5_glu")
            elif kind == 2:
                n_kv = n_heads // SWA_GROUP
                qkv = _qkv_proj(u, swa_w_qkv, swa_q_gain, swa_k_gain, n_q=n_heads, n_kv=n_kv,
                                rope_tables=_rope_tables(n_lat, n_ctx), rows=n_tot)
                o = _swa_attention(qkv, swa_sinks, n_q=n_heads, n_kv=n_kv, **res)
                xs = _mm_residual(o, swa_w_o, xs, mod, gate_idx=2, w_outer=True, name="swa_out", **res)
            else:
                qkv = _qkv_proj(u, na_w_qkv, na_q_gain, na_k_gain, n_q=n_heads, n_kv=n_heads,
                                rope_tables=None, rows=n_tot)
                o = _na_attention(qkv, _na_bias(na_rpb), n_heads=n_heads, **res)
                xs = _mm_residual(o, na_w_o, xs, mod, gate_idx=2, w_outer=True, name="na_out", **res)

        hx = _norm_mod(xs, norm_ffn[layer], mod, shift_idx=3, **res)
        tm = _pick(rows, (1056, 1024, 640, 256))
        tn = _pick(d_ff, (256, 128))
        hh, w_down = _mm(hx, ffn_w_gate_up, layer=layer, rows=rows, tm=tm, tn=tn, out_cols=d_ff, out_dtype=BF16,
                         col_maps=(lambda j: j, lambda j, o=d_ff // tn: j + o), epilogue=_epi_swiglu,
                         w_outer=True, name="ffn_gate_up", to_bf16=ffn_w_down)
        xs = _mm_residual(hh, w_down, xs, mod, gate_idx=5, w_outer=False, name="ffn_down", **res)

    return xs[:n_lat][None]
```

```python
import functools
import math

import jax
import jax.numpy as jnp
from jax import lax
from jax.experimental import pallas as pl
from jax.experimental.pallas import tpu as pltpu

F32 = jnp.float32
BF16 = jnp.bfloat16

RMS_EPS = 1e-6
NEG_INF = -1e30
HEAD_DIM = 128
GRID_W = 64
ROPE_BASE = 10000.0
POOL_WINDOWS = (2, 4, 8, 16)
POOL_HALO = 8
S5_GROUP = 16
S5_STATE = 64
S5_CHUNK = 256
S5_BLOCKS = 64
S5_ROW_PAD = 8
SWA_WINDOW = 128
SWA_GROUP = 4
NA_WIN_ROWS = 8
NA_WIN_COLS = 16
NA_QROWS = 4
SWA_KV_PER_STEP = 2
NA_HEADS_PER_STEP = 2

LANES = 128
ROW_TILE = 256
MM_ROW_TILES = (1056, 1024, 640, 256)
V7X_VMEM_LIMIT = 56 * 1024 * 1024


def _pick(n, candidates):
    for c in candidates:
        if n % c == 0:
            return c
    raise ValueError(f"no tile for {n} in {candidates}")


def _params(sem, vmem=V7X_VMEM_LIMIT):
    return pltpu.CompilerParams(dimension_semantics=sem, vmem_limit_bytes=vmem)


def _dot(a, b):
    return jnp.dot(a, b, preferred_element_type=F32)


def _dot_nt(a, b):
    return lax.dot_general(a, b, (((1,), (1,)), ((), ())), preferred_element_type=F32)


def _small_mm_kernel(a_ref, w_ref, b_ref, o_ref, *, silu_in, kc):
    a = a_ref[0]
    if silu_in:
        a = a * jax.nn.sigmoid(a)
    a = a.astype(BF16)
    k = a.shape[1]
    acc = jnp.zeros(o_ref.shape[1:], F32)
    for k0 in range(0, k, kc):
        acc += _dot(a[:, k0:k0 + kc], w_ref[0, k0:k0 + kc, :].astype(BF16))
    o_ref[0] = acc + b_ref[0]


def _small_mm(a, w, b, *, silu_in, tn):
    depth, k, n = w.shape
    shared = a.shape[0] == 1
    kern = functools.partial(_small_mm_kernel, silu_in=silu_in, kc=min(k, 512))
    return pl.pallas_call(
        kern,
        grid=(depth, n // tn),
        in_specs=[
            pl.BlockSpec((1, 8, k), lambda l, j: (0 if shared else l, 0, 0)),
            pl.BlockSpec((1, k, tn), lambda l, j: (l, 0, j)),
            pl.BlockSpec((1, 1, tn), lambda l, j: (l, 0, j)),
        ],
        out_specs=pl.BlockSpec((1, 8, tn), lambda l, j: (l, 0, j)),
        out_shape=jax.ShapeDtypeStruct((depth, 8, n), F32),
        compiler_params=_params(("arbitrary", "arbitrary")),
        name="adaln_mm",
    )(a, w, b)


def _adaln_all(c, c_ctx, ada_down, ada_up, ada_b):
    depth, d, rank = ada_down.shape
    cvec = jnp.zeros((1, 8, d), F32).at[0, 0].set(c[0]).at[0, 1].set(c_ctx)
    t = _small_mm(cvec, ada_down, jnp.zeros((depth, 1, rank), F32), silu_in=True, tn=_pick(rank, (512, 256, 128)))
    m = _small_mm(t, ada_up, ada_b.reshape(depth, 1, 6 * d), silu_in=False, tn=_pick(6 * d, (2048, 1024, 512)))
    return m[:, :2].reshape(depth, 2, 6, d)


def _rms_mod(x, gain, shift, scale):
    ms = jnp.mean(x * x, axis=-1, keepdims=True)
    return (x * lax.rsqrt(ms + RMS_EPS) * gain) * (1.0 + scale) + shift


def _norm_mod_kernel(x_ref, g_ref, mod_ref, o_ref, *, shift_idx):
    u = _rms_mod(x_ref[...], g_ref[...], mod_ref[0, shift_idx:shift_idx + 1, :],
                 mod_ref[0, shift_idx + 1:shift_idx + 2, :])
    o_ref[...] = u.astype(o_ref.dtype)


def _norm_mod(xs, gain, mod, *, shift_idx, rows, n_lat):
    d = xs.shape[1]
    tr = ROW_TILE
    return pl.pallas_call(
        functools.partial(_norm_mod_kernel, shift_idx=shift_idx),
        grid=(rows // tr,),
        in_specs=[
            pl.BlockSpec((tr, d), lambda i: (i, 0)),
            pl.BlockSpec((1, d), lambda i: (0, 0)),
            pl.BlockSpec((1, 6, d), lambda i: (jnp.where(i * tr >= n_lat, 1, 0), 0, 0)),
        ],
        out_specs=pl.BlockSpec((tr, d), lambda i: (i, 0)),
        out_shape=jax.ShapeDtypeStruct((rows, d), BF16),
        compiler_params=_params(("arbitrary",)),
        name="norm_mod",
    )(xs, gain.reshape(1, d), mod)


def _mm(a, w, *, rows, tm, tn, out_cols, out_dtype, col_maps, epilogue, extras=(), w_outer, name,
        layer=None, to_bf16=None):
    k = a.shape[1]
    n_i, n_j = rows // tm, out_cols // tn
    cast = w.dtype != BF16
    assert w_outer or not cast
    n_parts = len(col_maps)
    kc = _pick(k, (512, 256, 128))

    if w_outer:
        grid = (n_j, n_i)
        ij = lambda g0, g1: (g1, g0)
    else:
        grid = (n_i, n_j)
        ij = lambda g0, g1: (g0, g1)

    if w.ndim == 3:
        w_block, w_index = (None, k, tn), lambda col: (layer, 0, col)
    else:
        w_block, w_index = (k, tn), lambda col: (0, col)
    in_specs = [pl.BlockSpec((tm, k), lambda g0, g1: (ij(g0, g1)[0], 0))]
    for cm in col_maps:
        in_specs.append(pl.BlockSpec(w_block, lambda g0, g1, cm=cm: w_index(cm(ij(g0, g1)[1]))))
    for _, shape, fn in extras:
        in_specs.append(pl.BlockSpec(shape, lambda g0, g1, fn=fn: fn(*ij(g0, g1))))
    n_ex = len(extras)
    out_specs = [pl.BlockSpec((tm, tn), lambda g0, g1: ij(g0, g1))]
    out_shape = [jax.ShapeDtypeStruct((rows, out_cols), out_dtype)]
    n_side = 0 if to_bf16 is None else 1
    if n_side:
        _, side_r, side_c = to_bf16.shape
        slab = side_r // (grid[0] * grid[1])
        assert slab * grid[0] * grid[1] == side_r and slab % 16 == 0
        in_specs.append(pl.BlockSpec((None, slab, side_c), lambda g0, g1: (layer, g0 * grid[1] + g1, 0)))
        out_specs.append(pl.BlockSpec((slab, side_c), lambda g0, g1: (g0 * grid[1] + g1, 0)))
        out_shape.append(jax.ShapeDtypeStruct((side_r, side_c), BF16))

    def body(*refs):
        a_ref = refs[0]
        w_refs = refs[1:1 + n_parts]
        ex_refs = refs[1 + n_parts:1 + n_parts + n_ex]
        n_in = 1 + n_parts + n_ex + n_side
        o_ref = refs[n_in]
        scratch = refs[n_in + 1 + n_side:]
        i, j = ij(pl.program_id(0), pl.program_id(1))
        if n_side:
            refs[n_in + 1][...] = refs[n_in - 1][...].astype(BF16)
        if cast:
            @pl.when(i == 0)
            def _():
                def cp(c, carry):
                    sl = pl.ds(pl.multiple_of(c * kc, kc), kc)
                    for p in range(n_parts):
                        scratch[p][sl, :] = w_refs[p][sl, :].astype(BF16)
                    return carry
                lax.fori_loop(0, k // kc, cp, 0)
            ws = scratch
        else:
            ws = w_refs
        av = a_ref[...]
        accs = [_dot(av, ws[p][...]) for p in range(n_parts)]
        epilogue(accs, i, j, ex_refs, o_ref)

    outs = pl.pallas_call(
        body,
        grid=grid,
        in_specs=in_specs,
        out_specs=out_specs,
        out_shape=out_shape,
        scratch_shapes=[pltpu.VMEM((k, tn), BF16) for _ in range(n_parts)] if cast else [],
        compiler_params=_params(("arbitrary", "arbitrary")),
        name=name,
    )(a, *([w] * n_parts), *[e[0] for e in extras], *([to_bf16] if n_side else []))
    return outs if n_side else outs[0]


def _row_gate(mod_ref, idx, i, tm, n_lat):
    rows = i * tm + lax.broadcasted_iota(jnp.int32, (tm, 1), 0)
    return jnp.where(rows < n_lat, mod_ref[0, idx:idx + 1, :], mod_ref[1, idx:idx + 1, :])


def _epi_swiglu(accs, i, j, ex, o_ref):
    a, g = accs
    o_ref[...] = (a * jax.nn.sigmoid(a) * g).astype(o_ref.dtype)


def _epi_residual(accs, i, j, ex, o_ref, *, gate_idx, tm, n_lat):
    res_ref, mod_ref = ex
    o_ref[...] = res_ref[...] + _row_gate(mod_ref, gate_idx, i, tm, n_lat) * accs[0]


def _epi_glu_residual(accs, i, j, ex, o_ref, *, gate_idx, tm, n_lat):
    res_ref, mod_ref = ex
    a, g = accs
    o_ref[...] = res_ref[...] + _row_gate(mod_ref, gate_idx, i, tm, n_lat) * (a * jax.nn.sigmoid(g))


def _residual_extras(xs, mod, tm, tn):
    return ((xs, (tm, tn), lambda i, j: (i, j)), (mod, (2, 6, tn), lambda i, j: (0, 0, j)))


def _mm_residual(a, w, xs, mod, *, gate_idx, rows, n_lat, w_outer, name):
    n = w.shape[1]
    tm = _pick(rows, MM_ROW_TILES if w_outer else (768, 512, 640, 256))
    tn = _pick(n, (512, 256) if w_outer else (256, 128))
    return _mm(a, w, rows=rows, tm=tm, tn=tn, out_cols=n, out_dtype=F32, col_maps=(lambda j: j,),
               epilogue=functools.partial(_epi_residual, gate_idx=gate_idx, tm=tm, n_lat=n_lat),
               extras=_residual_extras(xs, mod, tm, tn), w_outer=w_outer, name=name)


def _pool_kernel(xp_ref, x_ref, xn_ref, c_ref, g_ref, gf_ref, mod_ref, pw_ref, ps_ref, o_ref, h_ref, *, tr, n_lat, n_tot):
    i = pl.program_id(0)
    halo = POOL_HALO
    gain = g_ref[...]
    shift, scale, gate = mod_ref[0, 0:1, :], mod_ref[0, 1:2, :], mod_ref[0, 2:3, :]
    is_lat = i * tr < n_lat
    lat_rows = lax.broadcasted_iota(jnp.int32, (tr, 1), 0) * 0 + jnp.where(is_lat, 1, 0) > 0
    x = jnp.where(lat_rows, x_ref[...], c_ref[...])
    ue = jnp.concatenate([_rms_mod(xp_ref[...], gain, shift, scale), _rms_mod(x, gain, shift, scale),
                          _rms_mod(xn_ref[...], gain, shift, scale)], axis=0)
    seq_lo = jnp.where(is_lat, 0, n_lat)
    seq_hi = jnp.where(is_lat, n_lat, n_tot)
    t_g = i * tr + lax.broadcasted_iota(jnp.int32, (tr, 1), 0)
    s_g = i * tr - halo + lax.broadcasted_iota(jnp.int32, (1, tr + 2 * halo), 1)
    s_ok = (s_g >= seq_lo) & (s_g < seq_hi)
    pg = ue.shape[1] // len(POOL_WINDOWS)
    for gi, w in enumerate(POOL_WINDOWS):
        cols = slice(gi * pg, (gi + 1) * pg)
        ug = ue[:, cols]
        band = (s_g >= t_g - w // 2) & (s_g < t_g + w // 2) & s_ok
        ssum = _dot(jnp.where(band, 1.0, 0.0).astype(BF16), ug.astype(BF16))
        cnt = jnp.minimum(t_g + w // 2, seq_hi) - jnp.maximum(t_g - w // 2, seq_lo)
        p = ssum / cnt.astype(F32) - ug[halo:halo + tr]
        y = _dot(p.astype(BF16), pw_ref[gi]) * ps_ref[:, cols]
        o_ref[:, cols] = x[:, cols] + gate[:, cols] * y
    h_ref[...] = _rms_mod(o_ref[...], gf_ref[...], mod_ref[0, 3:4, :], mod_ref[0, 4:5, :]).astype(h_ref.dtype)


def _pool_layer(lat, ctx_src, ctx_block, gain, gain_ffn, mod, pool_w, pool_scale, *, rows, n_lat, n_tot):
    d = lat.shape[1]
    tr, halo = ROW_TILE, POOL_HALO
    hb = tr // halo
    last_hb = n_lat // halo - 1
    vec = pl.BlockSpec((1, d), lambda i: (0, 0))
    tile = pl.BlockSpec((tr, d), lambda i: (i, 0))
    return pl.pallas_call(
        functools.partial(_pool_kernel, tr=tr, n_lat=n_lat, n_tot=n_tot),
        grid=(rows // tr,),
        in_specs=[
            pl.BlockSpec((halo, d), lambda i: (jnp.clip(i * hb - 1, 0, last_hb), 0)),
            pl.BlockSpec((tr, d), lambda i: (jnp.minimum(i, n_lat // tr - 1), 0)),
            pl.BlockSpec((halo, d), lambda i: (jnp.minimum((i + 1) * hb, last_hb), 0)),
            pl.BlockSpec((tr, d), lambda i: (ctx_block, 0)),
            vec, vec,
            pl.BlockSpec((1, 6, d), lambda i: (jnp.where(i * tr >= n_lat, 1, 0), 0, 0)),
            pl.BlockSpec(pool_w.shape, lambda i: (0, 0, 0)),
            vec,
        ],
        out_specs=[tile, tile],
        out_shape=[jax.ShapeDtypeStruct((rows, d), F32), jax.ShapeDtypeStruct((rows, d), BF16)],
        compiler_params=_params(("arbitrary",)),
        name="pool_mixer",
    )(lat, lat, lat, ctx_src, gain.reshape(1, d), gain_ffn.reshape(1, d), mod, pool_w.astype(BF16),
      pool_scale.reshape(1, d))


def _s5_prep_kernel(lre_ref, lim_ref, ls_ref, bre_ref, bim_ref, are_ref, aim_ref, bbre_ref, bbim_ref):
    lam_re, lam_im = lre_ref[0], lim_ref[0]
    dt = jnp.exp(ls_ref[0])
    z_re, z_im = lam_re * dt, lam_im * dt
    mag = jnp.exp(z_re)
    a_re, a_im = mag * jnp.cos(z_im), mag * jnp.sin(z_im)
    den = lam_re * lam_re + lam_im * lam_im
    k_re = ((a_re - 1.0) * lam_re + a_im * lam_im) / den
    k_im = (a_im * lam_re - (a_re - 1.0) * lam_im) / den
    are_ref[0] = a_re
    aim_ref[0] = a_im
    b_re, b_im = bre_ref[0], bim_ref[0]
    bbre_ref[0] = k_re[:, None, :] * b_re - k_im[:, None, :] * b_im
    bbim_ref[0] = k_re[:, None, :] * b_im + k_im[:, None, :] * b_re


def _s5_prep(lam_re, lam_im, log_step, b_re, b_im):
    _, g, p, h = b_re.shape
    spec2 = pl.BlockSpec((1, g, p), lambda d: (d, 0, 0))
    spec3 = pl.BlockSpec((1, g, h, p), lambda d: (d, 0, 0, 0))
    return pl.pallas_call(
        _s5_prep_kernel,
        grid=(2,),
        in_specs=[spec2, spec2, pl.BlockSpec((1, g, 1), lambda d: (d, 0, 0)), spec3, spec3],
        out_specs=[spec2, spec2, spec3, spec3],
        out_shape=[jax.ShapeDtypeStruct((2, g, p), F32)] * 2 + [jax.ShapeDtypeStruct((2, g, h, p), F32)] * 2,
        compiler_params=_params(("arbitrary",)),
        name="s5_prep",
    )(lam_re, lam_im, log_step[..., None], jnp.swapaxes(b_re, 2, 3), jnp.swapaxes(b_im, 2, 3))


def _s5_pack(a_re, a_im, bb_re, bb_im, c_re, c_im):
    _, g, h, p = bb_re.shape
    r = g // 2
    eye2 = jnp.eye(2, dtype=F32)
    eye4 = jnp.eye(4, dtype=F32)
    slot = eye4[jnp.arange(r) % 4]

    def win_part(bb):
        t = bb.reshape(2, r, 2, h, p)
        t = t[:, :, :, :, None, :] * eye2[None, None, :, None, :, None]
        t = t.reshape(2, r, 2 * h, 2 * p)
        t = slot[None, :, :, None, None] * t[:, :, None, :, :]
        return t.reshape(2, r, 4 * 2 * h, 2 * p)

    def c_part(c):
        t = jnp.swapaxes(c, 2, 3).reshape(2, r, 2, p, h)
        t = t[:, :, :, :, None, :] * eye2[None, None, :, None, :, None]
        t = t.reshape(2, r, 2 * p, 2 * h)
        t = t[:, :, :, None, :] * slot[None, :, None, :, None]
        return t.reshape(2, r, 2 * p, 4 * 2 * h)

    win = jnp.concatenate([win_part(bb_re), win_part(bb_im)], axis=-1).astype(BF16)
    cmat = jnp.concatenate([c_part(c_re), c_part(-c_im)], axis=2).astype(BF16)
    return win, cmat, a_re.reshape(2, r, 2 * p), a_im.reshape(2, r, 2 * p)


def _s5_scan_kernel(u_ref, win_ref, cmat_ref, are_ref, aim_ref, y_ref, xre, xim, hre, him, *, nsub, rb):
    g0, c = pl.program_id(0), pl.program_id(1)
    rev = g0 // nsub == 1
    tc, pitch = S5_CHUNK, rb + S5_ROW_PAD
    block_rows = lambda r: pl.ds(r, tc, stride=pitch)

    @pl.when(c == 0)
    def _():
        hre[...] = jnp.zeros_like(hre)
        him[...] = jnp.zeros_like(him)

    for q in range(rb // 4):
        ut = u_ref[:, q * LANES:(q + 1) * LANES]
        for s in range(4):
            r = q * 4 + s
            x = _dot(ut, win_ref[0, r])
            xre[block_rows(r), :] = x[:, :LANES]
            xim[block_rows(r), :] = x[:, LANES:]

    a_re, a_im = are_ref[0], aim_ref[0]

    def step(t, carry):
        h_re, h_im = carry
        tt = jnp.where(rev, tc - 1 - t, t)
        rows = pl.ds(pl.multiple_of(tt * pitch, 8), rb)
        n_re = a_re * h_re - a_im * h_im + xre[rows, :]
        n_im = a_re * h_im + a_im * h_re + xim[rows, :]
        xre[rows, :] = n_re
        xim[rows, :] = n_im
        return n_re, n_im

    h_re, h_im = lax.fori_loop(0, tc, step, (hre[...], him[...]), unroll=8)
    hre[...] = h_re
    him[...] = h_im

    for q in range(rb // 4):
        acc = jnp.zeros((tc, LANES), F32)
        for s in range(4):
            r = q * 4 + s
            acc += (_dot(xre[block_rows(r), :].astype(BF16), cmat_ref[0, r, :LANES, :])
                    + _dot(xim[block_rows(r), :].astype(BF16), cmat_ref[0, r, LANES:, :]))
        y_ref[0, :, q * LANES:(q + 1) * LANES] = acc.astype(y_ref.dtype)


def _s5_scan(u, win, cmat, a_re, a_im, *, n_lat):
    n_tot, d = u.shape
    r_all = win.shape[1]
    rb = min(S5_BLOCKS, r_all)
    nsub = r_all // rb
    tc = S5_CHUNK
    n_c, n_lc = n_tot // tc, n_lat // tc
    cols = rb * 2 * S5_GROUP

    def split(w):
        return w.reshape((2 * nsub, rb) + w.shape[2:])

    def chunk(g0, c):
        return jnp.where(g0 // nsub == 0, (c + n_lc) % n_c, n_c - 1 - c)

    wspec = lambda shape: pl.BlockSpec((1,) + shape, lambda g0, c: (g0,) + (0,) * len(shape))
    return pl.pallas_call(
        functools.partial(_s5_scan_kernel, nsub=nsub, rb=rb),
        grid=(2 * nsub, n_c),
        in_specs=[
            pl.BlockSpec((tc, cols), lambda g0, c: (chunk(g0, c), g0 % nsub)),
            wspec((rb, LANES, 2 * LANES)), wspec((rb, 2 * LANES, LANES)), wspec((rb, LANES)), wspec((rb, LANES)),
        ],
        out_specs=pl.BlockSpec((1, tc, cols), lambda g0, c: (g0 // nsub, chunk(g0, c), g0 % nsub)),
        out_shape=jax.ShapeDtypeStruct((2, n_tot, d), BF16),
        scratch_shapes=[pltpu.VMEM((tc * (rb + S5_ROW_PAD), LANES), F32)] * 2 + [pltpu.VMEM((rb, LANES), F32)] * 2,
        compiler_params=_params(("arbitrary", "arbitrary")),
        name="s5_scan",
    )(u, split(win), split(cmat), split(a_re), split(a_im))


def _s5_gelu_kernel(y_ref, u_ref, d_ref, o_ref):
    v = y_ref[0].astype(F32) + y_ref[1].astype(F32) + d_ref[...] * u_ref[...].astype(F32)
    o_ref[...] = jax.nn.gelu(v, approximate=True).astype(o_ref.dtype)


def _s5_gelu(y, u, dvec, *, rows):
    d = u.shape[1]
    tr = ROW_TILE
    return pl.pallas_call(
        _s5_gelu_kernel,
        grid=(rows // tr,),
        in_specs=[pl.BlockSpec((2, tr, d), lambda i: (0, i, 0)), pl.BlockSpec((tr, d), lambda i: (i, 0)),
                  pl.BlockSpec((1, d), lambda i: (0, 0))],
        out_specs=pl.BlockSpec((tr, d), lambda i: (i, 0)),
        out_shape=jax.ShapeDtypeStruct((rows, d), BF16),
        compiler_params=_params(("arbitrary",)),
        name="s5_gelu",
    )(y, u, dvec.reshape(1, d))


def _rope_tables(n_lat, n_ctx):
    quarter = HEAD_DIM // 4
    inv = jnp.power(ROPE_BASE, -jnp.arange(quarter, dtype=F32) / quarter)
    t = jnp.arange(n_lat)
    ang_r = (t // GRID_W).astype(F32)[:, None] * inv
    ang_c = (t % GRID_W).astype(F32)[:, None] * inv
    ang = jnp.concatenate([ang_r, ang_r, ang_c, ang_c], axis=-1)
    sign = jnp.where((jnp.arange(HEAD_DIM) % (2 * quarter)) < quarter, -1.0, 1.0)
    cos = jnp.concatenate([jnp.cos(ang), jnp.ones((n_ctx, HEAD_DIM), F32)], axis=0)
    sin = jnp.concatenate([jnp.sin(ang) * sign, jnp.zeros((n_ctx, HEAD_DIM), F32)], axis=0)
    return cos, sin


def _epi_qkv(accs, i, j, ex, o_ref, *, n_norm_tiles, rope, tn):
    gain_ref = ex[0]
    heads_per_acc = tn // len(accs) // HEAD_DIM

    @pl.when(j < n_norm_tiles)
    def _():
        quarter = HEAD_DIM // 4
        lane = lax.broadcasted_iota(jnp.int32, (1, HEAD_DIM), 1)
        first = (lane % (2 * quarter)) < quarter
        for h in range(tn // HEAD_DIM):
            cols = slice(h * HEAD_DIM, (h + 1) * HEAD_DIM)
            hl = h % heads_per_acc
            x = accs[h // heads_per_acc][:, hl * HEAD_DIM:(hl + 1) * HEAD_DIM]
            ms = jnp.mean(x * x, axis=-1, keepdims=True)
            xn = x * lax.rsqrt(ms + RMS_EPS) * gain_ref[:, cols]
            if rope:
                cos_ref, sin_ref = ex[1], ex[2]
                partner = jnp.where(first, pltpu.roll(xn, HEAD_DIM - quarter, 1), pltpu.roll(xn, quarter, 1))
                xn = xn * cos_ref[...] + partner * sin_ref[...]
            o_ref[:, cols] = xn.astype(o_ref.dtype)

    @pl.when(j >= n_norm_tiles)
    def _():
        sub = tn // len(accs)
        for s, acc in enumerate(accs):
            o_ref[:, s * sub:(s + 1) * sub] = acc.astype(o_ref.dtype)


def _qkv_proj(u, w_qkv, q_gain, k_gain, *, n_q, n_kv, rope_tables, rows):
    n = w_qkv.shape[1]
    tm = _pick(rows, MM_ROW_TILES)
    tn = _pick(n_kv * HEAD_DIM, (512, 256, 128))
    gain = jnp.concatenate([jnp.tile(q_gain * HEAD_DIM ** -0.5, n_q), jnp.tile(k_gain, n_kv),
                            jnp.ones((n_kv * HEAD_DIM,), F32)]).reshape(1, n)
    extras = [(gain, (1, tn), lambda i, j: (0, j))]
    if rope_tables is not None:
        extras += [(t, (tm, HEAD_DIM), lambda i, j: (i, 0)) for t in rope_tables]
    epi = functools.partial(_epi_qkv, n_norm_tiles=(n_q + n_kv) * HEAD_DIM // tn, rope=rope_tables is not None, tn=tn)
    return _mm(u, w_qkv, rows=rows, tm=tm, tn=tn, out_cols=n, out_dtype=BF16, col_maps=(lambda j: j,),
               epilogue=epi, extras=tuple(extras), w_outer=True, name="qkv_proj")


def _softmax_pv(tiles, values, extra_logit=None):
    rows = tiles[0].shape[0]
    mx = functools.reduce(jnp.maximum, tiles)
    if extra_logit is not None:
        mx = jnp.maximum(mx, extra_logit)
    m = jnp.broadcast_to(jnp.max(mx, axis=-1, keepdims=True), (rows, LANES))
    p = jnp.concatenate([jnp.exp(t - m).astype(BF16) for t in tiles], axis=1)
    ones = jnp.ones((LANES, LANES), BF16)
    v_aug = jnp.concatenate([jnp.concatenate([v, ones], axis=1) for v in values], axis=0)
    acc = _dot(p, v_aug)
    den = acc[:, LANES:]
    if extra_logit is not None:
        den = den + jnp.exp(extra_logit - m)
    return acc[:, :LANES] / den


def _swa_kernel(sink_ref, q_ref, kp_ref, kc_ref, kn_ref, vp_ref, vc_ref, vn_ref, kx_ref, vx_ref, o_ref, *, n_lat_blocks):
    h, b = pl.program_id(0), pl.program_id(1)
    blk = SWA_WINDOW
    is_lat = b < n_lat_blocks
    qi = lax.broadcasted_iota(jnp.int32, (SWA_GROUP * blk, 1), 0) & (blk - 1)
    kj = lax.broadcasted_iota(jnp.int32, (1, blk), 1)
    off = lambda ok: jnp.where(ok, 0, blk + 1)
    m_prev = kj >= qi + off(is_lat & (b > 0))
    m_cur = kj >= qi * 0 + off(is_lat)
    m_next = kj <= qi - off(is_lat & (b < n_lat_blocks - 1))
    n_ctx_tiles = kx_ref.shape[0] // LANES
    for hh in range(SWA_KV_PER_STEP):
        kcols = slice(hh * HEAD_DIM, (hh + 1) * HEAD_DIM)
        qh = lambda g: (hh * SWA_GROUP + g) * HEAD_DIM
        q = jnp.concatenate([q_ref[:, qh(g):qh(g) + HEAD_DIM] for g in range(SWA_GROUP)], axis=0)
        sink = jnp.concatenate([jnp.full((blk, LANES), sink_ref[(h * SWA_KV_PER_STEP + hh) * SWA_GROUP + g], F32)
                                for g in range(SWA_GROUP)], axis=0)
        s_ctx = _dot_nt(q, kx_ref[:, kcols])
        tiles = [jnp.where(m_prev, _dot_nt(q, kp_ref[:, kcols]), NEG_INF),
                 jnp.where(m_cur, _dot_nt(q, kc_ref[:, kcols]), NEG_INF),
                 jnp.where(m_next, _dot_nt(q, kn_ref[:, kcols]), NEG_INF)]
        tiles += [s_ctx[:, t * LANES:(t + 1) * LANES] for t in range(n_ctx_tiles)]
        vals = [vp_ref[:, kcols], vc_ref[:, kcols], vn_ref[:, kcols]]
        vals += [vx_ref[t * LANES:(t + 1) * LANES, kcols] for t in range(n_ctx_tiles)]
        o = _softmax_pv(tiles, vals, sink).astype(o_ref.dtype)
        for g in range(SWA_GROUP):
            o_ref[:, qh(g):qh(g) + HEAD_DIM] = o[g * blk:(g + 1) * blk]


def _swa_attention(qkv, sinks, *, n_q, n_kv, rows, n_lat):
    blk = SWA_WINDOW
    nlb = n_lat // blk
    ctx_rows = (qkv.shape[0] - n_lat)
    hp = SWA_KV_PER_STEP
    assert n_kv % hp == 0 and n_q % hp == 0
    kcol = lambda h: n_q // hp + h
    vcol = lambda h: (n_q + n_kv) // hp + h
    prev = lambda b: jnp.clip(b - 1, 0, nlb - 1)
    cur = lambda b: jnp.minimum(b, nlb - 1)
    nxt = lambda b: jnp.clip(b + 1, 0, nlb - 1)
    kv = lambda rowf, colf: pl.BlockSpec((blk, hp * HEAD_DIM), lambda h, b: (rowf(b), colf(h)))
    ctx = lambda colf: pl.BlockSpec((ctx_rows, hp * HEAD_DIM), lambda h, b: (n_lat // ctx_rows, colf(h)))
    return pl.pallas_call(
        functools.partial(_swa_kernel, n_lat_blocks=nlb),
        grid=(n_kv // hp, rows // blk),
        in_specs=[
            pl.BlockSpec(memory_space=pltpu.SMEM),
            pl.BlockSpec((blk, hp * SWA_GROUP * HEAD_DIM), lambda h, b: (b, h)),
            kv(prev, kcol), kv(cur, kcol), kv(nxt, kcol), kv(prev, vcol), kv(cur, vcol), kv(nxt, vcol),
            ctx(kcol), ctx(vcol),
        ],
        out_specs=pl.BlockSpec((blk, hp * SWA_GROUP * HEAD_DIM), lambda h, b: (b, h)),
        out_shape=jax.ShapeDtypeStruct((rows, n_q * HEAD_DIM), BF16),
        compiler_params=_params(("arbitrary", "arbitrary")),
        name="swa_attention",
    )(sinks, *([qkv] * 9))


def _na_bias_kernel(rpb_ref, o_ref):
    h = pl.program_id(0)
    n_ri, n_ci = 2 * NA_WIN_ROWS - 1, 2 * NA_WIN_COLS - 1
    cq = lax.broadcasted_iota(jnp.int32, (GRID_W, 2 * GRID_W), 0)
    ck = lax.broadcasted_iota(jnp.int32, (GRID_W, 2 * GRID_W), 1)
    second = ck >= GRID_W
    ck = jnp.where(second, ck - GRID_W, ck)
    c0 = jnp.clip(cq - NA_WIN_COLS // 2, 0, GRID_W - NA_WIN_COLS)
    in_win = (ck >= c0) & (ck < c0 + NA_WIN_COLS)
    cidx = jnp.clip(ck - cq + NA_WIN_COLS - 1, 0, n_ci - 1)
    neg = jnp.full((GRID_W, 2 * GRID_W), NEG_INF, F32)
    per_offset = []
    for ri in range(n_ri):
        t = neg
        for dci in range(n_ci):
            t = jnp.where(cidx == dci, rpb_ref[(h * n_ri + ri) * n_ci + dci], t)
        per_offset.append(jnp.where(in_win, t, NEG_INF))
    for k in range(n_ri + 1):
        o_ref[0, k] = jnp.where(second, per_offset[k] if k < n_ri else neg, per_offset[k - 1] if k >= 1 else neg)


def _na_bias(rpb):
    n_heads = rpb.shape[0]
    n_tiles = 2 * NA_WIN_ROWS
    return pl.pallas_call(
        _na_bias_kernel,
        grid=(n_heads,),
        in_specs=[pl.BlockSpec(memory_space=pltpu.SMEM)],
        out_specs=pl.BlockSpec((1, n_tiles, GRID_W, 2 * GRID_W), lambda h: (h, 0, 0, 0)),
        out_shape=jax.ShapeDtypeStruct((n_heads, n_tiles, GRID_W, 2 * GRID_W), F32),
        compiler_params=_params(("arbitrary",)),
        name="na_bias",
    )(rpb.reshape(-1))


def _na_kernel(q_ref, kp_ref, kc_ref, kn_ref, vp_ref, vc_ref, vn_ref, kx_ref, vx_ref, bias_ref, o_ref, *, n_grid_rows):
    b = pl.program_id(1)
    qr, w = NA_QROWS, GRID_W
    is_lat = b * qr < n_grid_rows
    lane = lax.broadcasted_iota(jnp.int32, (1, 2 * w), 1)
    n_kt = qr // 2
    windows = {}
    for kb in range(3):
        for dq in range(qr):
            r = b * qr + dq
            r0 = jnp.clip(r - NA_WIN_ROWS // 2, 0, n_grid_rows - NA_WIN_ROWS)
            for jt in range(n_kt):
                kr = b * qr + (kb - 1) * qr + 2 * jt
                ok0 = (kr >= r0) & (kr < r0 + NA_WIN_ROWS) & is_lat
                ok1 = (kr + 1 >= r0) & (kr + 1 < r0 + NA_WIN_ROWS) & is_lat
                lo, hi = jnp.where(ok0, 0, w), jnp.where(ok1, 2 * w, w)
                windows[kb, dq, jt] = (lane >= lo) & (lane < hi)
    for hh in range(NA_HEADS_PER_STEP):
        cols = slice(hh * HEAD_DIM, (hh + 1) * HEAD_DIM)
        q = q_ref[:, cols]
        tiles, vals = [], []
        for kb, (k_ref, v_ref) in enumerate(((kp_ref, vp_ref), (kc_ref, vc_ref), (kn_ref, vn_ref))):
            s = _dot_nt(q, k_ref[:, cols])
            for jt in range(n_kt):
                parts = []
                for dq in range(qr):
                    ri0 = (kb - 1) * qr + 2 * jt - dq + NA_WIN_ROWS - 1
                    t = s[dq * w:(dq + 1) * w, jt * 2 * w:(jt + 1) * 2 * w] + bias_ref[hh, ri0 + 1]
                    parts.append(jnp.where(windows[kb, dq, jt], t, NEG_INF))
                tiles.append(jnp.concatenate(parts, axis=0))
                vals.append(v_ref[jt * LANES:(jt + 1) * LANES, cols])
        s_ctx = _dot_nt(q, kx_ref[:, cols])
        for t in range(kx_ref.shape[0] // LANES):
            tiles.append(s_ctx[:, t * LANES:(t + 1) * LANES])
            vals.append(vx_ref[t * LANES:(t + 1) * LANES, cols])
        o_ref[:, cols] = _softmax_pv(tiles, vals).astype(o_ref.dtype)


def _na_attention(qkv, bias, *, n_heads, rows, n_lat):
    blk = NA_QROWS * GRID_W
    assert blk == qkv.shape[0] - n_lat
    nlb = n_lat // blk
    hp = NA_HEADS_PER_STEP
    assert n_heads % hp == 0
    kcol = lambda h: n_heads // hp + h
    vcol = lambda h: 2 * n_heads // hp + h
    prev = lambda b: jnp.clip(b - 1, 0, nlb - 1)
    cur = lambda b: jnp.minimum(b, nlb - 1)
    nxt = lambda b: jnp.clip(b + 1, 0, nlb - 1)
    spec = lambda rowf, colf: pl.BlockSpec((blk, hp * HEAD_DIM), lambda h, b: (rowf(b), colf(h)))
    ctx = lambda b: nlb
    return pl.pallas_call(
        functools.partial(_na_kernel, n_grid_rows=n_lat // GRID_W),
        grid=(n_heads // hp, rows // blk),
        in_specs=[
            spec(lambda b: b, lambda h: h),
            spec(prev, kcol), spec(cur, kcol), spec(nxt, kcol), spec(prev, vcol), spec(cur, vcol), spec(nxt, vcol),
            spec(ctx, kcol), spec(ctx, vcol),
            pl.BlockSpec((hp,) + bias.shape[1:], lambda h, b: (h, 0, 0, 0)),
        ],
        out_specs=spec(lambda b: b, lambda h: h),
        out_shape=jax.ShapeDtypeStruct((rows, n_heads * HEAD_DIM), BF16),
        compiler_params=_params(("arbitrary", "arbitrary")),
        name="na_attention",
    )(*([qkv] * 9), bias)


def kernel(x, c, ctx, c_ctx, ada_down, ada_up, ada_b, norm_mix, norm_ffn, ffn_w_gate_up, ffn_w_down, pool_w, pool_scale, s5_lam_re, s5_lam_im, s5_log_step, s5_b_re, s5_b_im, s5_c_re, s5_c_im, s5_d, s5_w_glu, swa_w_qkv, swa_w_o, swa_q_gain, swa_k_gain, swa_sinks, na_w_qkv, na_w_o, na_q_gain, na_k_gain, na_rpb):
    bsz, n_lat, d = x.shape
    n_ctx = ctx.shape[1]
    assert bsz == 1 and n_lat % ROW_TILE == 0 and n_ctx == ROW_TILE
    n_tot = n_lat + n_ctx
    depth = ada_down.shape[0]
    d_ff = ffn_w_down.shape[1]
    n_heads = d // HEAD_DIM

    mods = _adaln_all(c, c_ctx, ada_down, ada_up, ada_b)
    xs = None

    for layer in range(depth):
        kind = layer % 4
        last = layer == depth - 1
        rows = n_lat if last else n_tot
        mod = mods[layer]
        res = dict(rows=rows, n_lat=n_lat)

        if kind == 0:
            lat, ctx_src, ctx_block = (x[0], ctx[0], 0) if xs is None else (xs, xs, n_lat // ROW_TILE)
            xs, hx = _pool_layer(lat, ctx_src, ctx_block, norm_mix[layer], norm_ffn[layer], mod, pool_w, pool_scale,
                                 n_tot=n_tot, **res)
        else:
            u = _norm_mod(xs, norm_mix[layer], mod, shift_idx=0, rows=n_tot, n_lat=n_lat)
            if kind == 1:
                a_re, a_im, bb_re, bb_im = _s5_prep(s5_lam_re, s5_lam_im, s5_log_step, s5_b_re, s5_b_im)
                y = _s5_scan(u, *_s5_pack(a_re, a_im, bb_re, bb_im, s5_c_re, s5_c_im), n_lat=n_lat)
                g = _s5_gelu(y, u, s5_d, rows=rows)
                half = s5_w_glu.shape[1] // 2
                tm = _pick(rows, MM_ROW_TILES)
                tn = _pick(half, (256, 128))
                xs = _mm(g, s5_w_glu, rows=rows, tm=tm, tn=tn, out_cols=half, out_dtype=F32,
                         col_maps=(lambda j: j, lambda j, o=half // tn: j + o),
                         epilogue=functools.partial(_epi_glu_residual, gate_idx=2, tm=tm, n_lat=n_lat),
                         extras=_residual_extras(xs, mod, tm, tn), w_outer=True, name="s5_glu")
            elif kind == 2:
                n_kv = n_heads // SWA_GROUP
                qkv = _qkv_proj(u, swa_w_qkv, swa_q_gain, swa_k_gain, n_q=n_heads, n_kv=n_kv,
                                rope_tables=_rope_tables(n_lat, n_ctx), rows=n_tot)
                o = _swa_attention(qkv, swa_sinks, n_q=n_heads, n_kv=n_kv, **res)
                xs = _mm_residual(o, swa_w_o, xs, mod, gate_idx=2, w_outer=True, name="swa_out", **res)
            else:
                qkv = _qkv_proj(u, na_w_qkv, na_q_gain, na_k_gain, n_q=n_heads, n_kv=n_heads,
                                rope_tables=None, rows=n_tot)
                o = _na_attention(qkv, _na_bias(na_rpb), n_heads=n_heads, **res)
                xs = _mm_residual(o, na_w_o, xs, mod, gate_idx=2, w_outer=True, name="na_out", **res)

            hx = _norm_mod(xs, norm_ffn[layer], mod, shift_idx=3, **res)
        tm = _pick(rows, MM_ROW_TILES)
        tn = _pick(d_ff, (256, 128))
        hh, w_down = _mm(hx, ffn_w_gate_up, layer=layer, rows=rows, tm=tm, tn=tn, out_cols=d_ff, out_dtype=BF16,
                         col_maps=(lambda j: j, lambda j, o=d_ff // tn: j + o), epilogue=_epi_swiglu,
                         w_outer=True, name="ffn_gate_up", to_bf16=ffn_w_down)
        xs = _mm_residual(hh, w_down, xs, mod, gate_idx=5, w_outer=False, name="ffn_down", **res)

    return xs[:n_lat][None]
```

```python
import functools
import math

import jax
import jax.numpy as jnp
from jax import lax
from jax.experimental import pallas as pl
from jax.experimental.pallas import tpu as pltpu

F32 = jnp.float32
BF16 = jnp.bfloat16

RMS_EPS = 1e-6
NEG_INF = -1e30
HEAD_DIM = 128
GRID_W = 64
ROPE_BASE = 10000.0
POOL_WINDOWS = (2, 4, 8, 16)
POOL_HALO = 8
S5_GROUP = 16
S5_STATE = 64
S5_CHUNK = 256
S5_BLOCKS = 64
S5_ROW_PAD = 8
SWA_WINDOW = 128
SWA_GROUP = 4
NA_WIN_ROWS = 8
NA_WIN_COLS = 16
NA_QROWS = 4
SWA_KV_PER_STEP = 8
NA_HEADS_PER_STEP = 8

LANES = 128
ROW_TILE = 256
MM_ROW_TILES = (1408, 1024, 640, 256)
MM_ROW_TILES_RESIDUAL = (1056, 1024, 640, 256)
V7X_VMEM_LIMIT = 56 * 1024 * 1024


def _pick(n, candidates):
    for c in candidates:
        if n % c == 0:
            return c
    raise ValueError(f"no tile for {n} in {candidates}")


def _params(sem, vmem=V7X_VMEM_LIMIT):
    return pltpu.CompilerParams(dimension_semantics=sem, vmem_limit_bytes=vmem)


def _dot(a, b):
    return jnp.dot(a, b, preferred_element_type=F32)


def _dot_nt(a, b):
    return lax.dot_general(a, b, (((1,), (1,)), ((), ())), preferred_element_type=F32)


def _small_mm_kernel(a_ref, w_ref, b_ref, o_ref, *, silu_in, kc):
    a = a_ref[0]
    if silu_in:
        a = a * jax.nn.sigmoid(a)
    a = a.astype(BF16)
    k = a.shape[1]
    acc = jnp.zeros(o_ref.shape[1:], F32)
    for k0 in range(0, k, kc):
        acc += _dot(a[:, k0:k0 + kc], w_ref[0, k0:k0 + kc, :].astype(BF16))
    o_ref[0] = acc + b_ref[0]


def _small_mm(a, w, b, *, silu_in, tn):
    depth, k, n = w.shape
    shared = a.shape[0] == 1
    kern = functools.partial(_small_mm_kernel, silu_in=silu_in, kc=min(k, 512))
    return pl.pallas_call(
        kern,
        grid=(depth, n // tn),
        in_specs=[
            pl.BlockSpec((1, 8, k), lambda l, j: (0 if shared else l, 0, 0)),
            pl.BlockSpec((1, k, tn), lambda l, j: (l, 0, j)),
            pl.BlockSpec((1, 1, tn), lambda l, j: (l, 0, j)),
        ],
        out_specs=pl.BlockSpec((1, 8, tn), lambda l, j: (l, 0, j)),
        out_shape=jax.ShapeDtypeStruct((depth, 8, n), F32),
        compiler_params=_params(("arbitrary", "arbitrary")),
        name="adaln_mm",
    )(a, w, b)


def _adaln_all(c, c_ctx, ada_down, ada_up, ada_b):
    depth, d, rank = ada_down.shape
    cvec = jnp.zeros((1, 8, d), F32).at[0, 0].set(c[0]).at[0, 1].set(c_ctx)
    t = _small_mm(cvec, ada_down, jnp.zeros((depth, 1, rank), F32), silu_in=True, tn=_pick(rank, (512, 256, 128)))
    m = _small_mm(t, ada_up, ada_b.reshape(depth, 1, 6 * d), silu_in=False, tn=_pick(6 * d, (2048, 1024, 512)))
    return m[:, :2].reshape(depth, 2, 6, d)


def _rms_mod(x, gain, shift, scale):
    ms = jnp.mean(x * x, axis=-1, keepdims=True)
    return (x * lax.rsqrt(ms + RMS_EPS) * gain) * (1.0 + scale) + shift


def _norm_mod_kernel(x_ref, g_ref, mod_ref, o_ref, *, shift_idx):
    u = _rms_mod(x_ref[...], g_ref[...], mod_ref[0, shift_idx:shift_idx + 1, :],
                 mod_ref[0, shift_idx + 1:shift_idx + 2, :])
    o_ref[...] = u.astype(o_ref.dtype)


def _norm_mod(xs, gain, mod, *, shift_idx, rows, n_lat):
    d = xs.shape[1]
    tr = ROW_TILE
    return pl.pallas_call(
        functools.partial(_norm_mod_kernel, shift_idx=shift_idx),
        grid=(rows // tr,),
        in_specs=[
            pl.BlockSpec((tr, d), lambda i: (i, 0)),
            pl.BlockSpec((1, d), lambda i: (0, 0)),
            pl.BlockSpec((1, 6, d), lambda i: (jnp.where(i * tr >= n_lat, 1, 0), 0, 0)),
        ],
        out_specs=pl.BlockSpec((tr, d), lambda i: (i, 0)),
        out_shape=jax.ShapeDtypeStruct((rows, d), BF16),
        compiler_params=_params(("arbitrary",)),
        name="norm_mod",
    )(xs, gain.reshape(1, d), mod)


def _mm(a, w, *, rows, tm, tn, out_cols, out_dtype, col_maps, epilogue, extras=(), w_outer, name,
        layer=None, to_bf16=None):
    k = a.shape[1]
    n_i, n_j = rows // tm, out_cols // tn
    cast = w.dtype != BF16
    assert w_outer or not cast
    n_parts = len(col_maps)
    kc = _pick(k, (512, 256, 128))

    if w_outer:
        grid = (n_j, n_i)
        ij = lambda g0, g1: (g1, g0)
    else:
        grid = (n_i, n_j)
        ij = lambda g0, g1: (g0, g1)

    if w.ndim == 3:
        w_block, w_index = (None, k, tn), lambda col: (layer, 0, col)
    else:
        w_block, w_index = (k, tn), lambda col: (0, col)
    in_specs = [pl.BlockSpec((tm, k), lambda g0, g1: (ij(g0, g1)[0], 0))]
    for cm in col_maps:
        in_specs.append(pl.BlockSpec(w_block, lambda g0, g1, cm=cm: w_index(cm(ij(g0, g1)[1]))))
    for _, shape, fn in extras:
        in_specs.append(pl.BlockSpec(shape, lambda g0, g1, fn=fn: fn(*ij(g0, g1))))
    n_ex = len(extras)
    out_specs = [pl.BlockSpec((tm, tn), lambda g0, g1: ij(g0, g1))]
    out_shape = [jax.ShapeDtypeStruct((rows, out_cols), out_dtype)]
    n_side = 0 if to_bf16 is None else 1
    if n_side:
        _, side_r, side_c = to_bf16.shape
        side_steps = max(s for s in (1, 2, 4, 8, 16) if s <= grid[1] and side_r % (grid[0] * s * 16) == 0)
        slab = side_r // (grid[0] * side_steps)
        side_index = lambda g0, g1: g0 * side_steps + jnp.minimum(g1, side_steps - 1)
        in_specs.append(pl.BlockSpec((None, slab, side_c), lambda g0, g1: (layer, side_index(g0, g1), 0)))
        out_specs.append(pl.BlockSpec((slab, side_c), lambda g0, g1: (side_index(g0, g1), 0)))
        out_shape.append(jax.ShapeDtypeStruct((side_r, side_c), BF16))

    def body(*refs):
        a_ref = refs[0]
        w_refs = refs[1:1 + n_parts]
        ex_refs = refs[1 + n_parts:1 + n_parts + n_ex]
        n_in = 1 + n_parts + n_ex + n_side
        o_ref = refs[n_in]
        scratch = refs[n_in + 1 + n_side:]
        i, j = ij(pl.program_id(0), pl.program_id(1))
        if n_side:
            @pl.when(pl.program_id(1) < side_steps)
            def _():
                refs[n_in + 1][...] = refs[n_in - 1][...].astype(BF16)
        if cast:
            @pl.when(i == 0)
            def _():
                def cp(c, carry):
                    sl = pl.ds(pl.multiple_of(c * kc, kc), kc)
                    for p in range(n_parts):
                        scratch[p][sl, :] = w_refs[p][sl, :].astype(BF16)
                    return carry
                lax.fori_loop(0, k // kc, cp, 0)
            ws = scratch
        else:
            ws = w_refs
        av = a_ref[...]
        accs = [_dot(av, ws[p][...]) for p in range(n_parts)]
        epilogue(accs, i, j, ex_refs, o_ref)

    outs = pl.pallas_call(
        body,
        grid=grid,
        in_specs=in_specs,
        out_specs=out_specs,
        out_shape=out_shape,
        scratch_shapes=[pltpu.VMEM((k, tn), BF16) for _ in range(n_parts)] if cast else [],
        compiler_params=_params(("arbitrary", "arbitrary")),
        name=name,
    )(a, *([w] * n_parts), *[e[0] for e in extras], *([to_bf16] if n_side else []))
    return outs if n_side else outs[0]


def _row_gate(mod_ref, idx, i, tm, n_lat):
    rows = i * tm + lax.broadcasted_iota(jnp.int32, (tm, 1), 0)
    return jnp.where(rows < n_lat, mod_ref[0, idx:idx + 1, :], mod_ref[1, idx:idx + 1, :])


def _epi_swiglu(accs, i, j, ex, o_ref):
    a, g = accs
    o_ref[...] = (a * jax.nn.sigmoid(a) * g).astype(o_ref.dtype)


def _epi_residual(accs, i, j, ex, o_ref, *, gate_idx, tm, n_lat):
    res_ref, mod_ref = ex
    o_ref[...] = res_ref[...] + _row_gate(mod_ref, gate_idx, i, tm, n_lat) * accs[0]


def _epi_glu_residual(accs, i, j, ex, o_ref, *, gate_idx, tm, n_lat):
    res_ref, mod_ref = ex
    a, g = accs
    o_ref[...] = res_ref[...] + _row_gate(mod_ref, gate_idx, i, tm, n_lat) * (a * jax.nn.sigmoid(g))


def _residual_extras(xs, mod, tm, tn):
    return ((xs, (tm, tn), lambda i, j: (i, j)), (mod, (2, 6, tn), lambda i, j: (0, 0, j)))


def _mm_residual(a, w, xs, mod, *, gate_idx, rows, n_lat, w_outer, name):
    n = w.shape[1]
    tm = _pick(rows, MM_ROW_TILES_RESIDUAL if w_outer else (768, 512, 640, 256))
    tn = _pick(n, (512, 256) if w_outer else (256, 128))
    return _mm(a, w, rows=rows, tm=tm, tn=tn, out_cols=n, out_dtype=F32, col_maps=(lambda j: j,),
               epilogue=functools.partial(_epi_residual, gate_idx=gate_idx, tm=tm, n_lat=n_lat),
               extras=_residual_extras(xs, mod, tm, tn), w_outer=w_outer, name=name)


def _pool_kernel(xp_ref, x_ref, xn_ref, c_ref, g_ref, gf_ref, mod_ref, pw_ref, ps_ref, o_ref, h_ref, *, tr, n_lat, n_tot):
    i = pl.program_id(0)
    halo = POOL_HALO
    gain = g_ref[...]
    shift, scale, gate = mod_ref[0, 0:1, :], mod_ref[0, 1:2, :], mod_ref[0, 2:3, :]
    is_lat = i * tr < n_lat
    lat_rows = lax.broadcasted_iota(jnp.int32, (tr, 1), 0) * 0 + jnp.where(is_lat, 1, 0) > 0
    x = jnp.where(lat_rows, x_ref[...], c_ref[...])
    ue = jnp.concatenate([_rms_mod(xp_ref[...], gain, shift, scale), _rms_mod(x, gain, shift, scale),
                          _rms_mod(xn_ref[...], gain, shift, scale)], axis=0)
    seq_lo = jnp.where(is_lat, 0, n_lat)
    seq_hi = jnp.where(is_lat, n_lat, n_tot)
    t_g = i * tr + lax.broadcasted_iota(jnp.int32, (tr, 1), 0)
    s_g = i * tr - halo + lax.broadcasted_iota(jnp.int32, (1, tr + 2 * halo), 1)
    s_ok = (s_g >= seq_lo) & (s_g < seq_hi)
    pg = ue.shape[1] // len(POOL_WINDOWS)
    for gi, w in enumerate(POOL_WINDOWS):
        cols = slice(gi * pg, (gi + 1) * pg)
        ug = ue[:, cols]
        band = (s_g >= t_g - w // 2) & (s_g < t_g + w // 2) & s_ok
        ssum = _dot(jnp.where(band, 1.0, 0.0).astype(BF16), ug.astype(BF16))
        cnt = jnp.minimum(t_g + w // 2, seq_hi) - jnp.maximum(t_g - w // 2, seq_lo)
        p = ssum / cnt.astype(F32) - ug[halo:halo + tr]
        y = _dot(p.astype(BF16), pw_ref[gi]) * ps_ref[:, cols]
        o_ref[:, cols] = x[:, cols] + gate[:, cols] * y
    h_ref[...] = _rms_mod(o_ref[...], gf_ref[...], mod_ref[0, 3:4, :], mod_ref[0, 4:5, :]).astype(h_ref.dtype)


def _pool_layer(lat, ctx_src, ctx_block, gain, gain_ffn, mod, pool_w, pool_scale, *, rows, n_lat, n_tot):
    d = lat.shape[1]
    tr, halo = ROW_TILE, POOL_HALO
    hb = tr // halo
    last_hb = n_lat // halo - 1
    vec = pl.BlockSpec((1, d), lambda i: (0, 0))
    tile = pl.BlockSpec((tr, d), lambda i: (i, 0))
    return pl.pallas_call(
        functools.partial(_pool_kernel, tr=tr, n_lat=n_lat, n_tot=n_tot),
        grid=(rows // tr,),
        in_specs=[
            pl.BlockSpec((halo, d), lambda i: (jnp.clip(i * hb - 1, 0, last_hb), 0)),
            pl.BlockSpec((tr, d), lambda i: (jnp.minimum(i, n_lat // tr - 1), 0)),
            pl.BlockSpec((halo, d), lambda i: (jnp.minimum((i + 1) * hb, last_hb), 0)),
            pl.BlockSpec((tr, d), lambda i: (ctx_block, 0)),
            vec, vec,
            pl.BlockSpec((1, 6, d), lambda i: (jnp.where(i * tr >= n_lat, 1, 0), 0, 0)),
            pl.BlockSpec(pool_w.shape, lambda i: (0, 0, 0)),
            vec,
        ],
        out_specs=[tile, tile],
        out_shape=[jax.ShapeDtypeStruct((rows, d), F32), jax.ShapeDtypeStruct((rows, d), BF16)],
        compiler_params=_params(("arbitrary",)),
        name="pool_mixer",
    )(lat, lat, lat, ctx_src, gain.reshape(1, d), gain_ffn.reshape(1, d), mod, pool_w.astype(BF16),
      pool_scale.reshape(1, d))


def _s5_prep_kernel(lre_ref, lim_ref, ls_ref, bre_ref, bim_ref, are_ref, aim_ref, bbre_ref, bbim_ref):
    lam_re, lam_im = lre_ref[0], lim_ref[0]
    dt = jnp.exp(ls_ref[0])
    z_re, z_im = lam_re * dt, lam_im * dt
    mag = jnp.exp(z_re)
    a_re, a_im = mag * jnp.cos(z_im), mag * jnp.sin(z_im)
    den = lam_re * lam_re + lam_im * lam_im
    k_re = ((a_re - 1.0) * lam_re + a_im * lam_im) / den
    k_im = (a_im * lam_re - (a_re - 1.0) * lam_im) / den
    are_ref[0] = a_re
    aim_ref[0] = a_im
    b_re, b_im = bre_ref[0], bim_ref[0]
    bbre_ref[0] = k_re[:, None, :] * b_re - k_im[:, None, :] * b_im
    bbim_ref[0] = k_re[:, None, :] * b_im + k_im[:, None, :] * b_re


def _s5_prep(lam_re, lam_im, log_step, b_re, b_im):
    _, g, p, h = b_re.shape
    spec2 = pl.BlockSpec((1, g, p), lambda d: (d, 0, 0))
    spec3 = pl.BlockSpec((1, g, h, p), lambda d: (d, 0, 0, 0))
    return pl.pallas_call(
        _s5_prep_kernel,
        grid=(2,),
        in_specs=[spec2, spec2, pl.BlockSpec((1, g, 1), lambda d: (d, 0, 0)), spec3, spec3],
        out_specs=[spec2, spec2, spec3, spec3],
        out_shape=[jax.ShapeDtypeStruct((2, g, p), F32)] * 2 + [jax.ShapeDtypeStruct((2, g, h, p), F32)] * 2,
        compiler_params=_params(("arbitrary",)),
        name="s5_prep",
    )(lam_re, lam_im, log_step[..., None], jnp.swapaxes(b_re, 2, 3), jnp.swapaxes(b_im, 2, 3))


def _s5_pack(a_re, a_im, bb_re, bb_im, c_re, c_im):
    _, g, h, p = bb_re.shape
    r = g // 2
    eye2 = jnp.eye(2, dtype=F32)
    eye4 = jnp.eye(4, dtype=F32)
    slot = eye4[jnp.arange(r) % 4]

    def win_part(bb):
        t = bb.reshape(2, r, 2, h, p)
        t = t[:, :, :, :, None, :] * eye2[None, None, :, None, :, None]
        t = t.reshape(2, r, 2 * h, 2 * p)
        t = slot[None, :, :, None, None] * t[:, :, None, :, :]
        return t.reshape(2, r, 4 * 2 * h, 2 * p)

    def c_part(c):
        t = jnp.swapaxes(c, 2, 3).reshape(2, r, 2, p, h)
        t = t[:, :, :, :, None, :] * eye2[None, None, :, None, :, None]
        t = t.reshape(2, r, 2 * p, 2 * h)
        t = t[:, :, :, None, :] * slot[None, :, None, :, None]
        return t.reshape(2, r, 2 * p, 4 * 2 * h)

    win = jnp.concatenate([win_part(bb_re), win_part(bb_im)], axis=-1).astype(BF16)
    cmat = jnp.concatenate([c_part(c_re), c_part(-c_im)], axis=2).astype(BF16)
    return win, cmat, a_re.reshape(2, r, 2 * p), a_im.reshape(2, r, 2 * p)


def _s5_scan_kernel(u_ref, win_ref, cmat_ref, are_ref, aim_ref, y_ref, xre, xim, hre, him, *, nsub, rb):
    g0, c = pl.program_id(0), pl.program_id(1)
    rev = g0 // nsub == 1
    tc, pitch = S5_CHUNK, rb + S5_ROW_PAD
    block_rows = lambda r: pl.ds(r, tc, stride=pitch)

    @pl.when(c == 0)
    def _():
        hre[...] = jnp.zeros_like(hre)
        him[...] = jnp.zeros_like(him)

    for q in range(rb // 4):
        ut = u_ref[:, q * LANES:(q + 1) * LANES]
        for s in range(4):
            r = q * 4 + s
            x = _dot(ut, win_ref[0, r])
            xre[block_rows(r), :] = x[:, :LANES]
            xim[block_rows(r), :] = x[:, LANES:]

    a_re, a_im = are_ref[0], aim_ref[0]

    def step(t, carry):
        h_re, h_im = carry
        tt = jnp.where(rev, tc - 1 - t, t)
        rows = pl.ds(pl.multiple_of(tt * pitch, 8), rb)
        n_re = a_re * h_re - a_im * h_im + xre[rows, :]
        n_im = a_re * h_im + a_im * h_re + xim[rows, :]
        xre[rows, :] = n_re
        xim[rows, :] = n_im
        return n_re, n_im

    h_re, h_im = lax.fori_loop(0, tc, step, (hre[...], him[...]), unroll=8)
    hre[...] = h_re
    him[...] = h_im

    for q in range(rb // 4):
        acc = jnp.zeros((tc, LANES), F32)
        for s in range(4):
            r = q * 4 + s
            acc += (_dot(xre[block_rows(r), :].astype(BF16), cmat_ref[0, r, :LANES, :])
                    + _dot(xim[block_rows(r), :].astype(BF16), cmat_ref[0, r, LANES:, :]))
        y_ref[0, :, q * LANES:(q + 1) * LANES] = acc.astype(y_ref.dtype)


def _s5_scan(u, win, cmat, a_re, a_im, *, n_lat):
    n_tot, d = u.shape
    r_all = win.shape[1]
    rb = min(S5_BLOCKS, r_all)
    nsub = r_all // rb
    tc = S5_CHUNK
    n_c, n_lc = n_tot // tc, n_lat // tc
    cols = rb * 2 * S5_GROUP

    def split(w):
        return w.reshape((2 * nsub, rb) + w.shape[2:])

    def chunk(g0, c):
        return jnp.where(g0 // nsub == 0, (c + n_lc) % n_c, n_c - 1 - c)

    wspec = lambda shape: pl.BlockSpec((1,) + shape, lambda g0, c: (g0,) + (0,) * len(shape))
    return pl.pallas_call(
        functools.partial(_s5_scan_kernel, nsub=nsub, rb=rb),
        grid=(2 * nsub, n_c),
        in_specs=[
            pl.BlockSpec((tc, cols), lambda g0, c: (chunk(g0, c), g0 % nsub)),
            wspec((rb, LANES, 2 * LANES)), wspec((rb, 2 * LANES, LANES)), wspec((rb, LANES)), wspec((rb, LANES)),
        ],
        out_specs=pl.BlockSpec((1, tc, cols), lambda g0, c: (g0 // nsub, chunk(g0, c), g0 % nsub)),
        out_shape=jax.ShapeDtypeStruct((2, n_tot, d), BF16),
        scratch_shapes=[pltpu.VMEM((tc * (rb + S5_ROW_PAD), LANES), F32)] * 2 + [pltpu.VMEM((rb, LANES), F32)] * 2,
        compiler_params=_params(("arbitrary", "arbitrary")),
        name="s5_scan",
    )(u, split(win), split(cmat), split(a_re), split(a_im))


def _s5_gelu_kernel(y_ref, u_ref, d_ref, o_ref):
    v = y_ref[0].astype(F32) + y_ref[1].astype(F32) + d_ref[...] * u_ref[...].astype(F32)
    o_ref[...] = jax.nn.gelu(v, approximate=True).astype(o_ref.dtype)


def _s5_gelu(y, u, dvec, *, rows):
    d = u.shape[1]
    tr = ROW_TILE
    return pl.pallas_call(
        _s5_gelu_kernel,
        grid=(rows // tr,),
        in_specs=[pl.BlockSpec((2, tr, d), lambda i: (0, i, 0)), pl.BlockSpec((tr, d), lambda i: (i, 0)),
                  pl.BlockSpec((1, d), lambda i: (0, 0))],
        out_specs=pl.BlockSpec((tr, d), lambda i: (i, 0)),
        out_shape=jax.ShapeDtypeStruct((rows, d), BF16),
        compiler_params=_params(("arbitrary",)),
        name="s5_gelu",
    )(y, u, dvec.reshape(1, d))


def _rope_tables(n_lat, n_ctx):
    quarter = HEAD_DIM // 4
    inv = jnp.power(ROPE_BASE, -jnp.arange(quarter, dtype=F32) / quarter)
    t = jnp.arange(n_lat)
    ang_r = (t // GRID_W).astype(F32)[:, None] * inv
    ang_c = (t % GRID_W).astype(F32)[:, None] * inv
    ang = jnp.concatenate([ang_r, ang_r, ang_c, ang_c], axis=-1)
    sign = jnp.where((jnp.arange(HEAD_DIM) % (2 * quarter)) < quarter, -1.0, 1.0)
    cos = jnp.concatenate([jnp.cos(ang), jnp.ones((n_ctx, HEAD_DIM), F32)], axis=0)
    sin = jnp.concatenate([jnp.sin(ang) * sign, jnp.zeros((n_ctx, HEAD_DIM), F32)], axis=0)
    return cos, sin


def _epi_qkv(accs, i, j, ex, o_ref, *, n_norm_tiles, rope, tn):
    gain_ref = ex[0]
    heads_per_acc = tn // len(accs) // HEAD_DIM

    @pl.when(j < n_norm_tiles)
    def _():
        quarter = HEAD_DIM // 4
        lane = lax.broadcasted_iota(jnp.int32, (1, HEAD_DIM), 1)
        first = (lane % (2 * quarter)) < quarter
        for h in range(tn // HEAD_DIM):
            cols = slice(h * HEAD_DIM, (h + 1) * HEAD_DIM)
            hl = h % heads_per_acc
            x = accs[h // heads_per_acc][:, hl * HEAD_DIM:(hl + 1) * HEAD_DIM]
            ms = jnp.mean(x * x, axis=-1, keepdims=True)
            xn = x * lax.rsqrt(ms + RMS_EPS) * gain_ref[:, cols]
            if rope:
                cos_ref, sin_ref = ex[1], ex[2]
                partner = jnp.where(first, pltpu.roll(xn, HEAD_DIM - quarter, 1), pltpu.roll(xn, quarter, 1))
                xn = xn * cos_ref[...] + partner * sin_ref[...]
            o_ref[:, cols] = xn.astype(o_ref.dtype)

    @pl.when(j >= n_norm_tiles)
    def _():
        sub = tn // len(accs)
        for s, acc in enumerate(accs):
            o_ref[:, s * sub:(s + 1) * sub] = acc.astype(o_ref.dtype)


def _qkv_proj(u, w_qkv, q_gain, k_gain, *, n_q, n_kv, rope_tables, rows):
    n = w_qkv.shape[1]
    tm = _pick(rows, MM_ROW_TILES)
    tn = _pick(n_kv * HEAD_DIM, (512, 256, 128))
    gain = jnp.concatenate([jnp.tile(q_gain * HEAD_DIM ** -0.5, n_q), jnp.tile(k_gain, n_kv),
                            jnp.ones((n_kv * HEAD_DIM,), F32)]).reshape(1, n)
    extras = [(gain, (1, tn), lambda i, j: (0, j))]
    if rope_tables is not None:
        extras += [(t, (tm, HEAD_DIM), lambda i, j: (i, 0)) for t in rope_tables]
    epi = functools.partial(_epi_qkv, n_norm_tiles=(n_q + n_kv) * HEAD_DIM // tn, rope=rope_tables is not None, tn=tn)
    return _mm(u, w_qkv, rows=rows, tm=tm, tn=tn, out_cols=n, out_dtype=BF16, col_maps=(lambda j: j,),
               epilogue=epi, extras=tuple(extras), w_outer=True, name="qkv_proj")


def _softmax_pv(tiles, values, extra_logit=None):
    rows = tiles[0].shape[0]
    mx = functools.reduce(jnp.maximum, tiles)
    if extra_logit is not None:
        mx = jnp.maximum(mx, extra_logit)
    m = jnp.broadcast_to(jnp.max(mx, axis=-1, keepdims=True), (rows, LANES))
    p = jnp.concatenate([jnp.exp(t - m).astype(BF16) for t in tiles], axis=1)
    ones = jnp.ones((LANES, LANES), BF16)
    v_aug = jnp.concatenate([jnp.concatenate([v, ones], axis=1) for v in values], axis=0)
    acc = _dot(p, v_aug)
    den = acc[:, LANES:]
    if extra_logit is not None:
        den = den + jnp.exp(extra_logit - m)
    return acc[:, :LANES] / den


def _swa_kernel(sink_ref, q_ref, kp_ref, kc_ref, kn_ref, vp_ref, vc_ref, vn_ref, kx_ref, vx_ref, o_ref, *, n_lat_blocks):
    h, b = pl.program_id(0), pl.program_id(1)
    blk = SWA_WINDOW
    is_lat = b < n_lat_blocks
    qi = lax.broadcasted_iota(jnp.int32, (SWA_GROUP * blk, 1), 0) & (blk - 1)
    kj = lax.broadcasted_iota(jnp.int32, (1, blk), 1)
    off = lambda ok: jnp.where(ok, 0, blk + 1)
    m_prev = kj >= qi + off(is_lat & (b > 0))
    m_cur = kj >= qi * 0 + off(is_lat)
    m_next = kj <= qi - off(is_lat & (b < n_lat_blocks - 1))
    n_ctx_tiles = kx_ref.shape[0] // LANES
    hp = kp_ref.shape[1] // HEAD_DIM
    for hh in range(hp):
        kcols = slice(hh * HEAD_DIM, (hh + 1) * HEAD_DIM)
        qh = lambda g: (hh * SWA_GROUP + g) * HEAD_DIM
        q = jnp.concatenate([q_ref[:, qh(g):qh(g) + HEAD_DIM] for g in range(SWA_GROUP)], axis=0)
        sink = jnp.concatenate([jnp.full((blk, LANES), sink_ref[(h * hp + hh) * SWA_GROUP + g], F32)
                                for g in range(SWA_GROUP)], axis=0)
        s_ctx = _dot_nt(q, kx_ref[:, kcols])
        tiles = [jnp.where(m_prev, _dot_nt(q, kp_ref[:, kcols]), NEG_INF),
                 jnp.where(m_cur, _dot_nt(q, kc_ref[:, kcols]), NEG_INF),
                 jnp.where(m_next, _dot_nt(q, kn_ref[:, kcols]), NEG_INF)]
        tiles += [s_ctx[:, t * LANES:(t + 1) * LANES] for t in range(n_ctx_tiles)]
        vals = [vp_ref[:, kcols], vc_ref[:, kcols], vn_ref[:, kcols]]
        vals += [vx_ref[t * LANES:(t + 1) * LANES, kcols] for t in range(n_ctx_tiles)]
        o = _softmax_pv(tiles, vals, sink).astype(o_ref.dtype)
        for g in range(SWA_GROUP):
            o_ref[:, qh(g):qh(g) + HEAD_DIM] = o[g * blk:(g + 1) * blk]


def _swa_attention(qkv, sinks, *, n_q, n_kv, rows, n_lat):
    blk = SWA_WINDOW
    nlb = n_lat // blk
    ctx_rows = (qkv.shape[0] - n_lat)
    hp = min(SWA_KV_PER_STEP, n_kv)
    assert n_kv % hp == 0 and n_q % hp == 0
    kcol = lambda h: n_q // hp + h
    vcol = lambda h: (n_q + n_kv) // hp + h
    prev = lambda b: jnp.clip(b - 1, 0, nlb - 1)
    cur = lambda b: jnp.minimum(b, nlb - 1)
    nxt = lambda b: jnp.clip(b + 1, 0, nlb - 1)
    kv = lambda rowf, colf: pl.BlockSpec((blk, hp * HEAD_DIM), lambda h, b: (rowf(b), colf(h)))
    ctx = lambda colf: pl.BlockSpec((ctx_rows, hp * HEAD_DIM), lambda h, b: (n_lat // ctx_rows, colf(h)))
    return pl.pallas_call(
        functools.partial(_swa_kernel, n_lat_blocks=nlb),
        grid=(n_kv // hp, rows // blk),
        in_specs=[
            pl.BlockSpec(memory_space=pltpu.SMEM),
            pl.BlockSpec((blk, hp * SWA_GROUP * HEAD_DIM), lambda h, b: (b, h)),
            kv(prev, kcol), kv(cur, kcol), kv(nxt, kcol), kv(prev, vcol), kv(cur, vcol), kv(nxt, vcol),
            ctx(kcol), ctx(vcol),
        ],
        out_specs=pl.BlockSpec((blk, hp * SWA_GROUP * HEAD_DIM), lambda h, b: (b, h)),
        out_shape=jax.ShapeDtypeStruct((rows, n_q * HEAD_DIM), BF16),
        compiler_params=_params(("arbitrary", "arbitrary")),
        name="swa_attention",
    )(sinks, *([qkv] * 9))


def _na_bias_kernel(rpb_ref, o_ref):
    h = pl.program_id(0)
    n_ri, n_ci = 2 * NA_WIN_ROWS - 1, 2 * NA_WIN_COLS - 1
    cq = lax.broadcasted_iota(jnp.int32, (GRID_W, 2 * GRID_W), 0)
    ck = lax.broadcasted_iota(jnp.int32, (GRID_W, 2 * GRID_W), 1)
    second = ck >= GRID_W
    ck = jnp.where(second, ck - GRID_W, ck)
    c0 = jnp.clip(cq - NA_WIN_COLS // 2, 0, GRID_W - NA_WIN_COLS)
    in_win = (ck >= c0) & (ck < c0 + NA_WIN_COLS)
    cidx = jnp.clip(ck - cq + NA_WIN_COLS - 1, 0, n_ci - 1)
    neg = jnp.full((GRID_W, 2 * GRID_W), NEG_INF, F32)
    per_offset = []
    for ri in range(n_ri):
        t = neg
        for dci in range(n_ci):
            t = jnp.where(cidx == dci, rpb_ref[(h * n_ri + ri) * n_ci + dci], t)
        per_offset.append(jnp.where(in_win, t, NEG_INF))
    for k in range(n_ri + 1):
        o_ref[0, k] = jnp.where(second, per_offset[k] if k < n_ri else neg, per_offset[k - 1] if k >= 1 else neg)


def _na_bias(rpb):
    n_heads = rpb.shape[0]
    n_tiles = 2 * NA_WIN_ROWS
    return pl.pallas_call(
        _na_bias_kernel,
        grid=(n_heads,),
        in_specs=[pl.BlockSpec(memory_space=pltpu.SMEM)],
        out_specs=pl.BlockSpec((1, n_tiles, GRID_W, 2 * GRID_W), lambda h: (h, 0, 0, 0)),
        out_shape=jax.ShapeDtypeStruct((n_heads, n_tiles, GRID_W, 2 * GRID_W), F32),
        compiler_params=_params(("arbitrary",)),
        name="na_bias",
    )(rpb.reshape(-1))


def _na_kernel(q_ref, kp_ref, kc_ref, kn_ref, vp_ref, vc_ref, vn_ref, kx_ref, vx_ref, bias_ref, o_ref, *, n_grid_rows):
    b = pl.program_id(1)
    qr, w = NA_QROWS, GRID_W
    is_lat = b * qr < n_grid_rows
    lane = lax.broadcasted_iota(jnp.int32, (1, 2 * w), 1)
    n_kt = qr // 2
    windows = {}
    for kb in range(3):
        for dq in range(qr):
            r = b * qr + dq
            r0 = jnp.clip(r - NA_WIN_ROWS // 2, 0, n_grid_rows - NA_WIN_ROWS)
            for jt in range(n_kt):
                kr = b * qr + (kb - 1) * qr + 2 * jt
                ok0 = (kr >= r0) & (kr < r0 + NA_WIN_ROWS) & is_lat
                ok1 = (kr + 1 >= r0) & (kr + 1 < r0 + NA_WIN_ROWS) & is_lat
                lo, hi = jnp.where(ok0, 0, w), jnp.where(ok1, 2 * w, w)
                windows[kb, dq, jt] = (lane >= lo) & (lane < hi)
    for hh in range(q_ref.shape[1] // HEAD_DIM):
        cols = slice(hh * HEAD_DIM, (hh + 1) * HEAD_DIM)
        q = q_ref[:, cols]
        tiles, vals = [], []
        for kb, (k_ref, v_ref) in enumerate(((kp_ref, vp_ref), (kc_ref, vc_ref), (kn_ref, vn_ref))):
            s = _dot_nt(q, k_ref[:, cols])
            for jt in range(n_kt):
                parts = []
                for dq in range(qr):
                    ri0 = (kb - 1) * qr + 2 * jt - dq + NA_WIN_ROWS - 1
                    t = s[dq * w:(dq + 1) * w, jt * 2 * w:(jt + 1) * 2 * w] + bias_ref[hh, ri0 + 1]
                    parts.append(jnp.where(windows[kb, dq, jt], t, NEG_INF))
                tiles.append(jnp.concatenate(parts, axis=0))
                vals.append(v_ref[jt * LANES:(jt + 1) * LANES, cols])
        s_ctx = _dot_nt(q, kx_ref[:, cols])
        for t in range(kx_ref.shape[0] // LANES):
            tiles.append(s_ctx[:, t * LANES:(t + 1) * LANES])
            vals.append(vx_ref[t * LANES:(t + 1) * LANES, cols])
        o_ref[:, cols] = _softmax_pv(tiles, vals).astype(o_ref.dtype)


def _na_attention(qkv, bias, *, n_heads, rows, n_lat):
    blk = NA_QROWS * GRID_W
    assert blk == qkv.shape[0] - n_lat
    nlb = n_lat // blk
    hp = min(NA_HEADS_PER_STEP, n_heads)
    assert n_heads % hp == 0
    kcol = lambda h: n_heads // hp + h
    vcol = lambda h: 2 * n_heads // hp + h
    prev = lambda b: jnp.clip(b - 1, 0, nlb - 1)
    cur = lambda b: jnp.minimum(b, nlb - 1)
    nxt = lambda b: jnp.clip(b + 1, 0, nlb - 1)
    spec = lambda rowf, colf: pl.BlockSpec((blk, hp * HEAD_DIM), lambda h, b: (rowf(b), colf(h)))
    ctx = lambda b: nlb
    return pl.pallas_call(
        functools.partial(_na_kernel, n_grid_rows=n_lat // GRID_W),
        grid=(n_heads // hp, rows // blk),
        in_specs=[
            spec(lambda b: b, lambda h: h),
            spec(prev, kcol), spec(cur, kcol), spec(nxt, kcol), spec(prev, vcol), spec(cur, vcol), spec(nxt, vcol),
            spec(ctx, kcol), spec(ctx, vcol),
            pl.BlockSpec((hp,) + bias.shape[1:], lambda h, b: (h, 0, 0, 0)),
        ],
        out_specs=spec(lambda b: b, lambda h: h),
        out_shape=jax.ShapeDtypeStruct((rows, n_heads * HEAD_DIM), BF16),
        compiler_params=_params(("arbitrary", "arbitrary")),
        name="na_attention",
    )(*([qkv] * 9), bias)


def kernel(x, c, ctx, c_ctx, ada_down, ada_up, ada_b, norm_mix, norm_ffn, ffn_w_gate_up, ffn_w_down, pool_w, pool_scale, s5_lam_re, s5_lam_im, s5_log_step, s5_b_re, s5_b_im, s5_c_re, s5_c_im, s5_d, s5_w_glu, swa_w_qkv, swa_w_o, swa_q_gain, swa_k_gain, swa_sinks, na_w_qkv, na_w_o, na_q_gain, na_k_gain, na_rpb):
    bsz, n_lat, d = x.shape
    n_ctx = ctx.shape[1]
    assert bsz == 1 and n_lat % ROW_TILE == 0 and n_ctx == ROW_TILE
    n_tot = n_lat + n_ctx
    depth = ada_down.shape[0]
    d_ff = ffn_w_down.shape[1]
    n_heads = d // HEAD_DIM

    mods = _adaln_all(c, c_ctx, ada_down, ada_up, ada_b)
    xs = None

    for layer in range(depth):
        kind = layer % 4
        last = layer == depth - 1
        rows = n_lat if last else n_tot
        mod = mods[layer]
        res = dict(rows=rows, n_lat=n_lat)

        if kind == 0:
            lat, ctx_src, ctx_block = (x[0], ctx[0], 0) if xs is None else (xs, xs, n_lat // ROW_TILE)
            xs, hx = _pool_layer(lat, ctx_src, ctx_block, norm_mix[layer], norm_ffn[layer], mod, pool_w, pool_scale,
                                 n_tot=n_tot, **res)
        else:
            u = _norm_mod(xs, norm_mix[layer], mod, shift_idx=0, rows=n_tot, n_lat=n_lat)
            if kind == 1:
                a_re, a_im, bb_re, bb_im = _s5_prep(s5_lam_re, s5_lam_im, s5_log_step, s5_b_re, s5_b_im)
                y = _s5_scan(u, *_s5_pack(a_re, a_im, bb_re, bb_im, s5_c_re, s5_c_im), n_lat=n_lat)
                g = _s5_gelu(y, u, s5_d, rows=rows)
                half = s5_w_glu.shape[1] // 2
                tm = _pick(rows, MM_ROW_TILES)
                tn = _pick(half, (256, 128))
                xs = _mm(g, s5_w_glu, rows=rows, tm=tm, tn=tn, out_cols=half, out_dtype=F32,
                         col_maps=(lambda j: j, lambda j, o=half // tn: j + o),
                         epilogue=functools.partial(_epi_glu_residual, gate_idx=2, tm=tm, n_lat=n_lat),
                         extras=_residual_extras(xs, mod, tm, tn), w_outer=True, name="s5_glu")
            elif kind == 2:
                n_kv = n_heads // SWA_GROUP
                qkv = _qkv_proj(u, swa_w_qkv, swa_q_gain, swa_k_gain, n_q=n_heads, n_kv=n_kv,
                                rope_tables=_rope_tables(n_lat, n_ctx), rows=n_tot)
                o = _swa_attention(qkv, swa_sinks, n_q=n_heads, n_kv=n_kv, **res)
                xs = _mm_residual(o, swa_w_o, xs, mod, gate_idx=2, w_outer=True, name="swa_out", **res)
            else:
                qkv = _qkv_proj(u, na_w_qkv, na_q_gain, na_k_gain, n_q=n_heads, n_kv=n_heads,
                                rope_tables=None, rows=n_tot)
                o = _na_attention(qkv, _na_bias(na_rpb), n_heads=n_heads, **res)
                xs = _mm_residual(o, na_w_o, xs, mod, gate_idx=2, w_outer=True, name="na_out", **res)

            hx = _norm_mod(xs, norm_ffn[layer], mod, shift_idx=3, **res)
        tm = _pick(rows, MM_ROW_TILES)
        tn = _pick(d_ff, (256, 128))
        hh, w_down = _mm(hx, ffn_w_gate_up, layer=layer, rows=rows, tm=tm, tn=tn, out_cols=d_ff, out_dtype=BF16,
                         col_maps=(lambda j: j, lambda j, o=d_ff // tn: j + o), epilogue=_epi_swiglu,
                         w_outer=True, name="ffn_gate_up", to_bf16=ffn_w_down)
        xs = _mm_residual(hh, w_down, xs, mod, gate_idx=5, w_outer=False, name="ffn_down", **res)

    return xs[:n_lat][None]
```

```python
import functools

import jax
import jax.numpy as jnp
from jax import lax
from jax.experimental import pallas as pl
from jax.experimental.pallas import tpu as pltpu

F32 = jnp.float32
BF16 = jnp.bfloat16

RMS_EPS = 1e-6
NEG_INF = -1e30
HEAD_DIM = 128
GRID_W = 64
ROPE_BASE = 10000.0
POOL_WINDOWS = (2, 4, 8, 16)
POOL_HALO = 8
S5_GROUP = 16
S5_CONV_LEN = 16
S5_CONV_GROUPS = 16
S5_CHUNK = 256
S5_BLOCKS = 64
S5_ROW_PAD = 8
SWA_WINDOW = 128
SWA_GROUP = 4
NA_WIN_ROWS = 8
NA_WIN_COLS = 16
NA_QROWS = 4
SWA_KV_PER_STEP = 8
NA_HEADS_PER_STEP = 8

LANES = 128
ROW_TILE = 256
MM_ROW_TILES = (1408, 1024, 640, 256)
MM_ROW_TILES_RESIDUAL = (1056, 1024, 640, 256)
V7X_VMEM_LIMIT = 56 * 1024 * 1024


def _pick(n, candidates):
    for c in candidates:
        if n % c == 0:
            return c
    raise ValueError(f"no tile for {n} in {candidates}")


def _params(sem, vmem=V7X_VMEM_LIMIT):
    return pltpu.CompilerParams(dimension_semantics=sem, vmem_limit_bytes=vmem)


def _dot(a, b):
    return jnp.dot(a, b, preferred_element_type=F32)


def _dot_nt(a, b):
    return lax.dot_general(a, b, (((1,), (1,)), ((), ())), preferred_element_type=F32)


def _small_mm_kernel(a_ref, w_ref, b_ref, o_ref, *, silu_in, kc):
    a = a_ref[0]
    if silu_in:
        a = a * jax.nn.sigmoid(a)
    a = a.astype(BF16)
    k = a.shape[1]
    acc = jnp.zeros(o_ref.shape[1:], F32)
    for k0 in range(0, k, kc):
        acc += _dot(a[:, k0:k0 + kc], w_ref[0, k0:k0 + kc, :].astype(BF16))
    o_ref[0] = acc + b_ref[0]


def _small_mm(a, w, b, *, silu_in, tn):
    depth, k, n = w.shape
    shared = a.shape[0] == 1
    kern = functools.partial(_small_mm_kernel, silu_in=silu_in, kc=min(k, 512))
    return pl.pallas_call(
        kern,
        grid=(depth, n // tn),
        in_specs=[
            pl.BlockSpec((1, 8, k), lambda l, j: (0 if shared else l, 0, 0)),
            pl.BlockSpec((1, k, tn), lambda l, j: (l, 0, j)),
            pl.BlockSpec((1, 1, tn), lambda l, j: (l, 0, j)),
        ],
        out_specs=pl.BlockSpec((1, 8, tn), lambda l, j: (l, 0, j)),
        out_shape=jax.ShapeDtypeStruct((depth, 8, n), F32),
        compiler_params=_params(("arbitrary", "arbitrary")),
        name="adaln_mm",
    )(a, w, b)


def _adaln_all(c, c_ctx, ada_down, ada_up, ada_b):
    depth, d, rank = ada_down.shape
    cvec = jnp.zeros((1, 8, d), F32).at[0, 0].set(c[0]).at[0, 1].set(c_ctx)
    t = _small_mm(cvec, ada_down, jnp.zeros((depth, 1, rank), F32), silu_in=True, tn=_pick(rank, (512, 256, 128)))
    m = _small_mm(t, ada_up, ada_b.reshape(depth, 1, 6 * d), silu_in=False, tn=_pick(6 * d, (2048, 1024, 512)))
    return m[:, :2].reshape(depth, 2, 6, d)


def _rms_mod(x, gain, shift, scale):
    ms = jnp.mean(x * x, axis=-1, keepdims=True)
    return (x * lax.rsqrt(ms + RMS_EPS) * gain) * (1.0 + scale) + shift


def _norm_mod_kernel(x_ref, g_ref, mod_ref, o_ref, *, shift_idx):
    u = _rms_mod(x_ref[...], g_ref[...], mod_ref[0, shift_idx:shift_idx + 1, :],
                 mod_ref[0, shift_idx + 1:shift_idx + 2, :])
    o_ref[...] = u.astype(o_ref.dtype)


def _norm_mod(xs, gain, mod, *, shift_idx, rows, n_lat):
    d = xs.shape[1]
    tr = ROW_TILE
    return pl.pallas_call(
        functools.partial(_norm_mod_kernel, shift_idx=shift_idx),
        grid=(rows // tr,),
        in_specs=[
            pl.BlockSpec((tr, d), lambda i: (i, 0)),
            pl.BlockSpec((1, d), lambda i: (0, 0)),
            pl.BlockSpec((1, 6, d), lambda i: (jnp.where(i * tr >= n_lat, 1, 0), 0, 0)),
        ],
        out_specs=pl.BlockSpec((tr, d), lambda i: (i, 0)),
        out_shape=jax.ShapeDtypeStruct((rows, d), BF16),
        compiler_params=_params(("arbitrary",)),
        name="norm_mod",
    )(xs, gain.reshape(1, d), mod)


def _mm(a, w, *, rows, tm, tn, out_cols, out_dtype, col_maps, epilogue, extras=(), w_outer, name,
        layer=None, to_bf16=None):
    k = a.shape[1]
    n_i, n_j = rows // tm, out_cols // tn
    cast = w.dtype != BF16
    assert w_outer or not cast
    n_parts = len(col_maps)
    kc = _pick(k, (512, 256, 128))

    if w_outer:
        grid = (n_j, n_i)
        ij = lambda g0, g1: (g1, g0)
    else:
        grid = (n_i, n_j)
        ij = lambda g0, g1: (g0, g1)

    if w.ndim == 3:
        w_block, w_index = (None, k, tn), lambda col: (layer, 0, col)
    else:
        w_block, w_index = (k, tn), lambda col: (0, col)
    in_specs = [pl.BlockSpec((tm, k), lambda g0, g1: (ij(g0, g1)[0], 0))]
    for cm in col_maps:
        in_specs.append(pl.BlockSpec(w_block, lambda g0, g1, cm=cm: w_index(cm(ij(g0, g1)[1]))))
    for _, shape, fn in extras:
        in_specs.append(pl.BlockSpec(shape, lambda g0, g1, fn=fn: fn(*ij(g0, g1))))
    n_ex = len(extras)
    out_specs = [pl.BlockSpec((tm, tn), lambda g0, g1: ij(g0, g1))]
    out_shape = [jax.ShapeDtypeStruct((rows, out_cols), out_dtype)]
    n_side = 0 if to_bf16 is None else 1
    if n_side:
        _, side_r, side_c = to_bf16.shape
        side_steps = max(s for s in (1, 2, 4, 8, 16) if s <= grid[1] and side_r % (grid[0] * s * 16) == 0)
        slab = side_r // (grid[0] * side_steps)
        side_index = lambda g0, g1: g0 * side_steps + jnp.minimum(g1, side_steps - 1)
        in_specs.append(pl.BlockSpec((None, slab, side_c), lambda g0, g1: (layer, side_index(g0, g1), 0)))
        out_specs.append(pl.BlockSpec((slab, side_c), lambda g0, g1: (side_index(g0, g1), 0)))
        out_shape.append(jax.ShapeDtypeStruct((side_r, side_c), BF16))

    def body(*refs):
        a_ref = refs[0]
        w_refs = refs[1:1 + n_parts]
        ex_refs = refs[1 + n_parts:1 + n_parts + n_ex]
        n_in = 1 + n_parts + n_ex + n_side
        o_ref = refs[n_in]
        scratch = refs[n_in + 1 + n_side:]
        i, j = ij(pl.program_id(0), pl.program_id(1))
        if n_side:
            @pl.when(pl.program_id(1) < side_steps)
            def _():
                refs[n_in + 1][...] = refs[n_in - 1][...].astype(BF16)
        if cast:
            @pl.when(i == 0)
            def _():
                def cp(c, carry):
                    sl = pl.ds(pl.multiple_of(c * kc, kc), kc)
                    for p in range(n_parts):
                        scratch[p][sl, :] = w_refs[p][sl, :].astype(BF16)
                    return carry
                lax.fori_loop(0, k // kc, cp, 0)
            ws = scratch
        else:
            ws = w_refs
        av = a_ref[...]
        accs = [_dot(av, ws[p][...]) for p in range(n_parts)]
        epilogue(accs, i, j, ex_refs, o_ref)

    outs = pl.pallas_call(
        body,
        grid=grid,
        in_specs=in_specs,
        out_specs=out_specs,
        out_shape=out_shape,
        scratch_shapes=[pltpu.VMEM((k, tn), BF16) for _ in range(n_parts)] if cast else [],
        compiler_params=_params(("arbitrary", "arbitrary")),
        name=name,
    )(a, *([w] * n_parts), *[e[0] for e in extras], *([to_bf16] if n_side else []))
    return outs if n_side else outs[0]


def _row_gate(mod_ref, idx, i, tm, n_lat):
    rows = i * tm + lax.broadcasted_iota(jnp.int32, (tm, 1), 0)
    return jnp.where(rows < n_lat, mod_ref[0, idx:idx + 1, :], mod_ref[1, idx:idx + 1, :])


def _epi_swiglu(accs, i, j, ex, o_ref):
    a, g = accs
    o_ref[...] = (a * jax.nn.sigmoid(a) * g).astype(o_ref.dtype)


def _epi_residual(accs, i, j, ex, o_ref, *, gate_idx, tm, n_lat):
    res_ref, mod_ref = ex
    o_ref[...] = res_ref[...] + _row_gate(mod_ref, gate_idx, i, tm, n_lat) * accs[0]


def _epi_glu_residual(accs, i, j, ex, o_ref, *, gate_idx, tm, n_lat):
    res_ref, mod_ref = ex
    a, g = accs
    o_ref[...] = res_ref[...] + _row_gate(mod_ref, gate_idx, i, tm, n_lat) * (a * jax.nn.sigmoid(g))


def _residual_extras(xs, mod, tm, tn):
    return ((xs, (tm, tn), lambda i, j: (i, j)), (mod, (2, 6, tn), lambda i, j: (0, 0, j)))


def _mm_residual(a, w, xs, mod, *, gate_idx, rows, n_lat, w_outer, name):
    n = w.shape[1]
    tm = _pick(rows, MM_ROW_TILES_RESIDUAL if w_outer else (768, 512, 640, 256))
    tn = _pick(n, (512, 256) if w_outer else (256, 128))
    return _mm(a, w, rows=rows, tm=tm, tn=tn, out_cols=n, out_dtype=F32, col_maps=(lambda j: j,),
               epilogue=functools.partial(_epi_residual, gate_idx=gate_idx, tm=tm, n_lat=n_lat),
               extras=_residual_extras(xs, mod, tm, tn), w_outer=w_outer, name=name)


def _pool_kernel(xp_ref, x_ref, xn_ref, c_ref, g_ref, gf_ref, mod_ref, pw_ref, ps_ref, o_ref, h_ref, *, tr, n_lat, n_tot):
    i = pl.program_id(0)
    halo = POOL_HALO
    gain = g_ref[...]
    shift, scale, gate = mod_ref[0, 0:1, :], mod_ref[0, 1:2, :], mod_ref[0, 2:3, :]
    is_lat = i * tr < n_lat
    lat_rows = lax.broadcasted_iota(jnp.int32, (tr, 1), 0) * 0 + jnp.where(is_lat, 1, 0) > 0
    x = jnp.where(lat_rows, x_ref[...], c_ref[...])
    ue = jnp.concatenate([_rms_mod(xp_ref[...], gain, shift, scale), _rms_mod(x, gain, shift, scale),
                          _rms_mod(xn_ref[...], gain, shift, scale)], axis=0)
    seq_lo = jnp.where(is_lat, 0, n_lat)
    seq_hi = jnp.where(is_lat, n_lat, n_tot)
    t_g = i * tr + lax.broadcasted_iota(jnp.int32, (tr, 1), 0)
    s_g = i * tr - halo + lax.broadcasted_iota(jnp.int32, (1, tr + 2 * halo), 1)
    s_ok = (s_g >= seq_lo) & (s_g < seq_hi)
    pg = ue.shape[1] // len(POOL_WINDOWS)
    for gi, w in enumerate(POOL_WINDOWS):
        cols = slice(gi * pg, (gi + 1) * pg)
        ug = ue[:, cols]
        band = (s_g >= t_g - w // 2) & (s_g < t_g + w // 2) & s_ok
        ssum = _dot(jnp.where(band, 1.0, 0.0).astype(BF16), ug.astype(BF16))
        cnt = jnp.minimum(t_g + w // 2, seq_hi) - jnp.maximum(t_g - w // 2, seq_lo)
        p = ssum / cnt.astype(F32) - ug[halo:halo + tr]
        y = _dot(p.astype(BF16), pw_ref[gi]) * ps_ref[:, cols]
        o_ref[:, cols] = x[:, cols] + gate[:, cols] * y
    h_ref[...] = _rms_mod(o_ref[...], gf_ref[...], mod_ref[0, 3:4, :], mod_ref[0, 4:5, :]).astype(h_ref.dtype)


def _pool_layer(lat, ctx_src, ctx_block, gain, gain_ffn, mod, pool_w, pool_scale, *, rows, n_lat, n_tot):
    d = lat.shape[1]
    tr, halo = ROW_TILE, POOL_HALO
    hb = tr // halo
    last_hb = n_lat // halo - 1
    vec = pl.BlockSpec((1, d), lambda i: (0, 0))
    tile = pl.BlockSpec((tr, d), lambda i: (i, 0))
    return pl.pallas_call(
        functools.partial(_pool_kernel, tr=tr, n_lat=n_lat, n_tot=n_tot),
        grid=(rows // tr,),
        in_specs=[
            pl.BlockSpec((halo, d), lambda i: (jnp.clip(i * hb - 1, 0, last_hb), 0)),
            pl.BlockSpec((tr, d), lambda i: (jnp.minimum(i, n_lat // tr - 1), 0)),
            pl.BlockSpec((halo, d), lambda i: (jnp.minimum((i + 1) * hb, last_hb), 0)),
            pl.BlockSpec((tr, d), lambda i: (ctx_block, 0)),
            vec, vec,
            pl.BlockSpec((1, 6, d), lambda i: (jnp.where(i * tr >= n_lat, 1, 0), 0, 0)),
            pl.BlockSpec(pool_w.shape, lambda i: (0, 0, 0)),
            vec,
        ],
        out_specs=[tile, tile],
        out_shape=[jax.ShapeDtypeStruct((rows, d), F32), jax.ShapeDtypeStruct((rows, d), BF16)],
        compiler_params=_params(("arbitrary",)),
        name="pool_mixer",
    )(lat, lat, lat, ctx_src, gain.reshape(1, d), gain_ffn.reshape(1, d), mod, pool_w.astype(BF16),
      pool_scale.reshape(1, d))


def _s5_prep_kernel(lre_ref, lim_ref, ls_ref, bre_ref, bim_ref, are_ref, aim_ref, bbre_ref, bbim_ref):
    lam_re, lam_im = lre_ref[0], lim_ref[0]
    dt = jnp.exp(ls_ref[0])
    z_re, z_im = lam_re * dt, lam_im * dt
    mag = jnp.exp(z_re)
    a_re, a_im = mag * jnp.cos(z_im), mag * jnp.sin(z_im)
    den = lam_re * lam_re + lam_im * lam_im
    k_re = ((a_re - 1.0) * lam_re + a_im * lam_im) / den
    k_im = (a_im * lam_re - (a_re - 1.0) * lam_im) / den
    are_ref[0] = a_re
    aim_ref[0] = a_im
    b_re, b_im = bre_ref[0], bim_ref[0]
    bbre_ref[0] = k_re[:, None, :] * b_re - k_im[:, None, :] * b_im
    bbim_ref[0] = k_re[:, None, :] * b_im + k_im[:, None, :] * b_re


def _s5_prep(lam_re, lam_im, log_step, b_re, b_im):
    _, g, p, h = b_re.shape
    spec2 = pl.BlockSpec((1, g, p), lambda d: (d, 0, 0))
    spec3 = pl.BlockSpec((1, g, h, p), lambda d: (d, 0, 0, 0))
    return pl.pallas_call(
        _s5_prep_kernel,
        grid=(2,),
        in_specs=[spec2, spec2, pl.BlockSpec((1, g, 1), lambda d: (d, 0, 0)), spec3, spec3],
        out_specs=[spec2, spec2, spec3, spec3],
        out_shape=[jax.ShapeDtypeStruct((2, g, p), F32)] * 2 + [jax.ShapeDtypeStruct((2, g, h, p), F32)] * 2,
        compiler_params=_params(("arbitrary",)),
        name="s5_prep",
    )(lam_re, lam_im, log_step[..., None], jnp.swapaxes(b_re, 2, 3), jnp.swapaxes(b_im, 2, 3))


def _s5_pack(a_re, a_im, bb_re, bb_im, c_re, c_im):
    _, g, h, p = bb_re.shape
    r = g // 2
    eye2 = jnp.eye(2, dtype=F32)
    eye4 = jnp.eye(4, dtype=F32)
    slot = eye4[jnp.arange(r) % 4]

    def win_part(bb):
        t = bb.reshape(2, r, 2, h, p)
        t = t[:, :, :, :, None, :] * eye2[None, None, :, None, :, None]
        t = t.reshape(2, r, 2 * h, 2 * p)
        t = slot[None, :, :, None, None] * t[:, :, None, :, :]
        return t.reshape(2, r, 4 * 2 * h, 2 * p)

    def c_part(c):
        t = jnp.swapaxes(c, 2, 3).reshape(2, r, 2, p, h)
        t = t[:, :, :, :, None, :] * eye2[None, None, :, None, :, None]
        t = t.reshape(2, r, 2 * p, 2 * h)
        t = t[:, :, :, None, :] * slot[None, :, None, :, None]
        return t.reshape(2, r, 2 * p, 4 * 2 * h)

    win = jnp.concatenate([win_part(bb_re), win_part(bb_im)], axis=-1).astype(BF16)
    cmat = jnp.concatenate([c_part(c_re), c_part(-c_im)], axis=2).astype(BF16)
    return win, cmat, a_re.reshape(2, r, 2 * p), a_im.reshape(2, r, 2 * p)


def _s5_scan_kernel(u_ref, win_ref, cmat_ref, are_ref, aim_ref, y_ref, xre, xim, hre, him, *, nsub, rb):
    g0, c = pl.program_id(0), pl.program_id(1)
    rev = g0 // nsub == 1
    tc, pitch = S5_CHUNK, rb + S5_ROW_PAD
    block_rows = lambda r: pl.ds(r, tc, stride=pitch)

    @pl.when(c == 0)
    def _():
        hre[...] = jnp.zeros_like(hre)
        him[...] = jnp.zeros_like(him)

    for q in range(rb // 4):
        ut = u_ref[:, q * LANES:(q + 1) * LANES]
        for s in range(4):
            r = q * 4 + s
            x = _dot(ut, win_ref[0, r])
            xre[block_rows(r), :] = x[:, :LANES]
            xim[block_rows(r), :] = x[:, LANES:]

    a_re, a_im = are_ref[0], aim_ref[0]

    def step(t, carry):
        h_re, h_im = carry
        tt = jnp.where(rev, tc - 1 - t, t)
        rows = pl.ds(pl.multiple_of(tt * pitch, 8), rb)
        n_re = a_re * h_re - a_im * h_im + xre[rows, :]
        n_im = a_re * h_im + a_im * h_re + xim[rows, :]
        xre[rows, :] = n_re
        xim[rows, :] = n_im
        return n_re, n_im

    h_re, h_im = lax.fori_loop(0, tc, step, (hre[...], him[...]), unroll=8)
    hre[...] = h_re
    him[...] = h_im

    for q in range(rb // 4):
        acc = jnp.zeros((tc, LANES), F32)
        for s in range(4):
            r = q * 4 + s
            acc += (_dot(xre[block_rows(r), :].astype(BF16), cmat_ref[0, r, :LANES, :])
                    + _dot(xim[block_rows(r), :].astype(BF16), cmat_ref[0, r, LANES:, :]))
        y_ref[0, :, q * LANES:(q + 1) * LANES] = acc.astype(y_ref.dtype)


def _s5_scan(u, win, cmat, a_re, a_im, *, n_lat):
    n_tot, d = u.shape
    r_all = win.shape[1]
    rb = min(S5_BLOCKS, r_all)
    nsub = r_all // rb
    tc = S5_CHUNK
    n_c, n_lc = n_tot // tc, n_lat // tc
    cols = rb * 2 * S5_GROUP

    def split(w):
        return w.reshape((2 * nsub, rb) + w.shape[2:])

    def chunk(g0, c):
        return jnp.where(g0 // nsub == 0, (c + n_lc) % n_c, n_c - 1 - c)

    wspec = lambda shape: pl.BlockSpec((1,) + shape, lambda g0, c: (g0,) + (0,) * len(shape))
    return pl.pallas_call(
        functools.partial(_s5_scan_kernel, nsub=nsub, rb=rb),
        grid=(2 * nsub, n_c),
        in_specs=[
            pl.BlockSpec((tc, cols), lambda g0, c: (chunk(g0, c), g0 % nsub)),
            wspec((rb, LANES, 2 * LANES)), wspec((rb, 2 * LANES, LANES)), wspec((rb, LANES)), wspec((rb, LANES)),
        ],
        out_specs=pl.BlockSpec((1, tc, cols), lambda g0, c: (g0 // nsub, chunk(g0, c), g0 % nsub)),
        out_shape=jax.ShapeDtypeStruct((2, n_tot, d), BF16),
        scratch_shapes=[pltpu.VMEM((tc * (rb + S5_ROW_PAD), LANES), F32)] * 2 + [pltpu.VMEM((rb, LANES), F32)] * 2,
        compiler_params=_params(("arbitrary", "arbitrary")),
        name="s5_scan",
    )(u, split(win), split(cmat), split(a_re), split(a_im))


def _s5_gelu_kernel(y_ref, u_ref, d_ref, o_ref):
    v = y_ref[0].astype(F32) + y_ref[1].astype(F32) + d_ref[...] * u_ref[...].astype(F32)
    o_ref[...] = jax.nn.gelu(v, approximate=True).astype(o_ref.dtype)


def _s5_gelu(y, u, dvec, *, rows):
    d = u.shape[1]
    tr = ROW_TILE
    return pl.pallas_call(
        _s5_gelu_kernel,
        grid=(rows // tr,),
        in_specs=[pl.BlockSpec((2, tr, d), lambda i: (0, i, 0)), pl.BlockSpec((tr, d), lambda i: (i, 0)),
                  pl.BlockSpec((1, d), lambda i: (0, 0))],
        out_specs=pl.BlockSpec((tr, d), lambda i: (i, 0)),
        out_shape=jax.ShapeDtypeStruct((rows, d), BF16),
        compiler_params=_params(("arbitrary",)),
        name="s5_gelu",
    )(y, u, dvec.reshape(1, d))


def _s5_expand_kernel(are_ref, aim_ref, bre_ref, bim_ref, cre_ref, cim_ref,
                      fre_ref, fim_ref, ere_ref, eim_ref, k_ref, alre_ref, alim_ref, ca_ref):
    n_len = S5_CONV_LEN
    a_re, a_im = are_ref[0][:, None, :], aim_ref[0][:, None, :]
    b_re, b_im = bre_ref[0], bim_ref[0]
    c_re, c_im = cre_ref[0], cim_ref[0]
    n_grp, n_h, n_p = b_re.shape
    p_re, p_im = jnp.ones_like(a_re), jnp.zeros_like(a_re)
    for n in range(n_len):
        fre_ref[0, :, n] = p_re * b_re - p_im * b_im
        fim_ref[0, :, n] = p_re * b_im + p_im * b_re
        rows = slice(n * n_h, (n + 1) * n_h)
        ca_ref[:, rows, :n_p] = (p_re * c_re - p_im * c_im).astype(BF16)
        ca_ref[:, rows, n_p:] = (-(p_re * c_im + p_im * c_re)).astype(BF16)
        p_re, p_im = p_re * a_re - p_im * a_im, p_re * a_im + p_im * a_re
        ere_ref[0, :, n] = p_re * c_re - p_im * c_im
        eim_ref[0, :, n] = p_re * c_im + p_im * c_re
    alre_ref[0] = p_re[:, 0, :]
    alim_ref[0] = p_im[:, 0, :]
    bcat = jnp.concatenate([b_re, b_im], axis=-1).astype(BF16)
    for g in range(n_grp):
        k_ref[0, g] = _dot_nt(ca_ref[g], bcat[g])


def _s5_expand(a_re, a_im, bb_re_t, bb_im_t, c_re, c_im):
    _, n_g, n_h, n_p = bb_re_t.shape
    n_len, gb = S5_CONV_LEN, 16
    vec = pl.BlockSpec((1, gb, n_p), lambda d, b: (d, b, 0))
    mat = pl.BlockSpec((1, gb, n_h, n_p), lambda d, b: (d, b, 0, 0))
    powers = pl.BlockSpec((1, gb, n_len, n_h, n_p), lambda d, b: (d, b, 0, 0, 0))
    pshape = jax.ShapeDtypeStruct((2, n_g, n_len, n_h, n_p), F32)
    return pl.pallas_call(
        _s5_expand_kernel,
        grid=(2, n_g // gb),
        in_specs=[vec, vec, mat, mat, mat, mat],
        out_specs=[powers, powers, powers, powers,
                   pl.BlockSpec((1, gb, n_len * n_h, n_h), lambda d, b: (d, b, 0, 0)), vec, vec],
        out_shape=[pshape, pshape, pshape, pshape, jax.ShapeDtypeStruct((2, n_g, n_len * n_h, n_h), F32),
                   jax.ShapeDtypeStruct((2, n_g, n_p), F32), jax.ShapeDtypeStruct((2, n_g, n_p), F32)],
        scratch_shapes=[pltpu.VMEM((gb, n_len * n_h, 2 * n_p), BF16)],
        compiler_params=_params(("arbitrary", "arbitrary")),
        name="s5_expand",
    )(a_re, a_im, bb_re_t, bb_im_t, c_re, c_im)


def _s5_conv_weights(f_re, f_im, e_re, e_im, kmat, al_re, al_im):
    _, n_g, n_len, n_h, n_p = f_re.shape
    lh = n_len * n_h
    f_dir = lambda f: jnp.stack([jnp.flip(f[0], axis=1), f[1]])
    e_dir = lambda e: jnp.stack([e[0], jnp.flip(e[1], axis=1)])
    parity = jnp.arange(n_g) % 2
    lane_sel = (parity[:, None] == jnp.arange(2)[None, :]).astype(F32)

    def drive(f):
        f = f_dir(f).reshape(2, n_g, lh, 1, n_p) * lane_sel[None, :, None, :, None]
        return f.reshape(2, n_g, lh, 2 * n_p)

    def readout(e, sign):
        e = jnp.swapaxes(e_dir(e).reshape(2, n_g, lh, n_p), 2, 3) * sign
        e = e[:, :, None, :, :] * lane_sel[None, :, :, None, None]
        return e.reshape(2, n_g, 2 * n_p, lh)

    fcat = jnp.concatenate([drive(f_re), drive(f_im)], axis=-1).astype(BF16)
    ecat = jnp.concatenate([readout(e_re, 1.0), readout(e_im, -1.0)], axis=2).astype(BF16)
    lag = jnp.arange(n_len)[None, :] - jnp.arange(n_len)[:, None]
    k5 = kmat.reshape(2, n_g, n_len, n_h, n_h)
    fwd = jnp.where((lag >= 0)[None, :, :, None, None], k5[0][:, jnp.clip(lag, 0, n_len - 1)], 0.0)
    bwd = jnp.where((lag <= 0)[None, :, :, None, None], k5[1][:, jnp.clip(-lag, 0, n_len - 1)], 0.0)
    mt = jnp.transpose(fwd + bwd, (0, 1, 4, 2, 3)).reshape(n_g, lh, lh).astype(BF16)
    pair = lambda v: v.reshape(2, n_g // 2, 2 * n_p)
    return fcat, mt, ecat, pair(al_re), pair(al_im)


def _s5_conv_kernel(u_ref, fcat_ref, mt_ref, ecat_ref, alre_ref, alim_ref, d_ref, o_ref, dre0, dim0, dre1, dim1,
                    *, n_lc, pitch):
    gb, n_c, lh = u_ref.shape
    n_pair = gb // 2
    dre, dim = (dre0, dre1), (dim0, dim1)
    for p in range(n_pair):
        for dr in range(2):
            acc = _dot(u_ref[2 * p], fcat_ref[dr, 2 * p]) + _dot(u_ref[2 * p + 1], fcat_ref[dr, 2 * p + 1])
            dre[dr][p * pitch:p * pitch + n_c, :] = acc[:, :LANES]
            dim[dr][p * pitch:p * pitch + n_c, :] = acc[:, LANES:]

    al = [(alre_ref[dr], alim_ref[dr]) for dr in range(2)]

    def step(k, carry):
        out = []
        for dr, (h_re, h_im) in enumerate(carry):
            c = (k + n_lc) % n_c if dr == 0 else n_c - 1 - k
            rows = pl.ds(c, n_pair, stride=pitch)
            x_re, x_im = dre[dr][rows, :], dim[dr][rows, :]
            dre[dr][rows, :] = h_re
            dim[dr][rows, :] = h_im
            a_re, a_im = al[dr]
            out.append((a_re * h_re - a_im * h_im + x_re, a_re * h_im + a_im * h_re + x_im))
        return tuple(out)

    zero = jnp.zeros((n_pair, LANES), F32)
    lax.fori_loop(0, n_c, step, ((zero, zero), (zero, zero)))

    for g in range(gb):
        p = g // 2
        u = u_ref[g]
        y = _dot(u, mt_ref[g])
        for dr in range(2):
            h_in = jnp.concatenate([dre[dr][p * pitch:p * pitch + n_c, :], dim[dr][p * pitch:p * pitch + n_c, :]],
                                   axis=1).astype(BF16)
            y += _dot(h_in, ecat_ref[dr, g])
        v = y + d_ref[g] * u.astype(F32)
        o_ref[g] = jax.nn.gelu(v, approximate=True).astype(o_ref.dtype)


def _s5_conv(u, fcat, mt, ecat, al_re, al_im, dvec, *, n_lat):
    n_tot, d = u.shape
    n_len, n_h = S5_CONV_LEN, S5_GROUP
    n_g, lh = d // n_h, n_len * n_h
    n_c, gb = n_tot // n_len, S5_CONV_GROUPS
    pitch = n_c + 8
    assert n_c % 16 == 0 and n_g % gb == 0
    u_r = u.reshape(n_c, n_len, n_g, n_h).transpose(2, 0, 1, 3).reshape(n_g, n_c, lh)
    d_r = jnp.tile(dvec.reshape(n_g, 1, n_h), (1, 1, n_len))
    slab = pltpu.VMEM((gb // 2 * pitch, LANES), F32)
    g_r = pl.pallas_call(
        functools.partial(_s5_conv_kernel, n_lc=n_lat // n_len, pitch=pitch),
        grid=(n_g // gb,),
        in_specs=[
            pl.BlockSpec((gb, n_c, lh), lambda b: (b, 0, 0)),
            pl.BlockSpec((2, gb, lh, 2 * LANES), lambda b: (0, b, 0, 0)),
            pl.BlockSpec((gb, lh, lh), lambda b: (b, 0, 0)),
            pl.BlockSpec((2, gb, 2 * LANES, lh), lambda b: (0, b, 0, 0)),
            pl.BlockSpec((2, gb // 2, LANES), lambda b: (0, b, 0)),
            pl.BlockSpec((2, gb // 2, LANES), lambda b: (0, b, 0)),
            pl.BlockSpec((gb, 1, lh), lambda b: (b, 0, 0)),
        ],
        out_specs=pl.BlockSpec((gb, n_c, lh), lambda b: (b, 0, 0)),
        out_shape=jax.ShapeDtypeStruct((n_g, n_c, lh), BF16),
        scratch_shapes=[slab] * 4,
        compiler_params=_params(("arbitrary",)),
        name="s5_conv",
    )(u_r, fcat, mt, ecat, al_re, al_im, d_r)
    return g_r.reshape(n_g, n_c, n_len, n_h).transpose(1, 2, 0, 3).reshape(n_tot, d)


def _rope_tables(n_lat, n_ctx):
    quarter = HEAD_DIM // 4
    inv = jnp.power(ROPE_BASE, -jnp.arange(quarter, dtype=F32) / quarter)
    t = jnp.arange(n_lat)
    ang_r = (t // GRID_W).astype(F32)[:, None] * inv
    ang_c = (t % GRID_W).astype(F32)[:, None] * inv
    ang = jnp.concatenate([ang_r, ang_r, ang_c, ang_c], axis=-1)
    sign = jnp.where((jnp.arange(HEAD_DIM) % (2 * quarter)) < quarter, -1.0, 1.0)
    cos = jnp.concatenate([jnp.cos(ang), jnp.ones((n_ctx, HEAD_DIM), F32)], axis=0)
    sin = jnp.concatenate([jnp.sin(ang) * sign, jnp.zeros((n_ctx, HEAD_DIM), F32)], axis=0)
    return cos, sin


def _epi_qkv(accs, i, j, ex, o_ref, *, n_norm_tiles, rope, tn):
    gain_ref = ex[0]
    heads_per_acc = tn // len(accs) // HEAD_DIM

    @pl.when(j < n_norm_tiles)
    def _():
        quarter = HEAD_DIM // 4
        lane = lax.broadcasted_iota(jnp.int32, (1, HEAD_DIM), 1)
        first = (lane % (2 * quarter)) < quarter
        for h in range(tn // HEAD_DIM):
            cols = slice(h * HEAD_DIM, (h + 1) * HEAD_DIM)
            hl = h % heads_per_acc
            x = accs[h // heads_per_acc][:, hl * HEAD_DIM:(hl + 1) * HEAD_DIM]
            ms = jnp.mean(x * x, axis=-1, keepdims=True)
            xn = x * lax.rsqrt(ms + RMS_EPS) * gain_ref[:, cols]
            if rope:
                cos_ref, sin_ref = ex[1], ex[2]
                partner = jnp.where(first, pltpu.roll(xn, HEAD_DIM - quarter, 1), pltpu.roll(xn, quarter, 1))
                xn = xn * cos_ref[...] + partner * sin_ref[...]
            o_ref[:, cols] = xn.astype(o_ref.dtype)

    @pl.when(j >= n_norm_tiles)
    def _():
        sub = tn // len(accs)
        for s, acc in enumerate(accs):
            o_ref[:, s * sub:(s + 1) * sub] = acc.astype(o_ref.dtype)


def _qkv_proj(u, w_qkv, q_gain, k_gain, *, n_q, n_kv, rope_tables, rows):
    n = w_qkv.shape[1]
    tm = _pick(rows, MM_ROW_TILES)
    tn = _pick(n_kv * HEAD_DIM, (512, 256, 128))
    gain = jnp.concatenate([jnp.tile(q_gain * HEAD_DIM ** -0.5, n_q), jnp.tile(k_gain, n_kv),
                            jnp.ones((n_kv * HEAD_DIM,), F32)]).reshape(1, n)
    extras = [(gain, (1, tn), lambda i, j: (0, j))]
    if rope_tables is not None:
        extras += [(t, (tm, HEAD_DIM), lambda i, j: (i, 0)) for t in rope_tables]
    epi = functools.partial(_epi_qkv, n_norm_tiles=(n_q + n_kv) * HEAD_DIM // tn, rope=rope_tables is not None, tn=tn)
    return _mm(u, w_qkv, rows=rows, tm=tm, tn=tn, out_cols=n, out_dtype=BF16, col_maps=(lambda j: j,),
               epilogue=epi, extras=tuple(extras), w_outer=True, name="qkv_proj")


def _softmax_pv(tiles, values, extra_logit=None):
    rows = tiles[0].shape[0]
    mx = functools.reduce(jnp.maximum, tiles)
    if extra_logit is not None:
        mx = jnp.maximum(mx, extra_logit)
    m = jnp.broadcast_to(jnp.max(mx, axis=-1, keepdims=True), (rows, LANES))
    p = jnp.concatenate([jnp.exp(t - m).astype(BF16) for t in tiles], axis=1)
    ones = jnp.ones((LANES, LANES), BF16)
    v_aug = jnp.concatenate([jnp.concatenate([v, ones], axis=1) for v in values], axis=0)
    acc = _dot(p, v_aug)
    den = acc[:, LANES:]
    if extra_logit is not None:
        den = den + jnp.exp(extra_logit - m)
    return acc[:, :LANES] / den


def _swa_kernel(sink_ref, q_ref, kp_ref, kc_ref, kn_ref, vp_ref, vc_ref, vn_ref, kx_ref, vx_ref, o_ref, *, n_lat_blocks):
    h, b = pl.program_id(0), pl.program_id(1)
    blk = SWA_WINDOW
    is_lat = b < n_lat_blocks
    qi = lax.broadcasted_iota(jnp.int32, (SWA_GROUP * blk, 1), 0) & (blk - 1)
    kj = lax.broadcasted_iota(jnp.int32, (1, blk), 1)
    off = lambda ok: jnp.where(ok, 0, blk + 1)
    m_prev = kj >= qi + off(is_lat & (b > 0))
    m_cur = kj >= qi * 0 + off(is_lat)
    m_next = kj <= qi - off(is_lat & (b < n_lat_blocks - 1))
    n_ctx_tiles = kx_ref.shape[0] // LANES
    hp = kp_ref.shape[1] // HEAD_DIM
    for hh in range(hp):
        kcols = slice(hh * HEAD_DIM, (hh + 1) * HEAD_DIM)
        qh = lambda g: (hh * SWA_GROUP + g) * HEAD_DIM
        q = jnp.concatenate([q_ref[:, qh(g):qh(g) + HEAD_DIM] for g in range(SWA_GROUP)], axis=0)
        sink = jnp.concatenate([jnp.full((blk, LANES), sink_ref[(h * hp + hh) * SWA_GROUP + g], F32)
                                for g in range(SWA_GROUP)], axis=0)
        s_ctx = _dot_nt(q, kx_ref[:, kcols])
        tiles = [jnp.where(m_prev, _dot_nt(q, kp_ref[:, kcols]), NEG_INF),
                 jnp.where(m_cur, _dot_nt(q, kc_ref[:, kcols]), NEG_INF),
                 jnp.where(m_next, _dot_nt(q, kn_ref[:, kcols]), NEG_INF)]
        tiles += [s_ctx[:, t * LANES:(t + 1) * LANES] for t in range(n_ctx_tiles)]
        vals = [vp_ref[:, kcols], vc_ref[:, kcols], vn_ref[:, kcols]]
        vals += [vx_ref[t * LANES:(t + 1) * LANES, kcols] for t in range(n_ctx_tiles)]
        o = _softmax_pv(tiles, vals, sink).astype(o_ref.dtype)
        for g in range(SWA_GROUP):
            o_ref[:, qh(g):qh(g) + HEAD_DIM] = o[g * blk:(g + 1) * blk]


def _swa_attention(qkv, sinks, *, n_q, n_kv, rows, n_lat):
    blk = SWA_WINDOW
    nlb = n_lat // blk
    ctx_rows = (qkv.shape[0] - n_lat)
    hp = min(SWA_KV_PER_STEP, n_kv)
    assert n_kv % hp == 0 and n_q % hp == 0
    kcol = lambda h: n_q // hp + h
    vcol = lambda h: (n_q + n_kv) // hp + h
    prev = lambda b: jnp.clip(b - 1, 0, nlb - 1)
    cur = lambda b: jnp.minimum(b, nlb - 1)
    nxt = lambda b: jnp.clip(b + 1, 0, nlb - 1)
    kv = lambda rowf, colf: pl.BlockSpec((blk, hp * HEAD_DIM), lambda h, b: (rowf(b), colf(h)))
    ctx = lambda colf: pl.BlockSpec((ctx_rows, hp * HEAD_DIM), lambda h, b: (n_lat // ctx_rows, colf(h)))
    return pl.pallas_call(
        functools.partial(_swa_kernel, n_lat_blocks=nlb),
        grid=(n_kv // hp, rows // blk),
        in_specs=[
            pl.BlockSpec(memory_space=pltpu.SMEM),
            pl.BlockSpec((blk, hp * SWA_GROUP * HEAD_DIM), lambda h, b: (b, h)),
            kv(prev, kcol), kv(cur, kcol), kv(nxt, kcol), kv(prev, vcol), kv(cur, vcol), kv(nxt, vcol),
            ctx(kcol), ctx(vcol),
        ],
        out_specs=pl.BlockSpec((blk, hp * SWA_GROUP * HEAD_DIM), lambda h, b: (b, h)),
        out_shape=jax.ShapeDtypeStruct((rows, n_q * HEAD_DIM), BF16),
        compiler_params=_params(("arbitrary", "arbitrary")),
        name="swa_attention",
    )(sinks, *([qkv] * 9))


def _na_bias_kernel(rpb_ref, o_ref):
    h = pl.program_id(0)
    n_ri, n_ci = 2 * NA_WIN_ROWS - 1, 2 * NA_WIN_COLS - 1
    cq = lax.broadcasted_iota(jnp.int32, (GRID_W, 2 * GRID_W), 0)
    ck = lax.broadcasted_iota(jnp.int32, (GRID_W, 2 * GRID_W), 1)
    second = ck >= GRID_W
    ck = jnp.where(second, ck - GRID_W, ck)
    c0 = jnp.clip(cq - NA_WIN_COLS // 2, 0, GRID_W - NA_WIN_COLS)
    in_win = (ck >= c0) & (ck < c0 + NA_WIN_COLS)
    cidx = jnp.clip(ck - cq + NA_WIN_COLS - 1, 0, n_ci - 1)
    neg = jnp.full((GRID_W, 2 * GRID_W), NEG_INF, F32)
    per_offset = []
    for ri in range(n_ri):
        t = neg
        for dci in range(n_ci):
            t = jnp.where(cidx == dci, rpb_ref[(h * n_ri + ri) * n_ci + dci], t)
        per_offset.append(jnp.where(in_win, t, NEG_INF))
    for k in range(n_ri + 1):
        o_ref[0, k] = jnp.where(second, per_offset[k] if k < n_ri else neg, per_offset[k - 1] if k >= 1 else neg)


def _na_bias(rpb):
    n_heads = rpb.shape[0]
    n_tiles = 2 * NA_WIN_ROWS
    return pl.pallas_call(
        _na_bias_kernel,
        grid=(n_heads,),
        in_specs=[pl.BlockSpec(memory_space=pltpu.SMEM)],
        out_specs=pl.BlockSpec((1, n_tiles, GRID_W, 2 * GRID_W), lambda h: (h, 0, 0, 0)),
        out_shape=jax.ShapeDtypeStruct((n_heads, n_tiles, GRID_W, 2 * GRID_W), F32),
        compiler_params=_params(("arbitrary",)),
        name="na_bias",
    )(rpb.reshape(-1))


def _na_kernel(q_ref, kp_ref, kc_ref, kn_ref, vp_ref, vc_ref, vn_ref, kx_ref, vx_ref, bias_ref, o_ref, *, n_grid_rows):
    b = pl.program_id(1)
    qr, w = NA_QROWS, GRID_W
    is_lat = b * qr < n_grid_rows
    lane = lax.broadcasted_iota(jnp.int32, (1, 2 * w), 1)
    n_kt = qr // 2
    windows = {}
    for kb in range(3):
        for dq in range(qr):
            r = b * qr + dq
            r0 = jnp.clip(r - NA_WIN_ROWS // 2, 0, n_grid_rows - NA_WIN_ROWS)
            for jt in range(n_kt):
                kr = b * qr + (kb - 1) * qr + 2 * jt
                ok0 = (kr >= r0) & (kr < r0 + NA_WIN_ROWS) & is_lat
                ok1 = (kr + 1 >= r0) & (kr + 1 < r0 + NA_WIN_ROWS) & is_lat
                lo, hi = jnp.where(ok0, 0, w), jnp.where(ok1, 2 * w, w)
                windows[kb, dq, jt] = (lane >= lo) & (lane < hi)
    for hh in range(q_ref.shape[1] // HEAD_DIM):
        cols = slice(hh * HEAD_DIM, (hh + 1) * HEAD_DIM)
        q = q_ref[:, cols]
        tiles, vals = [], []
        for kb, (k_ref, v_ref) in enumerate(((kp_ref, vp_ref), (kc_ref, vc_ref), (kn_ref, vn_ref))):
            s = _dot_nt(q, k_ref[:, cols])
            for jt in range(n_kt):
                parts = []
                for dq in range(qr):
                    ri0 = (kb - 1) * qr + 2 * jt - dq + NA_WIN_ROWS - 1
                    t = s[dq * w:(dq + 1) * w, jt * 2 * w:(jt + 1) * 2 * w] + bias_ref[hh, ri0 + 1]
                    parts.append(jnp.where(windows[kb, dq, jt], t, NEG_INF))
                tiles.append(jnp.concatenate(parts, axis=0))
                vals.append(v_ref[jt * LANES:(jt + 1) * LANES, cols])
        s_ctx = _dot_nt(q, kx_ref[:, cols])
        for t in range(kx_ref.shape[0] // LANES):
            tiles.append(s_ctx[:, t * LANES:(t + 1) * LANES])
            vals.append(vx_ref[t * LANES:(t + 1) * LANES, cols])
        o_ref[:, cols] = _softmax_pv(tiles, vals).astype(o_ref.dtype)


def _na_attention(qkv, bias, *, n_heads, rows, n_lat):
    blk = NA_QROWS * GRID_W
    assert blk == qkv.shape[0] - n_lat
    nlb = n_lat // blk
    hp = min(NA_HEADS_PER_STEP, n_heads)
    assert n_heads % hp == 0
    kcol = lambda h: n_heads // hp + h
    vcol = lambda h: 2 * n_heads // hp + h
    prev = lambda b: jnp.clip(b - 1, 0, nlb - 1)
    cur = lambda b: jnp.minimum(b, nlb - 1)
    nxt = lambda b: jnp.clip(b + 1, 0, nlb - 1)
    spec = lambda rowf, colf: pl.BlockSpec((blk, hp * HEAD_DIM), lambda h, b: (rowf(b), colf(h)))
    ctx = lambda b: nlb
    return pl.pallas_call(
        functools.partial(_na_kernel, n_grid_rows=n_lat // GRID_W),
        grid=(n_heads // hp, rows // blk),
        in_specs=[
            spec(lambda b: b, lambda h: h),
            spec(prev, kcol), spec(cur, kcol), spec(nxt, kcol), spec(prev, vcol), spec(cur, vcol), spec(nxt, vcol),
            spec(ctx, kcol), spec(ctx, vcol),
            pl.BlockSpec((hp,) + bias.shape[1:], lambda h, b: (h, 0, 0, 0)),
        ],
        out_specs=spec(lambda b: b, lambda h: h),
        out_shape=jax.ShapeDtypeStruct((rows, n_heads * HEAD_DIM), BF16),
        compiler_params=_params(("arbitrary", "arbitrary")),
        name="na_attention",
    )(*([qkv] * 9), bias)


def kernel(x, c, ctx, c_ctx, ada_down, ada_up, ada_b, norm_mix, norm_ffn, ffn_w_gate_up, ffn_w_down, pool_w, pool_scale, s5_lam_re, s5_lam_im, s5_log_step, s5_b_re, s5_b_im, s5_c_re, s5_c_im, s5_d, s5_w_glu, swa_w_qkv, swa_w_o, swa_q_gain, swa_k_gain, swa_sinks, na_w_qkv, na_w_o, na_q_gain, na_k_gain, na_rpb):
    bsz, n_lat, d = x.shape
    n_ctx = ctx.shape[1]
    assert bsz == 1 and n_lat % ROW_TILE == 0 and n_ctx == ROW_TILE
    n_tot = n_lat + n_ctx
    depth = ada_down.shape[0]
    d_ff = ffn_w_down.shape[1]
    n_heads = d // HEAD_DIM

    mods = _adaln_all(c, c_ctx, ada_down, ada_up, ada_b)
    xs = None

    for layer in range(depth):
        kind = layer % 4
        last = layer == depth - 1
        rows = n_lat if last else n_tot
        mod = mods[layer]
        res = dict(rows=rows, n_lat=n_lat)

        if kind == 0:
            lat, ctx_src, ctx_block = (x[0], ctx[0], 0) if xs is None else (xs, xs, n_lat // ROW_TILE)
            xs, hx = _pool_layer(lat, ctx_src, ctx_block, norm_mix[layer], norm_ffn[layer], mod, pool_w, pool_scale,
                                 n_tot=n_tot, **res)
        else:
            u = _norm_mod(xs, norm_mix[layer], mod, shift_idx=0, rows=n_tot, n_lat=n_lat)
            if kind == 1:
                a_re, a_im, bb_re, bb_im = _s5_prep(s5_lam_re, s5_lam_im, s5_log_step, s5_b_re, s5_b_im)
                powers = _s5_expand(a_re, a_im, bb_re, bb_im, s5_c_re, s5_c_im)
                g = _s5_conv(u, *_s5_conv_weights(*powers), s5_d, n_lat=n_lat)
                half = s5_w_glu.shape[1] // 2
                tm = _pick(rows, MM_ROW_TILES)
                tn = _pick(half, (256, 128))
                xs = _mm(g, s5_w_glu, rows=rows, tm=tm, tn=tn, out_cols=half, out_dtype=F32,
                         col_maps=(lambda j: j, lambda j, o=half // tn: j + o),
                         epilogue=functools.partial(_epi_glu_residual, gate_idx=2, tm=tm, n_lat=n_lat),
                         extras=_residual_extras(xs, mod, tm, tn), w_outer=True, name="s5_glu")
            elif kind == 2:
                n_kv = n_heads // SWA_GROUP
                qkv = _qkv_proj(u, swa_w_qkv, swa_q_gain, swa_k_gain, n_q=n_heads, n_kv=n_kv,
                                rope_tables=_rope_tables(n_lat, n_ctx), rows=n_tot)
                o = _swa_attention(qkv, swa_sinks, n_q=n_heads, n_kv=n_kv, **res)
                xs = _mm_residual(o, swa_w_o, xs, mod, gate_idx=2, w_outer=True, name="swa_out", **res)
            else:
                qkv = _qkv_proj(u, na_w_qkv, na_q_gain, na_k_gain, n_q=n_heads, n_kv=n_heads,
                                rope_tables=None, rows=n_tot)
                o = _na_attention(qkv, _na_bias(na_rpb), n_heads=n_heads, **res)
                xs = _mm_residual(o, na_w_o, xs, mod, gate_idx=2, w_outer=True, name="na_out", **res)

            hx = _norm_mod(xs, norm_ffn[layer], mod, shift_idx=3, **res)
        tm = _pick(rows, MM_ROW_TILES)
        tn = _pick(d_ff, (256, 128))
        hh, w_down = _mm(hx, ffn_w_gate_up, layer=layer, rows=rows, tm=tm, tn=tn, out_cols=d_ff, out_dtype=BF16,
                         col_maps=(lambda j: j, lambda j, o=d_ff // tn: j + o), epilogue=_epi_swiglu,
                         w_outer=True, name="ffn_gate_up", to_bf16=ffn_w_down)
        xs = _mm_residual(hh, w_down, xs, mod, gate_idx=5, w_outer=False, name="ffn_down", **res)

    return xs[:n_lat][None]
```

```python
import functools

import jax
import jax.numpy as jnp
from jax import lax
from jax.experimental import pallas as pl
from jax.experimental.pallas import tpu as pltpu

F32 = jnp.float32
BF16 = jnp.bfloat16

RMS_EPS = 1e-6
NEG_INF = -1e30
HEAD_DIM = 128
GRID_W = 64
ROPE_BASE = 10000.0
POOL_WINDOWS = (2, 4, 8, 16)
POOL_HALO = 8
S5_GROUP = 16
S5_CHUNK = 256
S5_BLOCKS = 64
S5_ROW_PAD = 8
SWA_WINDOW = 128
SWA_GROUP = 4
NA_WIN_ROWS = 8
NA_WIN_COLS = 16
NA_QROWS = 4
SWA_KV_PER_STEP = 8
NA_HEADS_PER_STEP = 8

LANES = 128
ROW_TILE = 256
MM_ROW_TILES = (1408, 1024, 640, 256)
MM_ROW_TILES_RESIDUAL = (1056, 1024, 640, 256)
V7X_VMEM_LIMIT = 56 * 1024 * 1024


def _pick(n, candidates):
    for c in candidates:
        if n % c == 0:
            return c
    raise ValueError(f"no tile for {n} in {candidates}")


def _params(sem, vmem=V7X_VMEM_LIMIT):
    return pltpu.CompilerParams(dimension_semantics=sem, vmem_limit_bytes=vmem)


def _dot(a, b):
    return jnp.dot(a, b, preferred_element_type=F32)


def _dot_nt(a, b):
    return lax.dot_general(a, b, (((1,), (1,)), ((), ())), preferred_element_type=F32)


def _small_mm_kernel(a_ref, w_ref, b_ref, o_ref, *, silu_in, kc):
    a = a_ref[0]
    if silu_in:
        a = a * jax.nn.sigmoid(a)
    a = a.astype(BF16)
    k = a.shape[1]
    acc = jnp.zeros(o_ref.shape[1:], F32)
    for k0 in range(0, k, kc):
        acc += _dot(a[:, k0:k0 + kc], w_ref[0, k0:k0 + kc, :].astype(BF16))
    o_ref[0] = acc + b_ref[0]


def _small_mm(a, w, b, *, silu_in, tn):
    depth, k, n = w.shape
    shared = a.shape[0] == 1
    kern = functools.partial(_small_mm_kernel, silu_in=silu_in, kc=min(k, 512))
    return pl.pallas_call(
        kern,
        grid=(depth, n // tn),
        in_specs=[
            pl.BlockSpec((1, 8, k), lambda l, j: (0 if shared else l, 0, 0)),
            pl.BlockSpec((1, k, tn), lambda l, j: (l, 0, j)),
            pl.BlockSpec((1, 1, tn), lambda l, j: (l, 0, j)),
        ],
        out_specs=pl.BlockSpec((1, 8, tn), lambda l, j: (l, 0, j)),
        out_shape=jax.ShapeDtypeStruct((depth, 8, n), F32),
        compiler_params=_params(("arbitrary", "arbitrary")),
        name="adaln_mm",
    )(a, w, b)


def _adaln_all(c, c_ctx, ada_down, ada_up, ada_b):
    depth, d, rank = ada_down.shape
    cvec = jnp.zeros((1, 8, d), F32).at[0, 0].set(c[0]).at[0, 1].set(c_ctx)
    t = _small_mm(cvec, ada_down, jnp.zeros((depth, 1, rank), F32), silu_in=True, tn=_pick(rank, (512, 256, 128)))
    m = _small_mm(t, ada_up, ada_b.reshape(depth, 1, 6 * d), silu_in=False, tn=_pick(6 * d, (2048, 1024, 512)))
    return m[:, :2].reshape(depth, 2, 6, d)


def _rms_mod(x, gain, shift, scale):
    ms = jnp.mean(x * x, axis=-1, keepdims=True)
    return (x * lax.rsqrt(ms + RMS_EPS) * gain) * (1.0 + scale) + shift


def _norm_mod_kernel(x_ref, g_ref, mod_ref, o_ref, *, shift_idx):
    u = _rms_mod(x_ref[...], g_ref[...], mod_ref[0, shift_idx:shift_idx + 1, :],
                 mod_ref[0, shift_idx + 1:shift_idx + 2, :])
    o_ref[...] = u.astype(o_ref.dtype)


def _norm_mod(xs, gain, mod, *, shift_idx, rows, n_lat):
    d = xs.shape[1]
    tr = ROW_TILE
    return pl.pallas_call(
        functools.partial(_norm_mod_kernel, shift_idx=shift_idx),
        grid=(rows // tr,),
        in_specs=[
            pl.BlockSpec((tr, d), lambda i: (i, 0)),
            pl.BlockSpec((1, d), lambda i: (0, 0)),
            pl.BlockSpec((1, 6, d), lambda i: (jnp.where(i * tr >= n_lat, 1, 0), 0, 0)),
        ],
        out_specs=pl.BlockSpec((tr, d), lambda i: (i, 0)),
        out_shape=jax.ShapeDtypeStruct((rows, d), BF16),
        compiler_params=_params(("arbitrary",)),
        name="norm_mod",
    )(xs, gain.reshape(1, d), mod)


def _mm(a, w, *, rows, tm, tn, out_cols, out_dtype, col_maps, epilogue, extras=(), w_outer, name,
        layer=None, to_bf16=None):
    k = a.shape[1]
    n_i, n_j = rows // tm, out_cols // tn
    cast = w.dtype != BF16
    assert w_outer or not cast
    n_parts = len(col_maps)
    kc = _pick(k, (512, 256, 128))

    if w_outer:
        grid = (n_j, n_i)
        ij = lambda g0, g1: (g1, g0)
    else:
        grid = (n_i, n_j)
        ij = lambda g0, g1: (g0, g1)

    if w.ndim == 3:
        w_block, w_index = (None, k, tn), lambda col: (layer, 0, col)
    else:
        w_block, w_index = (k, tn), lambda col: (0, col)
    in_specs = [pl.BlockSpec((tm, k), lambda g0, g1: (ij(g0, g1)[0], 0))]
    for cm in col_maps:
        in_specs.append(pl.BlockSpec(w_block, lambda g0, g1, cm=cm: w_index(cm(ij(g0, g1)[1]))))
    for _, shape, fn in extras:
        in_specs.append(pl.BlockSpec(shape, lambda g0, g1, fn=fn: fn(*ij(g0, g1))))
    n_ex = len(extras)
    out_specs = [pl.BlockSpec((tm, tn), lambda g0, g1: ij(g0, g1))]
    out_shape = [jax.ShapeDtypeStruct((rows, out_cols), out_dtype)]
    n_side = 0 if to_bf16 is None else 1
    if n_side:
        _, side_r, side_c = to_bf16.shape
        side_steps = max(s for s in (1, 2, 4, 8, 16) if s <= grid[1] and side_r % (grid[0] * s * 16) == 0)
        slab = side_r // (grid[0] * side_steps)
        side_index = lambda g0, g1: g0 * side_steps + jnp.minimum(g1, side_steps - 1)
        in_specs.append(pl.BlockSpec((None, slab, side_c), lambda g0, g1: (layer, side_index(g0, g1), 0)))
        out_specs.append(pl.BlockSpec((slab, side_c), lambda g0, g1: (side_index(g0, g1), 0)))
        out_shape.append(jax.ShapeDtypeStruct((side_r, side_c), BF16))

    def body(*refs):
        a_ref = refs[0]
        w_refs = refs[1:1 + n_parts]
        ex_refs = refs[1 + n_parts:1 + n_parts + n_ex]
        n_in = 1 + n_parts + n_ex + n_side
        o_ref = refs[n_in]
        scratch = refs[n_in + 1 + n_side:]
        i, j = ij(pl.program_id(0), pl.program_id(1))
        if n_side:
            @pl.when(pl.program_id(1) < side_steps)
            def _():
                refs[n_in + 1][...] = refs[n_in - 1][...].astype(BF16)
        if cast:
            @pl.when(i == 0)
            def _():
                def cp(c, carry):
                    sl = pl.ds(pl.multiple_of(c * kc, kc), kc)
                    for p in range(n_parts):
                        scratch[p][sl, :] = w_refs[p][sl, :].astype(BF16)
                    return carry
                lax.fori_loop(0, k // kc, cp, 0)
            ws = scratch
        else:
            ws = w_refs
        av = a_ref[...]
        accs = [_dot(av, ws[p][...]) for p in range(n_parts)]
        epilogue(accs, i, j, ex_refs, o_ref)

    outs = pl.pallas_call(
        body,
        grid=grid,
        in_specs=in_specs,
        out_specs=out_specs,
        out_shape=out_shape,
        scratch_shapes=[pltpu.VMEM((k, tn), BF16) for _ in range(n_parts)] if cast else [],
        compiler_params=_params(("arbitrary", "arbitrary")),
        name=name,
    )(a, *([w] * n_parts), *[e[0] for e in extras], *([to_bf16] if n_side else []))
    return outs if n_side else outs[0]


def _row_gate(mod_ref, idx, i, tm, n_lat):
    rows = i * tm + lax.broadcasted_iota(jnp.int32, (tm, 1), 0)
    return jnp.where(rows < n_lat, mod_ref[0, idx:idx + 1, :], mod_ref[1, idx:idx + 1, :])


def _epi_swiglu(accs, i, j, ex, o_ref):
    a, g = accs
    o_ref[...] = (a * jax.nn.sigmoid(a) * g).astype(o_ref.dtype)


def _epi_residual(accs, i, j, ex, o_ref, *, gate_idx, tm, n_lat):
    res_ref, mod_ref = ex
    o_ref[...] = res_ref[...] + _row_gate(mod_ref, gate_idx, i, tm, n_lat) * accs[0]


def _epi_glu_residual(accs, i, j, ex, o_ref, *, gate_idx, tm, n_lat):
    res_ref, mod_ref = ex
    a, g = accs
    o_ref[...] = res_ref[...] + _row_gate(mod_ref, gate_idx, i, tm, n_lat) * (a * jax.nn.sigmoid(g))


def _residual_extras(xs, mod, tm, tn):
    return ((xs, (tm, tn), lambda i, j: (i, j)), (mod, (2, 6, tn), lambda i, j: (0, 0, j)))


def _mm_residual(a, w, xs, mod, *, gate_idx, rows, n_lat, w_outer, name):
    n = w.shape[1]
    tm = _pick(rows, MM_ROW_TILES_RESIDUAL if w_outer else (768, 512, 640, 256))
    tn = _pick(n, (512, 256) if w_outer else (256, 128))
    return _mm(a, w, rows=rows, tm=tm, tn=tn, out_cols=n, out_dtype=F32, col_maps=(lambda j: j,),
               epilogue=functools.partial(_epi_residual, gate_idx=gate_idx, tm=tm, n_lat=n_lat),
               extras=_residual_extras(xs, mod, tm, tn), w_outer=w_outer, name=name)


def _pool_kernel(xp_ref, x_ref, xn_ref, c_ref, g_ref, gf_ref, mod_ref, pw_ref, ps_ref, o_ref, h_ref, *, tr, n_lat, n_tot):
    i = pl.program_id(0)
    halo = POOL_HALO
    gain = g_ref[...]
    shift, scale, gate = mod_ref[0, 0:1, :], mod_ref[0, 1:2, :], mod_ref[0, 2:3, :]
    is_lat = i * tr < n_lat
    lat_rows = lax.broadcasted_iota(jnp.int32, (tr, 1), 0) * 0 + jnp.where(is_lat, 1, 0) > 0
    x = jnp.where(lat_rows, x_ref[...], c_ref[...])
    ue = jnp.concatenate([_rms_mod(xp_ref[...], gain, shift, scale), _rms_mod(x, gain, shift, scale),
                          _rms_mod(xn_ref[...], gain, shift, scale)], axis=0)
    seq_lo = jnp.where(is_lat, 0, n_lat)
    seq_hi = jnp.where(is_lat, n_lat, n_tot)
    t_g = i * tr + lax.broadcasted_iota(jnp.int32, (tr, 1), 0)
    s_g = i * tr - halo + lax.broadcasted_iota(jnp.int32, (1, tr + 2 * halo), 1)
    s_ok = (s_g >= seq_lo) & (s_g < seq_hi)
    pg = ue.shape[1] // len(POOL_WINDOWS)
    for gi, w in enumerate(POOL_WINDOWS):
        cols = slice(gi * pg, (gi + 1) * pg)
        ug = ue[:, cols]
        band = (s_g >= t_g - w // 2) & (s_g < t_g + w // 2) & s_ok
        ssum = _dot(jnp.where(band, 1.0, 0.0).astype(BF16), ug.astype(BF16))
        cnt = jnp.minimum(t_g + w // 2, seq_hi) - jnp.maximum(t_g - w // 2, seq_lo)
        p = ssum / cnt.astype(F32) - ug[halo:halo + tr]
        y = _dot(p.astype(BF16), pw_ref[gi]) * ps_ref[:, cols]
        o_ref[:, cols] = x[:, cols] + gate[:, cols] * y
    h_ref[...] = _rms_mod(o_ref[...], gf_ref[...], mod_ref[0, 3:4, :], mod_ref[0, 4:5, :]).astype(h_ref.dtype)


def _pool_layer(lat, ctx_src, ctx_block, gain, gain_ffn, mod, pool_w, pool_scale, *, rows, n_lat, n_tot):
    d = lat.shape[1]
    tr, halo = ROW_TILE, POOL_HALO
    hb = tr // halo
    last_hb = n_lat // halo - 1
    vec = pl.BlockSpec((1, d), lambda i: (0, 0))
    tile = pl.BlockSpec((tr, d), lambda i: (i, 0))
    return pl.pallas_call(
        functools.partial(_pool_kernel, tr=tr, n_lat=n_lat, n_tot=n_tot),
        grid=(rows // tr,),
        in_specs=[
            pl.BlockSpec((halo, d), lambda i: (jnp.clip(i * hb - 1, 0, last_hb), 0)),
            pl.BlockSpec((tr, d), lambda i: (jnp.minimum(i, n_lat // tr - 1), 0)),
            pl.BlockSpec((halo, d), lambda i: (jnp.minimum((i + 1) * hb, last_hb), 0)),
            pl.BlockSpec((tr, d), lambda i: (ctx_block, 0)),
            vec, vec,
            pl.BlockSpec((1, 6, d), lambda i: (jnp.where(i * tr >= n_lat, 1, 0), 0, 0)),
            pl.BlockSpec(pool_w.shape, lambda i: (0, 0, 0)),
            vec,
        ],
        out_specs=[tile, tile],
        out_shape=[jax.ShapeDtypeStruct((rows, d), F32), jax.ShapeDtypeStruct((rows, d), BF16)],
        compiler_params=_params(("arbitrary",)),
        name="pool_mixer",
    )(lat, lat, lat, ctx_src, gain.reshape(1, d), gain_ffn.reshape(1, d), mod, pool_w.astype(BF16),
      pool_scale.reshape(1, d))


def _s5_prep_kernel(lre_ref, lim_ref, ls_ref, bre_ref, bim_ref, are_ref, aim_ref, bbre_ref, bbim_ref):
    lam_re, lam_im = lre_ref[0], lim_ref[0]
    dt = jnp.exp(ls_ref[0])
    z_re, z_im = lam_re * dt, lam_im * dt
    mag = jnp.exp(z_re)
    a_re, a_im = mag * jnp.cos(z_im), mag * jnp.sin(z_im)
    den = lam_re * lam_re + lam_im * lam_im
    k_re = ((a_re - 1.0) * lam_re + a_im * lam_im) / den
    k_im = (a_im * lam_re - (a_re - 1.0) * lam_im) / den
    are_ref[0] = a_re
    aim_ref[0] = a_im
    b_re, b_im = bre_ref[0], bim_ref[0]
    bbre_ref[0] = k_re[:, None, :] * b_re - k_im[:, None, :] * b_im
    bbim_ref[0] = k_re[:, None, :] * b_im + k_im[:, None, :] * b_re


def _s5_prep(lam_re, lam_im, log_step, b_re, b_im):
    _, g, p, h = b_re.shape
    spec2 = pl.BlockSpec((1, g, p), lambda d: (d, 0, 0))
    spec3 = pl.BlockSpec((1, g, h, p), lambda d: (d, 0, 0, 0))
    return pl.pallas_call(
        _s5_prep_kernel,
        grid=(2,),
        in_specs=[spec2, spec2, pl.BlockSpec((1, g, 1), lambda d: (d, 0, 0)), spec3, spec3],
        out_specs=[spec2, spec2, spec3, spec3],
        out_shape=[jax.ShapeDtypeStruct((2, g, p), F32)] * 2 + [jax.ShapeDtypeStruct((2, g, h, p), F32)] * 2,
        compiler_params=_params(("arbitrary",)),
        name="s5_prep",
    )(lam_re, lam_im, log_step[..., None], jnp.swapaxes(b_re, 2, 3), jnp.swapaxes(b_im, 2, 3))


def _s5_pack(a_re, a_im, bb_re, bb_im, c_re, c_im):
    _, g, h, p = bb_re.shape
    r = g // 2
    eye2 = jnp.eye(2, dtype=F32)
    eye4 = jnp.eye(4, dtype=F32)
    slot = eye4[jnp.arange(r) % 4]

    def win_part(bb):
        t = bb.reshape(2, r, 2, h, p)
        t = t[:, :, :, :, None, :] * eye2[None, None, :, None, :, None]
        t = t.reshape(2, r, 2 * h, 2 * p)
        t = slot[None, :, :, None, None] * t[:, :, None, :, :]
        return t.reshape(2, r, 4 * 2 * h, 2 * p)

    def c_part(c):
        t = jnp.swapaxes(c, 2, 3).reshape(2, r, 2, p, h)
        t = t[:, :, :, :, None, :] * eye2[None, None, :, None, :, None]
        t = t.reshape(2, r, 2 * p, 2 * h)
        t = t[:, :, :, None, :] * slot[None, :, None, :, None]
        return t.reshape(2, r, 2 * p, 4 * 2 * h)

    win = jnp.concatenate([win_part(bb_re), win_part(bb_im)], axis=-1).astype(BF16)
    cmat = jnp.concatenate([c_part(c_re), c_part(-c_im)], axis=2).astype(BF16)
    return win, cmat, a_re.reshape(2, r, 2 * p), a_im.reshape(2, r, 2 * p)


def _s5_scan_kernel(u_ref, win_ref, cmat_ref, are_ref, aim_ref, y_ref, xre, xim, hre, him, *, nsub, rb):
    g0, c = pl.program_id(0), pl.program_id(1)
    rev = g0 // nsub == 1
    tc, pitch = S5_CHUNK, rb + S5_ROW_PAD
    block_rows = lambda r: pl.ds(r, tc, stride=pitch)

    @pl.when(c == 0)
    def _():
        hre[...] = jnp.zeros_like(hre)
        him[...] = jnp.zeros_like(him)

    for q in range(rb // 4):
        ut = u_ref[:, q * LANES:(q + 1) * LANES]
        for s in range(4):
            r = q * 4 + s
            x = _dot(ut, win_ref[0, r])
            xre[block_rows(r), :] = x[:, :LANES]
            xim[block_rows(r), :] = x[:, LANES:]

    a_re, a_im = are_ref[0], aim_ref[0]

    def step(t, carry):
        h_re, h_im = carry
        tt = jnp.where(rev, tc - 1 - t, t)
        rows = pl.ds(pl.multiple_of(tt * pitch, 8), rb)
        n_re = a_re * h_re - a_im * h_im + xre[rows, :]
        n_im = a_re * h_im + a_im * h_re + xim[rows, :]
        xre[rows, :] = n_re
        xim[rows, :] = n_im
        return n_re, n_im

    h_re, h_im = lax.fori_loop(0, tc, step, (hre[...], him[...]), unroll=8)
    hre[...] = h_re
    him[...] = h_im

    for q in range(rb // 4):
        acc = jnp.zeros((tc, LANES), F32)
        for s in range(4):
            r = q * 4 + s
            acc += (_dot(xre[block_rows(r), :].astype(BF16), cmat_ref[0, r, :LANES, :])
                    + _dot(xim[block_rows(r), :].astype(BF16), cmat_ref[0, r, LANES:, :]))
        y_ref[0, :, q * LANES:(q + 1) * LANES] = acc.astype(y_ref.dtype)


def _s5_scan(u, win, cmat, a_re, a_im, *, n_lat):
    n_tot, d = u.shape
    r_all = win.shape[1]
    rb = min(S5_BLOCKS, r_all)
    nsub = r_all // rb
    tc = S5_CHUNK
    n_c, n_lc = n_tot // tc, n_lat // tc
    cols = rb * 2 * S5_GROUP

    def split(w):
        return w.reshape((2 * nsub, rb) + w.shape[2:])

    def chunk(g0, c):
        return jnp.where(g0 // nsub == 0, (c + n_lc) % n_c, n_c - 1 - c)

    wspec = lambda shape: pl.BlockSpec((1,) + shape, lambda g0, c: (g0,) + (0,) * len(shape))
    return pl.pallas_call(
        functools.partial(_s5_scan_kernel, nsub=nsub, rb=rb),
        grid=(2 * nsub, n_c),
        in_specs=[
            pl.BlockSpec((tc, cols), lambda g0, c: (chunk(g0, c), g0 % nsub)),
            wspec((rb, LANES, 2 * LANES)), wspec((rb, 2 * LANES, LANES)), wspec((rb, LANES)), wspec((rb, LANES)),
        ],
        out_specs=pl.BlockSpec((1, tc, cols), lambda g0, c: (g0 // nsub, chunk(g0, c), g0 % nsub)),
        out_shape=jax.ShapeDtypeStruct((2, n_tot, d), BF16),
        scratch_shapes=[pltpu.VMEM((tc * (rb + S5_ROW_PAD), LANES), F32)] * 2 + [pltpu.VMEM((rb, LANES), F32)] * 2,
        compiler_params=_params(("arbitrary", "arbitrary")),
        name="s5_scan",
    )(u, split(win), split(cmat), split(a_re), split(a_im))


def _s5_gelu_kernel(y_ref, u_ref, d_ref, o_ref):
    v = y_ref[0].astype(F32) + y_ref[1].astype(F32) + d_ref[...] * u_ref[...].astype(F32)
    o_ref[...] = jax.nn.gelu(v, approximate=True).astype(o_ref.dtype)


def _s5_gelu(y, u, dvec, *, rows):
    d = u.shape[1]
    tr = ROW_TILE
    return pl.pallas_call(
        _s5_gelu_kernel,
        grid=(rows // tr,),
        in_specs=[pl.BlockSpec((2, tr, d), lambda i: (0, i, 0)), pl.BlockSpec((tr, d), lambda i: (i, 0)),
                  pl.BlockSpec((1, d), lambda i: (0, 0))],
        out_specs=pl.BlockSpec((tr, d), lambda i: (i, 0)),
        out_shape=jax.ShapeDtypeStruct((rows, d), BF16),
        compiler_params=_params(("arbitrary",)),
        name="s5_gelu",
    )(y, u, dvec.reshape(1, d))


def _rope_tables(n_lat, n_ctx):
    quarter = HEAD_DIM // 4
    inv = jnp.power(ROPE_BASE, -jnp.arange(quarter, dtype=F32) / quarter)
    t = jnp.arange(n_lat)
    ang_r = (t // GRID_W).astype(F32)[:, None] * inv
    ang_c = (t % GRID_W).astype(F32)[:, None] * inv
    ang = jnp.concatenate([ang_r, ang_r, ang_c, ang_c], axis=-1)
    sign = jnp.where((jnp.arange(HEAD_DIM) % (2 * quarter)) < quarter, -1.0, 1.0)
    cos = jnp.concatenate([jnp.cos(ang), jnp.ones((n_ctx, HEAD_DIM), F32)], axis=0)
    sin = jnp.concatenate([jnp.sin(ang) * sign, jnp.zeros((n_ctx, HEAD_DIM), F32)], axis=0)
    return cos, sin


def _epi_qkv(accs, i, j, ex, o_ref, *, n_norm_tiles, rope, tn):
    acc = accs[0]
    gain_ref = ex[0]

    @pl.when(j < n_norm_tiles)
    def _():
        quarter = HEAD_DIM // 4
        lane = lax.broadcasted_iota(jnp.int32, (1, HEAD_DIM), 1)
        first = (lane % (2 * quarter)) < quarter
        for h in range(tn // HEAD_DIM):
            cols = slice(h * HEAD_DIM, (h + 1) * HEAD_DIM)
            x = acc[:, cols]
            ms = jnp.mean(x * x, axis=-1, keepdims=True)
            xn = x * lax.rsqrt(ms + RMS_EPS) * gain_ref[:, cols]
            if rope:
                cos_ref, sin_ref = ex[1], ex[2]
                partner = jnp.where(first, pltpu.roll(xn, HEAD_DIM - quarter, 1), pltpu.roll(xn, quarter, 1))
                xn = xn * cos_ref[...] + partner * sin_ref[...]
            o_ref[:, cols] = xn.astype(o_ref.dtype)

    @pl.when(j >= n_norm_tiles)
    def _():
        o_ref[...] = acc.astype(o_ref.dtype)


def _qkv_proj(u, w_qkv, q_gain, k_gain, *, n_q, n_kv, rope_tables, rows):
    n = w_qkv.shape[1]
    tm = _pick(rows, MM_ROW_TILES)
    tn = _pick(n_kv * HEAD_DIM, (512, 256, 128))
    gain = jnp.concatenate([jnp.tile(q_gain * HEAD_DIM ** -0.5, n_q), jnp.tile(k_gain, n_kv),
                            jnp.ones((n_kv * HEAD_DIM,), F32)]).reshape(1, n)
    extras = [(gain, (1, tn), lambda i, j: (0, j))]
    if rope_tables is not None:
        extras += [(t, (tm, HEAD_DIM), lambda i, j: (i, 0)) for t in rope_tables]
    epi = functools.partial(_epi_qkv, n_norm_tiles=(n_q + n_kv) * HEAD_DIM // tn, rope=rope_tables is not None, tn=tn)
    return _mm(u, w_qkv, rows=rows, tm=tm, tn=tn, out_cols=n, out_dtype=BF16, col_maps=(lambda j: j,),
               epilogue=epi, extras=tuple(extras), w_outer=True, name="qkv_proj")


def _softmax_pv(tiles, values, extra_logit=None):
    rows = tiles[0].shape[0]
    mx = functools.reduce(jnp.maximum, tiles)
    if extra_logit is not None:
        mx = jnp.maximum(mx, extra_logit)
    m = jnp.broadcast_to(jnp.max(mx, axis=-1, keepdims=True), (rows, LANES))
    p = jnp.concatenate([jnp.exp(t - m).astype(BF16) for t in tiles], axis=1)
    ones = jnp.ones((LANES, LANES), BF16)
    v_aug = jnp.concatenate([jnp.concatenate([v, ones], axis=1) for v in values], axis=0)
    acc = _dot(p, v_aug)
    den = acc[:, LANES:]
    if extra_logit is not None:
        den = den + jnp.exp(extra_logit - m)
    return acc[:, :LANES] / den


def _swa_kernel(sink_ref, q_ref, kp_ref, kc_ref, kn_ref, vp_ref, vc_ref, vn_ref, kx_ref, vx_ref, o_ref, *, n_lat_blocks):
    h, b = pl.program_id(0), pl.program_id(1)
    blk = SWA_WINDOW
    is_lat = b < n_lat_blocks
    qi = lax.broadcasted_iota(jnp.int32, (SWA_GROUP * blk, 1), 0) & (blk - 1)
    kj = lax.broadcasted_iota(jnp.int32, (1, blk), 1)
    off = lambda ok: jnp.where(ok, 0, blk + 1)
    m_prev = kj >= qi + off(is_lat & (b > 0))
    m_cur = kj >= qi * 0 + off(is_lat)
    m_next = kj <= qi - off(is_lat & (b < n_lat_blocks - 1))
    n_ctx_tiles = kx_ref.shape[0] // LANES
    hp = kp_ref.shape[1] // HEAD_DIM
    for hh in range(hp):
        kcols = slice(hh * HEAD_DIM, (hh + 1) * HEAD_DIM)
        qh = lambda g: (hh * SWA_GROUP + g) * HEAD_DIM
        q = jnp.concatenate([q_ref[:, qh(g):qh(g) + HEAD_DIM] for g in range(SWA_GROUP)], axis=0)
        sink = jnp.concatenate([jnp.full((blk, LANES), sink_ref[(h * hp + hh) * SWA_GROUP + g], F32)
                                for g in range(SWA_GROUP)], axis=0)
        s_ctx = _dot_nt(q, kx_ref[:, kcols])
        tiles = [jnp.where(m_prev, _dot_nt(q, kp_ref[:, kcols]), NEG_INF),
                 jnp.where(m_cur, _dot_nt(q, kc_ref[:, kcols]), NEG_INF),
                 jnp.where(m_next, _dot_nt(q, kn_ref[:, kcols]), NEG_INF)]
        tiles += [s_ctx[:, t * LANES:(t + 1) * LANES] for t in range(n_ctx_tiles)]
        vals = [vp_ref[:, kcols], vc_ref[:, kcols], vn_ref[:, kcols]]
        vals += [vx_ref[t * LANES:(t + 1) * LANES, kcols] for t in range(n_ctx_tiles)]
        o = _softmax_pv(tiles, vals, sink).astype(o_ref.dtype)
        for g in range(SWA_GROUP):
            o_ref[:, qh(g):qh(g) + HEAD_DIM] = o[g * blk:(g + 1) * blk]


def _swa_attention(qkv, sinks, *, n_q, n_kv, rows, n_lat):
    blk = SWA_WINDOW
    nlb = n_lat // blk
    ctx_rows = (qkv.shape[0] - n_lat)
    hp = min(SWA_KV_PER_STEP, n_kv)
    assert n_kv % hp == 0 and n_q % hp == 0
    kcol = lambda h: n_q // hp + h
    vcol = lambda h: (n_q + n_kv) // hp + h
    prev = lambda b: jnp.clip(b - 1, 0, nlb - 1)
    cur = lambda b: jnp.minimum(b, nlb - 1)
    nxt = lambda b: jnp.clip(b + 1, 0, nlb - 1)
    kv = lambda rowf, colf: pl.BlockSpec((blk, hp * HEAD_DIM), lambda h, b: (rowf(b), colf(h)))
    ctx = lambda colf: pl.BlockSpec((ctx_rows, hp * HEAD_DIM), lambda h, b: (n_lat // ctx_rows, colf(h)))
    return pl.pallas_call(
        functools.partial(_swa_kernel, n_lat_blocks=nlb),
        grid=(n_kv // hp, rows // blk),
        in_specs=[
            pl.BlockSpec(memory_space=pltpu.SMEM),
            pl.BlockSpec((blk, hp * SWA_GROUP * HEAD_DIM), lambda h, b: (b, h)),
            kv(prev, kcol), kv(cur, kcol), kv(nxt, kcol), kv(prev, vcol), kv(cur, vcol), kv(nxt, vcol),
            ctx(kcol), ctx(vcol),
        ],
        out_specs=pl.BlockSpec((blk, hp * SWA_GROUP * HEAD_DIM), lambda h, b: (b, h)),
        out_shape=jax.ShapeDtypeStruct((rows, n_q * HEAD_DIM), BF16),
        compiler_params=_params(("arbitrary", "arbitrary")),
        name="swa_attention",
    )(sinks, *([qkv] * 9))


def _na_bias_kernel(rpb_ref, o_ref):
    h = pl.program_id(0)
    n_ri, n_ci = 2 * NA_WIN_ROWS - 1, 2 * NA_WIN_COLS - 1
    cq = lax.broadcasted_iota(jnp.int32, (GRID_W, 2 * GRID_W), 0)
    ck = lax.broadcasted_iota(jnp.int32, (GRID_W, 2 * GRID_W), 1)
    second = ck >= GRID_W
    ck = jnp.where(second, ck - GRID_W, ck)
    c0 = jnp.clip(cq - NA_WIN_COLS // 2, 0, GRID_W - NA_WIN_COLS)
    in_win = (ck >= c0) & (ck < c0 + NA_WIN_COLS)
    cidx = jnp.clip(ck - cq + NA_WIN_COLS - 1, 0, n_ci - 1)
    neg = jnp.full((GRID_W, 2 * GRID_W), NEG_INF, F32)
    per_offset = []
    for ri in range(n_ri):
        t = neg
        for dci in range(n_ci):
            t = jnp.where(cidx == dci, rpb_ref[(h * n_ri + ri) * n_ci + dci], t)
        per_offset.append(jnp.where(in_win, t, NEG_INF))
    for k in range(n_ri + 1):
        o_ref[0, k] = jnp.where(second, per_offset[k] if k < n_ri else neg, per_offset[k - 1] if k >= 1 else neg)


def _na_bias(rpb):
    n_heads = rpb.shape[0]
    n_tiles = 2 * NA_WIN_ROWS
    return pl.pallas_call(
        _na_bias_kernel,
        grid=(n_heads,),
        in_specs=[pl.BlockSpec(memory_space=pltpu.SMEM)],
        out_specs=pl.BlockSpec((1, n_tiles, GRID_W, 2 * GRID_W), lambda h: (h, 0, 0, 0)),
        out_shape=jax.ShapeDtypeStruct((n_heads, n_tiles, GRID_W, 2 * GRID_W), F32),
        compiler_params=_params(("arbitrary",)),
        name="na_bias",
    )(rpb.reshape(-1))


def _na_kernel(q_ref, kp_ref, kc_ref, kn_ref, vp_ref, vc_ref, vn_ref, kx_ref, vx_ref, bias_ref, o_ref, *, n_grid_rows):
    b = pl.program_id(1)
    qr, w = NA_QROWS, GRID_W
    is_lat = b * qr < n_grid_rows
    lane = lax.broadcasted_iota(jnp.int32, (1, 2 * w), 1)
    n_kt = qr // 2
    windows = {}
    for kb in range(3):
        for dq in range(qr):
            r = b * qr + dq
            r0 = jnp.clip(r - NA_WIN_ROWS // 2, 0, n_grid_rows - NA_WIN_ROWS)
            for jt in range(n_kt):
                kr = b * qr + (kb - 1) * qr + 2 * jt
                ok0 = (kr >= r0) & (kr < r0 + NA_WIN_ROWS) & is_lat
                ok1 = (kr + 1 >= r0) & (kr + 1 < r0 + NA_WIN_ROWS) & is_lat
                lo, hi = jnp.where(ok0, 0, w), jnp.where(ok1, 2 * w, w)
                windows[kb, dq, jt] = (lane >= lo) & (lane < hi)
    for hh in range(q_ref.shape[1] // HEAD_DIM):
        cols = slice(hh * HEAD_DIM, (hh + 1) * HEAD_DIM)
        q = q_ref[:, cols]
        tiles, vals = [], []
        for kb, (k_ref, v_ref) in enumerate(((kp_ref, vp_ref), (kc_ref, vc_ref), (kn_ref, vn_ref))):
            s = _dot_nt(q, k_ref[:, cols])
            for jt in range(n_kt):
                parts = []
                for dq in range(qr):
                    ri0 = (kb - 1) * qr + 2 * jt - dq + NA_WIN_ROWS - 1
                    t = s[dq * w:(dq + 1) * w, jt * 2 * w:(jt + 1) * 2 * w] + bias_ref[hh, ri0 + 1]
                    parts.append(jnp.where(windows[kb, dq, jt], t, NEG_INF))
                tiles.append(jnp.concatenate(parts, axis=0))
                vals.append(v_ref[jt * LANES:(jt + 1) * LANES, cols])
        s_ctx = _dot_nt(q, kx_ref[:, cols])
        for t in range(kx_ref.shape[0] // LANES):
            tiles.append(s_ctx[:, t * LANES:(t + 1) * LANES])
            vals.append(vx_ref[t * LANES:(t + 1) * LANES, cols])
        o_ref[:, cols] = _softmax_pv(tiles, vals).astype(o_ref.dtype)


def _na_attention(qkv, bias, *, n_heads, rows, n_lat):
    blk = NA_QROWS * GRID_W
    assert blk == qkv.shape[0] - n_lat
    nlb = n_lat // blk
    hp = min(NA_HEADS_PER_STEP, n_heads)
    assert n_heads % hp == 0
    kcol = lambda h: n_heads // hp + h
    vcol = lambda h: 2 * n_heads // hp + h
    prev = lambda b: jnp.clip(b - 1, 0, nlb - 1)
    cur = lambda b: jnp.minimum(b, nlb - 1)
    nxt = lambda b: jnp.clip(b + 1, 0, nlb - 1)
    spec = lambda rowf, colf: pl.BlockSpec((blk, hp * HEAD_DIM), lambda h, b: (rowf(b), colf(h)))
    ctx = lambda b: nlb
    return pl.pallas_call(
        functools.partial(_na_kernel, n_grid_rows=n_lat // GRID_W),
        grid=(n_heads // hp, rows // blk),
        in_specs=[
            spec(lambda b: b, lambda h: h),
            spec(prev, kcol), spec(cur, kcol), spec(nxt, kcol), spec(prev, vcol), spec(cur, vcol), spec(nxt, vcol),
            spec(ctx, kcol), spec(ctx, vcol),
            pl.BlockSpec((hp,) + bias.shape[1:], lambda h, b: (h, 0, 0, 0)),
        ],
        out_specs=spec(lambda b: b, lambda h: h),
        out_shape=jax.ShapeDtypeStruct((rows, n_heads * HEAD_DIM), BF16),
        compiler_params=_params(("arbitrary", "arbitrary")),
        name="na_attention",
    )(*([qkv] * 9), bias)


def kernel(x, c, ctx, c_ctx, ada_down, ada_up, ada_b, norm_mix, norm_ffn, ffn_w_gate_up, ffn_w_down, pool_w, pool_scale, s5_lam_re, s5_lam_im, s5_log_step, s5_b_re, s5_b_im, s5_c_re, s5_c_im, s5_d, s5_w_glu, swa_w_qkv, swa_w_o, swa_q_gain, swa_k_gain, swa_sinks, na_w_qkv, na_w_o, na_q_gain, na_k_gain, na_rpb):
    bsz, n_lat, d = x.shape
    n_ctx = ctx.shape[1]
    assert bsz == 1 and n_lat % ROW_TILE == 0 and n_ctx == ROW_TILE
    n_tot = n_lat + n_ctx
    depth = ada_down.shape[0]
    d_ff = ffn_w_down.shape[1]
    n_heads = d // HEAD_DIM

    mods = _adaln_all(c, c_ctx, ada_down, ada_up, ada_b)
    xs = None

    for layer in range(depth):
        kind = layer % 4
        last = layer == depth - 1
        rows = n_lat if last else n_tot
        mod = mods[layer]
        res = dict(rows=rows, n_lat=n_lat)

        if kind == 0:
            lat, ctx_src, ctx_block = (x[0], ctx[0], 0) if xs is None else (xs, xs, n_lat // ROW_TILE)
            xs, hx = _pool_layer(lat, ctx_src, ctx_block, norm_mix[layer], norm_ffn[layer], mod, pool_w, pool_scale,
                                 n_tot=n_tot, **res)
        else:
            u = _norm_mod(xs, norm_mix[layer], mod, shift_idx=0, rows=n_tot, n_lat=n_lat)
            if kind == 1:
                a_re, a_im, bb_re, bb_im = _s5_prep(s5_lam_re, s5_lam_im, s5_log_step, s5_b_re, s5_b_im)
                y = _s5_scan(u, *_s5_pack(a_re, a_im, bb_re, bb_im, s5_c_re, s5_c_im), n_lat=n_lat)
                g = _s5_gelu(y, u, s5_d, rows=rows)
                half = s5_w_glu.shape[1] // 2
                tm = _pick(rows, MM_ROW_TILES)
                tn = _pick(half, (256, 128))
                xs = _mm(g, s5_w_glu, rows=rows, tm=tm, tn=tn, out_cols=half, out_dtype=F32,
                         col_maps=(lambda j: j, lambda j, o=half // tn: j + o),
                         epilogue=functools.partial(_epi_glu_residual, gate_idx=2, tm=tm, n_lat=n_lat),
                         extras=_residual_extras(xs, mod, tm, tn), w_outer=True, name="s5_glu")
            elif kind == 2:
                n_kv = n_heads // SWA_GROUP
                qkv = _qkv_proj(u, swa_w_qkv, swa_q_gain, swa_k_gain, n_q=n_heads, n_kv=n_kv,
                                rope_tables=_rope_tables(n_lat, n_ctx), rows=n_tot)
                o = _swa_attention(qkv, swa_sinks, n_q=n_heads, n_kv=n_kv, **res)
                xs = _mm_residual(o, swa_w_o, xs, mod, gate_idx=2, w_outer=True, name="swa_out", **res)
            else:
                qkv = _qkv_proj(u, na_w_qkv, na_q_gain, na_k_gain, n_q=n_heads, n_kv=n_heads,
                                rope_tables=None, rows=n_tot)
                o = _na_attention(qkv, _na_bias(na_rpb), n_heads=n_heads, **res)
                xs = _mm_residual(o, na_w_o, xs, mod, gate_idx=2, w_outer=True, name="na_out", **res)

            hx = _norm_mod(xs, norm_ffn[layer], mod, shift_idx=3, **res)
        tm = _pick(rows, MM_ROW_TILES)
        tn = _pick(d_ff, (256, 128))
        hh, w_down = _mm(hx, ffn_w_gate_up, layer=layer, rows=rows, tm=tm, tn=tn, out_cols=d_ff, out_dtype=BF16,
                         col_maps=(lambda j: j, lambda j, o=d_ff // tn: j + o), epilogue=_epi_swiglu,
                         w_outer=True, name="ffn_gate_up", to_bf16=ffn_w_down)
        xs = _mm_residual(hh, w_down, xs, mod, gate_idx=5, w_outer=False, name="ffn_down", **res)

    return xs[:n_lat][None]
```

```python
import functools

import jax
import jax.numpy as jnp
from jax import lax
from jax.experimental import pallas as pl
from jax.experimental.pallas import tpu as pltpu

F32 = jnp.float32
BF16 = jnp.bfloat16

RMS_EPS = 1e-6
NEG_INF = -1e30
HEAD_DIM = 128
GRID_W = 64
ROPE_BASE = 10000.0
POOL_WINDOWS = (2, 4, 8, 16)
POOL_HALO = 8
S5_GROUP = 16
S5_CHUNK = 256
S5_BLOCKS = 64
S5_ROW_PAD = 8
SWA_WINDOW = 128
SWA_GROUP = 4
NA_WIN_ROWS = 8
NA_WIN_COLS = 16
NA_QROWS = 4
SWA_KV_PER_STEP = 8
NA_HEADS_PER_STEP = 8

LANES = 128
ROW_TILE = 256
MM_ROW_TILES = (1408, 1024, 640, 256)
MM_ROW_TILES_RESIDUAL = (1056, 1024, 640, 256)
V7X_VMEM_LIMIT = 63 * 1024 * 1024


def _pick(n, candidates):
    for c in candidates:
        if n % c == 0:
            return c
    raise ValueError(f"no tile for {n} in {candidates}")


def _params(sem, vmem=V7X_VMEM_LIMIT):
    return pltpu.CompilerParams(dimension_semantics=sem, vmem_limit_bytes=vmem)


def _dot(a, b):
    return jnp.dot(a, b, preferred_element_type=F32)


def _dot_nt(a, b):
    return lax.dot_general(a, b, (((1,), (1,)), ((), ())), preferred_element_type=F32)


def _small_mm_kernel(a_ref, w_ref, b_ref, o_ref, *, silu_in, kc):
    a = a_ref[0]
    if silu_in:
        a = a * jax.nn.sigmoid(a)
    a = a.astype(BF16)
    k = a.shape[1]
    acc = jnp.zeros(o_ref.shape[1:], F32)
    for k0 in range(0, k, kc):
        acc += _dot(a[:, k0:k0 + kc], w_ref[0, k0:k0 + kc, :].astype(BF16))
    o_ref[0] = acc + b_ref[0]


def _small_mm(a, w, b, *, silu_in, tn):
    depth, k, n = w.shape
    shared = a.shape[0] == 1
    kern = functools.partial(_small_mm_kernel, silu_in=silu_in, kc=min(k, 512))
    return pl.pallas_call(
        kern,
        grid=(depth, n // tn),
        in_specs=[
            pl.BlockSpec((1, 8, k), lambda l, j: (0 if shared else l, 0, 0)),
            pl.BlockSpec((1, k, tn), lambda l, j: (l, 0, j)),
            pl.BlockSpec((1, 1, tn), lambda l, j: (l, 0, j)),
        ],
        out_specs=pl.BlockSpec((1, 8, tn), lambda l, j: (l, 0, j)),
        out_shape=jax.ShapeDtypeStruct((depth, 8, n), F32),
        compiler_params=_params(("arbitrary", "arbitrary")),
        name="adaln_mm",
    )(a, w, b)


def _adaln_all(c, c_ctx, ada_down, ada_up, ada_b):
    depth, d, rank = ada_down.shape
    cvec = jnp.zeros((1, 8, d), F32).at[0, 0].set(c[0]).at[0, 1].set(c_ctx)
    t = _small_mm(cvec, ada_down, jnp.zeros((depth, 1, rank), F32), silu_in=True, tn=_pick(rank, (512, 256, 128)))
    m = _small_mm(t, ada_up, ada_b.reshape(depth, 1, 6 * d), silu_in=False, tn=_pick(6 * d, (2048, 1024, 512)))
    return m[:, :2].reshape(depth, 2, 6, d)


def _rms_mod(x, gain, shift, scale):
    ms = jnp.mean(x * x, axis=-1, keepdims=True)
    return (x * lax.rsqrt(ms + RMS_EPS) * gain) * (1.0 + scale) + shift


def _norm_mod_kernel(x_ref, g_ref, mod_ref, o_ref, *, shift_idx):
    u = _rms_mod(x_ref[...], g_ref[...], mod_ref[0, shift_idx:shift_idx + 1, :],
                 mod_ref[0, shift_idx + 1:shift_idx + 2, :])
    o_ref[...] = u.astype(o_ref.dtype)


def _norm_mod(xs, gain, mod, *, shift_idx, rows, n_lat):
    d = xs.shape[1]
    tr = ROW_TILE
    return pl.pallas_call(
        functools.partial(_norm_mod_kernel, shift_idx=shift_idx),
        grid=(rows // tr,),
        in_specs=[
            pl.BlockSpec((tr, d), lambda i: (i, 0)),
            pl.BlockSpec((1, d), lambda i: (0, 0)),
            pl.BlockSpec((1, 6, d), lambda i: (jnp.where(i * tr >= n_lat, 1, 0), 0, 0)),
        ],
        out_specs=pl.BlockSpec((tr, d), lambda i: (i, 0)),
        out_shape=jax.ShapeDtypeStruct((rows, d), BF16),
        compiler_params=_params(("arbitrary",)),
        name="norm_mod",
    )(xs, gain.reshape(1, d), mod)


def _mm(a, w, *, rows, tm, tn, out_cols, out_dtype, col_maps, epilogue, extras=(), w_outer, name,
        layer=None, to_bf16=None):
    k = a.shape[1]
    n_i, n_j = rows // tm, out_cols // tn
    cast = w.dtype != BF16
    assert w_outer or not cast
    n_parts = len(col_maps)
    kc = _pick(k, (512, 256, 128))

    if w_outer:
        grid = (n_j, n_i)
        ij = lambda g0, g1: (g1, g0)
    else:
        grid = (n_i, n_j)
        ij = lambda g0, g1: (g0, g1)

    if w.ndim == 3:
        w_block, w_index = (None, k, tn), lambda col: (layer, 0, col)
    else:
        w_block, w_index = (k, tn), lambda col: (0, col)
    in_specs = [pl.BlockSpec((tm, k), lambda g0, g1: (ij(g0, g1)[0], 0))]
    for cm in col_maps:
        in_specs.append(pl.BlockSpec(w_block, lambda g0, g1, cm=cm: w_index(cm(ij(g0, g1)[1]))))
    for _, shape, fn in extras:
        in_specs.append(pl.BlockSpec(shape, lambda g0, g1, fn=fn: fn(*ij(g0, g1))))
    n_ex = len(extras)
    out_specs = [pl.BlockSpec((tm, tn), lambda g0, g1: ij(g0, g1))]
    out_shape = [jax.ShapeDtypeStruct((rows, out_cols), out_dtype)]
    n_side = 0 if to_bf16 is None else 1
    if n_side:
        _, side_r, side_c = to_bf16.shape
        side_steps = max(s for s in (1, 2, 4, 8, 16) if s <= grid[1] and side_r % (grid[0] * s * 16) == 0)
        slab = side_r // (grid[0] * side_steps)
        side_index = lambda g0, g1: g0 * side_steps + jnp.minimum(g1, side_steps - 1)
        in_specs.append(pl.BlockSpec((None, slab, side_c), lambda g0, g1: (layer, side_index(g0, g1), 0)))
        out_specs.append(pl.BlockSpec((slab, side_c), lambda g0, g1: (side_index(g0, g1), 0)))
        out_shape.append(jax.ShapeDtypeStruct((side_r, side_c), BF16))

    def body(*refs):
        a_ref = refs[0]
        w_refs = refs[1:1 + n_parts]
        ex_refs = refs[1 + n_parts:1 + n_parts + n_ex]
        n_in = 1 + n_parts + n_ex + n_side
        o_ref = refs[n_in]
        scratch = refs[n_in + 1 + n_side:]
        i, j = ij(pl.program_id(0), pl.program_id(1))
        if n_side:
            @pl.when(pl.program_id(1) < side_steps)
            def _():
                refs[n_in + 1][...] = refs[n_in - 1][...].astype(BF16)
        if cast:
            @pl.when(i == 0)
            def _():
                def cp(c, carry):
                    sl = pl.ds(pl.multiple_of(c * kc, kc), kc)
                    for p in range(n_parts):
                        scratch[p][sl, :] = w_refs[p][sl, :].astype(BF16)
                    return carry
                lax.fori_loop(0, k // kc, cp, 0)
            ws = scratch
        else:
            ws = w_refs
        av = a_ref[...]
        accs = [_dot(av, ws[p][...]) for p in range(n_parts)]
        epilogue(accs, i, j, ex_refs, o_ref)

    outs = pl.pallas_call(
        body,
        grid=grid,
        in_specs=in_specs,
        out_specs=out_specs,
        out_shape=out_shape,
        scratch_shapes=[pltpu.VMEM((k, tn), BF16) for _ in range(n_parts)] if cast else [],
        compiler_params=_params(("arbitrary", "arbitrary")),
        name=name,
    )(a, *([w] * n_parts), *[e[0] for e in extras], *([to_bf16] if n_side else []))
    return outs if n_side else outs[0]


def _row_gate(mod_ref, idx, i, tm, n_lat):
    rows = i * tm + lax.broadcasted_iota(jnp.int32, (tm, 1), 0)
    return jnp.where(rows < n_lat, mod_ref[0, idx:idx + 1, :], mod_ref[1, idx:idx + 1, :])


def _epi_swiglu(accs, i, j, ex, o_ref):
    a, g = accs
    o_ref[...] = (a * jax.nn.sigmoid(a) * g).astype(o_ref.dtype)


def _epi_residual(accs, i, j, ex, o_ref, *, gate_idx, tm, n_lat):
    res_ref, mod_ref = ex
    o_ref[...] = res_ref[...] + _row_gate(mod_ref, gate_idx, i, tm, n_lat) * accs[0]


def _epi_glu_residual(accs, i, j, ex, o_ref, *, gate_idx, tm, n_lat):
    res_ref, mod_ref = ex
    a, g = accs
    o_ref[...] = res_ref[...] + _row_gate(mod_ref, gate_idx, i, tm, n_lat) * (a * jax.nn.sigmoid(g))


def _residual_extras(xs, mod, tm, tn):
    return ((xs, (tm, tn), lambda i, j: (i, j)), (mod, (2, 6, tn), lambda i, j: (0, 0, j)))


def _mm_residual(a, w, xs, mod, *, gate_idx, rows, n_lat, w_outer, name):
    n = w.shape[1]
    tm = _pick(rows, MM_ROW_TILES_RESIDUAL if w_outer else (768, 512, 640, 256))
    tn = _pick(n, (512, 256) if w_outer else (256, 128))
    return _mm(a, w, rows=rows, tm=tm, tn=tn, out_cols=n, out_dtype=F32, col_maps=(lambda j: j,),
               epilogue=functools.partial(_epi_residual, gate_idx=gate_idx, tm=tm, n_lat=n_lat),
               extras=_residual_extras(xs, mod, tm, tn), w_outer=w_outer, name=name)


def _pool_kernel(xp_ref, x_ref, xn_ref, c_ref, g_ref, gf_ref, mod_ref, pw_ref, ps_ref, o_ref, h_ref, *, tr, n_lat, n_tot):
    i = pl.program_id(0)
    halo = POOL_HALO
    gain = g_ref[...]
    shift, scale, gate = mod_ref[0, 0:1, :], mod_ref[0, 1:2, :], mod_ref[0, 2:3, :]
    is_lat = i * tr < n_lat
    lat_rows = lax.broadcasted_iota(jnp.int32, (tr, 1), 0) * 0 + jnp.where(is_lat, 1, 0) > 0
    x = jnp.where(lat_rows, x_ref[...], c_ref[...])
    ue = jnp.concatenate([_rms_mod(xp_ref[...], gain, shift, scale), _rms_mod(x, gain, shift, scale),
                          _rms_mod(xn_ref[...], gain, shift, scale)], axis=0)
    seq_lo = jnp.where(is_lat, 0, n_lat)
    seq_hi = jnp.where(is_lat, n_lat, n_tot)
    t_g = i * tr + lax.broadcasted_iota(jnp.int32, (tr, 1), 0)
    s_g = i * tr - halo + lax.broadcasted_iota(jnp.int32, (1, tr + 2 * halo), 1)
    s_ok = (s_g >= seq_lo) & (s_g < seq_hi)
    pg = ue.shape[1] // len(POOL_WINDOWS)
    for gi, w in enumerate(POOL_WINDOWS):
        cols = slice(gi * pg, (gi + 1) * pg)
        ug = ue[:, cols]
        band = (s_g >= t_g - w // 2) & (s_g < t_g + w // 2) & s_ok
        ssum = _dot(jnp.where(band, 1.0, 0.0).astype(BF16), ug.astype(BF16))
        cnt = jnp.minimum(t_g + w // 2, seq_hi) - jnp.maximum(t_g - w // 2, seq_lo)
        p = ssum / cnt.astype(F32) - ug[halo:halo + tr]
        y = _dot(p.astype(BF16), pw_ref[gi]) * ps_ref[:, cols]
        o_ref[:, cols] = x[:, cols] + gate[:, cols] * y
    h_ref[...] = _rms_mod(o_ref[...], gf_ref[...], mod_ref[0, 3:4, :], mod_ref[0, 4:5, :]).astype(h_ref.dtype)


def _pool_layer(lat, ctx_src, ctx_block, gain, gain_ffn, mod, pool_w, pool_scale, *, rows, n_lat, n_tot):
    d = lat.shape[1]
    tr, halo = ROW_TILE, POOL_HALO
    hb = tr // halo
    last_hb = n_lat // halo - 1
    vec = pl.BlockSpec((1, d), lambda i: (0, 0))
    tile = pl.BlockSpec((tr, d), lambda i: (i, 0))
    return pl.pallas_call(
        functools.partial(_pool_kernel, tr=tr, n_lat=n_lat, n_tot=n_tot),
        grid=(rows // tr,),
        in_specs=[
            pl.BlockSpec((halo, d), lambda i: (jnp.clip(i * hb - 1, 0, last_hb), 0)),
            pl.BlockSpec((tr, d), lambda i: (jnp.minimum(i, n_lat // tr - 1), 0)),
            pl.BlockSpec((halo, d), lambda i: (jnp.minimum((i + 1) * hb, last_hb), 0)),
            pl.BlockSpec((tr, d), lambda i: (ctx_block, 0)),
            vec, vec,
            pl.BlockSpec((1, 6, d), lambda i: (jnp.where(i * tr >= n_lat, 1, 0), 0, 0)),
            pl.BlockSpec(pool_w.shape, lambda i: (0, 0, 0)),
            vec,
        ],
        out_specs=[tile, tile],
        out_shape=[jax.ShapeDtypeStruct((rows, d), F32), jax.ShapeDtypeStruct((rows, d), BF16)],
        compiler_params=_params(("arbitrary",)),
        name="pool_mixer",
    )(lat, lat, lat, ctx_src, gain.reshape(1, d), gain_ffn.reshape(1, d), mod, pool_w.astype(BF16),
      pool_scale.reshape(1, d))


def _s5_prep_kernel(lre_ref, lim_ref, ls_ref, bre_ref, bim_ref, are_ref, aim_ref, bbre_ref, bbim_ref):
    lam_re, lam_im = lre_ref[0], lim_ref[0]
    dt = jnp.exp(ls_ref[0])
    z_re, z_im = lam_re * dt, lam_im * dt
    mag = jnp.exp(z_re)
    a_re, a_im = mag * jnp.cos(z_im), mag * jnp.sin(z_im)
    den = lam_re * lam_re + lam_im * lam_im
    k_re = ((a_re - 1.0) * lam_re + a_im * lam_im) / den
    k_im = (a_im * lam_re - (a_re - 1.0) * lam_im) / den
    are_ref[0] = a_re
    aim_ref[0] = a_im
    b_re, b_im = bre_ref[0], bim_ref[0]
    bbre_ref[0] = k_re[:, None, :] * b_re - k_im[:, None, :] * b_im
    bbim_ref[0] = k_re[:, None, :] * b_im + k_im[:, None, :] * b_re


def _s5_prep(lam_re, lam_im, log_step, b_re, b_im):
    _, g, p, h = b_re.shape
    spec2 = pl.BlockSpec((1, g, p), lambda d: (d, 0, 0))
    spec3 = pl.BlockSpec((1, g, h, p), lambda d: (d, 0, 0, 0))
    return pl.pallas_call(
        _s5_prep_kernel,
        grid=(2,),
        in_specs=[spec2, spec2, pl.BlockSpec((1, g, 1), lambda d: (d, 0, 0)), spec3, spec3],
        out_specs=[spec2, spec2, spec3, spec3],
        out_shape=[jax.ShapeDtypeStruct((2, g, p), F32)] * 2 + [jax.ShapeDtypeStruct((2, g, h, p), F32)] * 2,
        compiler_params=_params(("arbitrary",)),
        name="s5_prep",
    )(lam_re, lam_im, log_step[..., None], jnp.swapaxes(b_re, 2, 3), jnp.swapaxes(b_im, 2, 3))


def _s5_pack(a_re, a_im, bb_re, bb_im, c_re, c_im):
    _, g, h, p = bb_re.shape
    r = g // 2
    eye2 = jnp.eye(2, dtype=F32)
    eye4 = jnp.eye(4, dtype=F32)
    slot = eye4[jnp.arange(r) % 4]

    def win_part(bb):
        t = bb.reshape(2, r, 2, h, p)
        t = t[:, :, :, :, None, :] * eye2[None, None, :, None, :, None]
        t = t.reshape(2, r, 2 * h, 2 * p)
        t = slot[None, :, :, None, None] * t[:, :, None, :, :]
        return t.reshape(2, r, 4 * 2 * h, 2 * p)

    def c_part(c):
        t = jnp.swapaxes(c, 2, 3).reshape(2, r, 2, p, h)
        t = t[:, :, :, :, None, :] * eye2[None, None, :, None, :, None]
        t = t.reshape(2, r, 2 * p, 2 * h)
        t = t[:, :, :, None, :] * slot[None, :, None, :, None]
        return t.reshape(2, r, 2 * p, 4 * 2 * h)

    win = jnp.concatenate([win_part(bb_re), win_part(bb_im)], axis=-1).astype(BF16)
    cmat = jnp.concatenate([c_part(c_re), c_part(-c_im)], axis=2).astype(BF16)
    return win, cmat, a_re.reshape(2, r, 2 * p), a_im.reshape(2, r, 2 * p)


def _s5_scan_kernel(u_ref, win_ref, cmat_ref, are_ref, aim_ref, y_ref, xre, xim, hre, him, *, nsub, rb):
    g0, c = pl.program_id(0), pl.program_id(1)
    rev = g0 // nsub == 1
    tc, pitch = S5_CHUNK, rb + S5_ROW_PAD
    block_rows = lambda r: pl.ds(r, tc, stride=pitch)

    @pl.when(c == 0)
    def _():
        hre[...] = jnp.zeros_like(hre)
        him[...] = jnp.zeros_like(him)

    for q in range(rb // 4):
        ut = u_ref[:, q * LANES:(q + 1) * LANES]
        for s in range(4):
            r = q * 4 + s
            x = _dot(ut, win_ref[0, r])
            xre[block_rows(r), :] = x[:, :LANES]
            xim[block_rows(r), :] = x[:, LANES:]

    a_re, a_im = are_ref[0], aim_ref[0]

    def step(t, carry):
        h_re, h_im = carry
        tt = jnp.where(rev, tc - 1 - t, t)
        rows = pl.ds(pl.multiple_of(tt * pitch, 8), rb)
        n_re = a_re * h_re - a_im * h_im + xre[rows, :]
        n_im = a_re * h_im + a_im * h_re + xim[rows, :]
        xre[rows, :] = n_re
        xim[rows, :] = n_im
        return n_re, n_im

    h_re, h_im = lax.fori_loop(0, tc, step, (hre[...], him[...]), unroll=8)
    hre[...] = h_re
    him[...] = h_im

    for q in range(rb // 4):
        acc = jnp.zeros((tc, LANES), F32)
        for s in range(4):
            r = q * 4 + s
            acc += (_dot(xre[block_rows(r), :].astype(BF16), cmat_ref[0, r, :LANES, :])
                    + _dot(xim[block_rows(r), :].astype(BF16), cmat_ref[0, r, LANES:, :]))
        y_ref[0, :, q * LANES:(q + 1) * LANES] = acc.astype(y_ref.dtype)


def _s5_scan(u, win, cmat, a_re, a_im, *, n_lat):
    n_tot, d = u.shape
    r_all = win.shape[1]
    rb = min(S5_BLOCKS, r_all)
    nsub = r_all // rb
    tc = S5_CHUNK
    n_c, n_lc = n_tot // tc, n_lat // tc
    cols = rb * 2 * S5_GROUP

    def split(w):
        return w.reshape((2 * nsub, rb) + w.shape[2:])

    def chunk(g0, c):
        return jnp.where(g0 // nsub == 0, (c + n_lc) % n_c, n_c - 1 - c)

    wspec = lambda shape: pl.BlockSpec((1,) + shape, lambda g0, c: (g0,) + (0,) * len(shape))
    return pl.pallas_call(
        functools.partial(_s5_scan_kernel, nsub=nsub, rb=rb),
        grid=(2 * nsub, n_c),
        in_specs=[
            pl.BlockSpec((tc, cols), lambda g0, c: (chunk(g0, c), g0 % nsub)),
            wspec((rb, LANES, 2 * LANES)), wspec((rb, 2 * LANES, LANES)), wspec((rb, LANES)), wspec((rb, LANES)),
        ],
        out_specs=pl.BlockSpec((1, tc, cols), lambda g0, c: (g0 // nsub, chunk(g0, c), g0 % nsub)),
        out_shape=jax.ShapeDtypeStruct((2, n_tot, d), BF16),
        scratch_shapes=[pltpu.VMEM((tc * (rb + S5_ROW_PAD), LANES), F32)] * 2 + [pltpu.VMEM((rb, LANES), F32)] * 2,
        compiler_params=_params(("arbitrary", "arbitrary")),
        name="s5_scan",
    )(u, split(win), split(cmat), split(a_re), split(a_im))


def _s5_gelu_kernel(y_ref, u_ref, d_ref, o_ref):
    v = y_ref[0].astype(F32) + y_ref[1].astype(F32) + d_ref[...] * u_ref[...].astype(F32)
    o_ref[...] = jax.nn.gelu(v, approximate=True).astype(o_ref.dtype)


def _s5_gelu(y, u, dvec, *, rows):
    d = u.shape[1]
    tr = ROW_TILE
    return pl.pallas_call(
        _s5_gelu_kernel,
        grid=(rows // tr,),
        in_specs=[pl.BlockSpec((2, tr, d), lambda i: (0, i, 0)), pl.BlockSpec((tr, d), lambda i: (i, 0)),
                  pl.BlockSpec((1, d), lambda i: (0, 0))],
        out_specs=pl.BlockSpec((tr, d), lambda i: (i, 0)),
        out_shape=jax.ShapeDtypeStruct((rows, d), BF16),
        compiler_params=_params(("arbitrary",)),
        name="s5_gelu",
    )(y, u, dvec.reshape(1, d))


def _rope_tables(n_lat, n_ctx):
    quarter = HEAD_DIM // 4
    inv = jnp.power(ROPE_BASE, -jnp.arange(quarter, dtype=F32) / quarter)
    t = jnp.arange(n_lat)
    ang_r = (t // GRID_W).astype(F32)[:, None] * inv
    ang_c = (t % GRID_W).astype(F32)[:, None] * inv
    ang = jnp.concatenate([ang_r, ang_r, ang_c, ang_c], axis=-1)
    sign = jnp.where((jnp.arange(HEAD_DIM) % (2 * quarter)) < quarter, -1.0, 1.0)
    cos = jnp.concatenate([jnp.cos(ang), jnp.ones((n_ctx, HEAD_DIM), F32)], axis=0)
    sin = jnp.concatenate([jnp.sin(ang) * sign, jnp.zeros((n_ctx, HEAD_DIM), F32)], axis=0)
    return cos, sin


def _epi_qkv(accs, i, j, ex, o_ref, *, n_norm_tiles, rope, tn):
    acc = accs[0]
    gain_ref = ex[0]

    @pl.when(j < n_norm_tiles)
    def _():
        quarter = HEAD_DIM // 4
        lane = lax.broadcasted_iota(jnp.int32, (1, HEAD_DIM), 1)
        first = (lane % (2 * quarter)) < quarter
        for h in range(tn // HEAD_DIM):
            cols = slice(h * HEAD_DIM, (h + 1) * HEAD_DIM)
            x = acc[:, cols]
            ms = jnp.mean(x * x, axis=-1, keepdims=True)
            xn = x * lax.rsqrt(ms + RMS_EPS) * gain_ref[:, cols]
            if rope:
                cos_ref, sin_ref = ex[1], ex[2]
                partner = jnp.where(first, pltpu.roll(xn, HEAD_DIM - quarter, 1), pltpu.roll(xn, quarter, 1))
                xn = xn * cos_ref[...] + partner * sin_ref[...]
            o_ref[:, cols] = xn.astype(o_ref.dtype)

    @pl.when(j >= n_norm_tiles)
    def _():
        o_ref[...] = acc.astype(o_ref.dtype)


def _qkv_proj(u, w_qkv, q_gain, k_gain, *, n_q, n_kv, rope_tables, rows):
    n = w_qkv.shape[1]
    tm = _pick(rows, MM_ROW_TILES)
    tn = _pick(n_kv * HEAD_DIM, (512, 256, 128))
    gain = jnp.concatenate([jnp.tile(q_gain * HEAD_DIM ** -0.5, n_q), jnp.tile(k_gain, n_kv),
                            jnp.ones((n_kv * HEAD_DIM,), F32)]).reshape(1, n)
    extras = [(gain, (1, tn), lambda i, j: (0, j))]
    if rope_tables is not None:
        extras += [(t, (tm, HEAD_DIM), lambda i, j: (i, 0)) for t in rope_tables]
    epi = functools.partial(_epi_qkv, n_norm_tiles=(n_q + n_kv) * HEAD_DIM // tn, rope=rope_tables is not None, tn=tn)
    return _mm(u, w_qkv, rows=rows, tm=tm, tn=tn, out_cols=n, out_dtype=BF16, col_maps=(lambda j: j,),
               epilogue=epi, extras=tuple(extras), w_outer=True, name="qkv_proj")


def _softmax_pv(tiles, values, extra_logit=None):
    rows = tiles[0].shape[0]
    mx = functools.reduce(jnp.maximum, tiles)
    if extra_logit is not None:
        mx = jnp.maximum(mx, extra_logit)
    m = jnp.broadcast_to(jnp.max(mx, axis=-1, keepdims=True), (rows, LANES))
    p = jnp.concatenate([jnp.exp(t - m).astype(BF16) for t in tiles], axis=1)
    ones = jnp.ones((LANES, LANES), BF16)
    v_aug = jnp.concatenate([jnp.concatenate([v, ones], axis=1) for v in values], axis=0)
    acc = _dot(p, v_aug)
    den = acc[:, LANES:]
    if extra_logit is not None:
        den = den + jnp.exp(extra_logit - m)
    return acc[:, :LANES] / den


def _swa_kernel(sink_ref, q_ref, kp_ref, kc_ref, kn_ref, vp_ref, vc_ref, vn_ref, kx_ref, vx_ref, o_ref, *, n_lat_blocks):
    h, b = pl.program_id(0), pl.program_id(1)
    blk = SWA_WINDOW
    is_lat = b < n_lat_blocks
    qi = lax.broadcasted_iota(jnp.int32, (SWA_GROUP * blk, 1), 0) & (blk - 1)
    kj = lax.broadcasted_iota(jnp.int32, (1, blk), 1)
    off = lambda ok: jnp.where(ok, 0, blk + 1)
    m_prev = kj >= qi + off(is_lat & (b > 0))
    m_cur = kj >= qi * 0 + off(is_lat)
    m_next = kj <= qi - off(is_lat & (b < n_lat_blocks - 1))
    n_ctx_tiles = kx_ref.shape[0] // LANES
    hp = kp_ref.shape[1] // HEAD_DIM
    for hh in range(hp):
        kcols = slice(hh * HEAD_DIM, (hh + 1) * HEAD_DIM)
        qh = lambda g: (hh * SWA_GROUP + g) * HEAD_DIM
        q = jnp.concatenate([q_ref[:, qh(g):qh(g) + HEAD_DIM] for g in range(SWA_GROUP)], axis=0)
        sink = jnp.concatenate([jnp.full((blk, LANES), sink_ref[(h * hp + hh) * SWA_GROUP + g], F32)
                                for g in range(SWA_GROUP)], axis=0)
        s_ctx = _dot_nt(q, kx_ref[:, kcols])
        tiles = [jnp.where(m_prev, _dot_nt(q, kp_ref[:, kcols]), NEG_INF),
                 jnp.where(m_cur, _dot_nt(q, kc_ref[:, kcols]), NEG_INF),
                 jnp.where(m_next, _dot_nt(q, kn_ref[:, kcols]), NEG_INF)]
        tiles += [s_ctx[:, t * LANES:(t + 1) * LANES] for t in range(n_ctx_tiles)]
        vals = [vp_ref[:, kcols], vc_ref[:, kcols], vn_ref[:, kcols]]
        vals += [vx_ref[t * LANES:(t + 1) * LANES, kcols] for t in range(n_ctx_tiles)]
        o = _softmax_pv(tiles, vals, sink).astype(o_ref.dtype)
        for g in range(SWA_GROUP):
            o_ref[:, qh(g):qh(g) + HEAD_DIM] = o[g * blk:(g + 1) * blk]


def _swa_attention(qkv, sinks, *, n_q, n_kv, rows, n_lat):
    blk = SWA_WINDOW
    nlb = n_lat // blk
    ctx_rows = (qkv.shape[0] - n_lat)
    hp = min(SWA_KV_PER_STEP, n_kv)
    assert n_kv % hp == 0 and n_q % hp == 0
    kcol = lambda h: n_q // hp + h
    vcol = lambda h: (n_q + n_kv) // hp + h
    prev = lambda b: jnp.clip(b - 1, 0, nlb - 1)
    cur = lambda b: jnp.minimum(b, nlb - 1)
    nxt = lambda b: jnp.clip(b + 1, 0, nlb - 1)
    kv = lambda rowf, colf: pl.BlockSpec((blk, hp * HEAD_DIM), lambda h, b: (rowf(b), colf(h)))
    ctx = lambda colf: pl.BlockSpec((ctx_rows, hp * HEAD_DIM), lambda h, b: (n_lat // ctx_rows, colf(h)))
    return pl.pallas_call(
        functools.partial(_swa_kernel, n_lat_blocks=nlb),
        grid=(n_kv // hp, rows // blk),
        in_specs=[
            pl.BlockSpec(memory_space=pltpu.SMEM),
            pl.BlockSpec((blk, hp * SWA_GROUP * HEAD_DIM), lambda h, b: (b, h)),
            kv(prev, kcol), kv(cur, kcol), kv(nxt, kcol), kv(prev, vcol), kv(cur, vcol), kv(nxt, vcol),
            ctx(kcol), ctx(vcol),
        ],
        out_specs=pl.BlockSpec((blk, hp * SWA_GROUP * HEAD_DIM), lambda h, b: (b, h)),
        out_shape=jax.ShapeDtypeStruct((rows, n_q * HEAD_DIM), BF16),
        compiler_params=_params(("arbitrary", "arbitrary")),
        name="swa_attention",
    )(sinks, *([qkv] * 9))


def _na_bias_kernel(rpb_ref, o_ref):
    h = pl.program_id(0)
    n_ri, n_ci = 2 * NA_WIN_ROWS - 1, 2 * NA_WIN_COLS - 1
    cq = lax.broadcasted_iota(jnp.int32, (GRID_W, 2 * GRID_W), 0)
    ck = lax.broadcasted_iota(jnp.int32, (GRID_W, 2 * GRID_W), 1)
    second = ck >= GRID_W
    ck = jnp.where(second, ck - GRID_W, ck)
    c0 = jnp.clip(cq - NA_WIN_COLS // 2, 0, GRID_W - NA_WIN_COLS)
    in_win = (ck >= c0) & (ck < c0 + NA_WIN_COLS)
    cidx = jnp.clip(ck - cq + NA_WIN_COLS - 1, 0, n_ci - 1)
    neg = jnp.full((GRID_W, 2 * GRID_W), NEG_INF, F32)
    per_offset = []
    for ri in range(n_ri):
        t = neg
        for dci in range(n_ci):
            t = jnp.where(cidx == dci, rpb_ref[(h * n_ri + ri) * n_ci + dci], t)
        per_offset.append(jnp.where(in_win, t, NEG_INF))
    for k in range(n_ri + 1):
        o_ref[0, k] = jnp.where(second, per_offset[k] if k < n_ri else neg, per_offset[k - 1] if k >= 1 else neg)


def _na_bias(rpb):
    n_heads = rpb.shape[0]
    n_tiles = 2 * NA_WIN_ROWS
    return pl.pallas_call(
        _na_bias_kernel,
        grid=(n_heads,),
        in_specs=[pl.BlockSpec(memory_space=pltpu.SMEM)],
        out_specs=pl.BlockSpec((1, n_tiles, GRID_W, 2 * GRID_W), lambda h: (h, 0, 0, 0)),
        out_shape=jax.ShapeDtypeStruct((n_heads, n_tiles, GRID_W, 2 * GRID_W), F32),
        compiler_params=_params(("arbitrary",)),
        name="na_bias",
    )(rpb.reshape(-1))


def _na_kernel(q_ref, kp_ref, kc_ref, kn_ref, vp_ref, vc_ref, vn_ref, kx_ref, vx_ref, bias_ref, o_ref, *, n_grid_rows):
    b = pl.program_id(1)
    qr, w = NA_QROWS, GRID_W
    is_lat = b * qr < n_grid_rows
    lane = lax.broadcasted_iota(jnp.int32, (1, 2 * w), 1)
    n_kt = qr // 2
    windows = {}
    for kb in range(3):
        for dq in range(qr):
            r = b * qr + dq
            r0 = jnp.clip(r - NA_WIN_ROWS // 2, 0, n_grid_rows - NA_WIN_ROWS)
            for jt in range(n_kt):
                kr = b * qr + (kb - 1) * qr + 2 * jt
                ok0 = (kr >= r0) & (kr < r0 + NA_WIN_ROWS) & is_lat
                ok1 = (kr + 1 >= r0) & (kr + 1 < r0 + NA_WIN_ROWS) & is_lat
                lo, hi = jnp.where(ok0, 0, w), jnp.where(ok1, 2 * w, w)
                windows[kb, dq, jt] = (lane >= lo) & (lane < hi)
    for hh in range(q_ref.shape[1] // HEAD_DIM):
        cols = slice(hh * HEAD_DIM, (hh + 1) * HEAD_DIM)
        q = q_ref[:, cols]
        tiles, vals = [], []
        for kb, (k_ref, v_ref) in enumerate(((kp_ref, vp_ref), (kc_ref, vc_ref), (kn_ref, vn_ref))):
            s = _dot_nt(q, k_ref[:, cols])
            for jt in range(n_kt):
                parts = []
                for dq in range(qr):
                    ri0 = (kb - 1) * qr + 2 * jt - dq + NA_WIN_ROWS - 1
                    t = s[dq * w:(dq + 1) * w, jt * 2 * w:(jt + 1) * 2 * w] + bias_ref[hh, ri0 + 1]
                    parts.append(jnp.where(windows[kb, dq, jt], t, NEG_INF))
                tiles.append(jnp.concatenate(parts, axis=0))
                vals.append(v_ref[jt * LANES:(jt + 1) * LANES, cols])
        s_ctx = _dot_nt(q, kx_ref[:, cols])
        for t in range(kx_ref.shape[0] // LANES):
            tiles.append(s_ctx[:, t * LANES:(t + 1) * LANES])
            vals.append(vx_ref[t * LANES:(t + 1) * LANES, cols])
        o_ref[:, cols] = _softmax_pv(tiles, vals).astype(o_ref.dtype)


def _na_attention(qkv, bias, *, n_heads, rows, n_lat):
    blk = NA_QROWS * GRID_W
    assert blk == qkv.shape[0] - n_lat
    nlb = n_lat // blk
    hp = min(NA_HEADS_PER_STEP, n_heads)
    assert n_heads % hp == 0
    kcol = lambda h: n_heads // hp + h
    vcol = lambda h: 2 * n_heads // hp + h
    prev = lambda b: jnp.clip(b - 1, 0, nlb - 1)
    cur = lambda b: jnp.minimum(b, nlb - 1)
    nxt = lambda b: jnp.clip(b + 1, 0, nlb - 1)
    spec = lambda rowf, colf: pl.BlockSpec((blk, hp * HEAD_DIM), lambda h, b: (rowf(b), colf(h)))
    ctx = lambda b: nlb
    return pl.pallas_call(
        functools.partial(_na_kernel, n_grid_rows=n_lat // GRID_W),
        grid=(n_heads // hp, rows // blk),
        in_specs=[
            spec(lambda b: b, lambda h: h),
            spec(prev, kcol), spec(cur, kcol), spec(nxt, kcol), spec(prev, vcol), spec(cur, vcol), spec(nxt, vcol),
            spec(ctx, kcol), spec(ctx, vcol),
            pl.BlockSpec((hp,) + bias.shape[1:], lambda h, b: (h, 0, 0, 0)),
        ],
        out_specs=spec(lambda b: b, lambda h: h),
        out_shape=jax.ShapeDtypeStruct((rows, n_heads * HEAD_DIM), BF16),
        compiler_params=_params(("arbitrary", "arbitrary")),
        name="na_attention",
    )(*([qkv] * 9), bias)


def kernel(x, c, ctx, c_ctx, ada_down, ada_up, ada_b, norm_mix, norm_ffn, ffn_w_gate_up, ffn_w_down, pool_w, pool_scale, s5_lam_re, s5_lam_im, s5_log_step, s5_b_re, s5_b_im, s5_c_re, s5_c_im, s5_d, s5_w_glu, swa_w_qkv, swa_w_o, swa_q_gain, swa_k_gain, swa_sinks, na_w_qkv, na_w_o, na_q_gain, na_k_gain, na_rpb):
    bsz, n_lat, d = x.shape
    n_ctx = ctx.shape[1]
    assert bsz == 1 and n_lat % ROW_TILE == 0 and n_ctx == ROW_TILE
    n_tot = n_lat + n_ctx
    depth = ada_down.shape[0]
    d_ff = ffn_w_down.shape[1]
    n_heads = d // HEAD_DIM

    mods = _adaln_all(c, c_ctx, ada_down, ada_up, ada_b)
    xs = None

    for layer in range(depth):
        kind = layer % 4
        last = layer == depth - 1
        rows = n_lat if last else n_tot
        mod = mods[layer]
        res = dict(rows=rows, n_lat=n_lat)

        if kind == 0:
            lat, ctx_src, ctx_block = (x[0], ctx[0], 0) if xs is None else (xs, xs, n_lat // ROW_TILE)
            xs, hx = _pool_layer(lat, ctx_src, ctx_block, norm_mix[layer], norm_ffn[layer], mod, pool_w, pool_scale,
                                 n_tot=n_tot, **res)
        else:
            u = _norm_mod(xs, norm_mix[layer], mod, shift_idx=0, rows=n_tot, n_lat=n_lat)
            if kind == 1:
                a_re, a_im, bb_re, bb_im = _s5_prep(s5_lam_re, s5_lam_im, s5_log_step, s5_b_re, s5_b_im)
                y = _s5_scan(u, *_s5_pack(a_re, a_im, bb_re, bb_im, s5_c_re, s5_c_im), n_lat=n_lat)
                g = _s5_gelu(y, u, s5_d, rows=rows)
                half = s5_w_glu.shape[1] // 2
                tm = _pick(rows, MM_ROW_TILES)
                tn = _pick(half, (256, 128))
                xs = _mm(g, s5_w_glu, rows=rows, tm=tm, tn=tn, out_cols=half, out_dtype=F32,
                         col_maps=(lambda j: j, lambda j, o=half // tn: j + o),
                         epilogue=functools.partial(_epi_glu_residual, gate_idx=2, tm=tm, n_lat=n_lat),
                         extras=_residual_extras(xs, mod, tm, tn), w_outer=True, name="s5_glu")
            elif kind == 2:
                n_kv = n_heads // SWA_GROUP
                qkv = _qkv_proj(u, swa_w_qkv, swa_q_gain, swa_k_gain, n_q=n_heads, n_kv=n_kv,
                                rope_tables=_rope_tables(n_lat, n_ctx), rows=n_tot)
                o = _swa_attention(qkv, swa_sinks, n_q=n_heads, n_kv=n_kv, **res)
                xs = _mm_residual(o, swa_w_o, xs, mod, gate_idx=2, w_outer=True, name="swa_out", **res)
            else:
                qkv = _qkv_proj(u, na_w_qkv, na_q_gain, na_k_gain, n_q=n_heads, n_kv=n_heads,
                                rope_tables=None, rows=n_tot)
                o = _na_attention(qkv, _na_bias(na_rpb), n_heads=n_heads, **res)
                xs = _mm_residual(o, na_w_o, xs, mod, gate_idx=2, w_outer=True, name="na_out", **res)

            hx = _norm_mod(xs, norm_ffn[layer], mod, shift_idx=3, **res)
        tm = _pick(rows, (2112, 2048, 640, 256))
        tn = _pick(d_ff, (256, 128))
        hh, w_down = _mm(hx, ffn_w_gate_up, layer=layer, rows=rows, tm=tm, tn=tn, out_cols=d_ff, out_dtype=BF16,
                         col_maps=(lambda j: j, lambda j, o=d_ff // tn: j + o), epilogue=_epi_swiglu,
                         w_outer=True, name="ffn_gate_up", to_bf16=ffn_w_down)
        xs = _mm_residual(hh, w_down, xs, mod, gate_idx=5, w_outer=False, name="ffn_down", **res)

    return xs[:n_lat][None]
```

```python
import functools

import jax
import jax.numpy as jnp
from jax import lax
from jax.experimental import pallas as pl
from jax.experimental.pallas import tpu as pltpu

F32 = jnp.float32
BF16 = jnp.bfloat16

RMS_EPS = 1e-6
NEG_INF = -1e30
HEAD_DIM = 128
GRID_W = 64
ROPE_BASE = 10000.0
POOL_WINDOWS = (2, 4, 8, 16)
POOL_HALO = 8
S5_GROUP = 16
S5_CHUNK = 256
S5_BLOCKS = 64
S5_ROW_PAD = 8
SWA_WINDOW = 128
SWA_GROUP = 4
NA_WIN_ROWS = 8
NA_WIN_COLS = 16
NA_QROWS = 4
SWA_KV_PER_STEP = 8
NA_HEADS_PER_STEP = 8

LANES = 128
ROW_TILE = 256
MM_ROW_TILES_FFN_UP = (2112, 2048, 640, 256)
MM_ROW_TILES = (1408, 1024, 640, 256)
MM_ROW_TILES_RESIDUAL = (1056, 1024, 640, 256)
V7X_VMEM_LIMIT = 63 * 1024 * 1024


def _pick(n, candidates):
    for c in candidates:
        if n % c == 0:
            return c
    raise ValueError(f"no tile for {n} in {candidates}")


def _params(sem, vmem=V7X_VMEM_LIMIT):
    return pltpu.CompilerParams(dimension_semantics=sem, vmem_limit_bytes=vmem)


def _dot(a, b):
    return jnp.dot(a, b, preferred_element_type=F32)


def _dot_nt(a, b):
    return lax.dot_general(a, b, (((1,), (1,)), ((), ())), preferred_element_type=F32)


def _small_mm_kernel(a_ref, w_ref, b_ref, o_ref, *, silu_in, kc):
    a = a_ref[0]
    if silu_in:
        a = a * jax.nn.sigmoid(a)
    a = a.astype(BF16)
    k = a.shape[1]
    acc = jnp.zeros(o_ref.shape[1:], F32)
    for k0 in range(0, k, kc):
        acc += _dot(a[:, k0:k0 + kc], w_ref[0, k0:k0 + kc, :].astype(BF16))
    o_ref[0] = acc + b_ref[0]


def _small_mm(a, w, b, *, silu_in, tn):
    depth, k, n = w.shape
    shared = a.shape[0] == 1
    kern = functools.partial(_small_mm_kernel, silu_in=silu_in, kc=min(k, 512))
    return pl.pallas_call(
        kern,
        grid=(depth, n // tn),
        in_specs=[
            pl.BlockSpec((1, 8, k), lambda l, j: (0 if shared else l, 0, 0)),
            pl.BlockSpec((1, k, tn), lambda l, j: (l, 0, j)),
            pl.BlockSpec((1, 1, tn), lambda l, j: (l, 0, j)),
        ],
        out_specs=pl.BlockSpec((1, 8, tn), lambda l, j: (l, 0, j)),
        out_shape=jax.ShapeDtypeStruct((depth, 8, n), F32),
        compiler_params=_params(("arbitrary", "arbitrary")),
        name="adaln_mm",
    )(a, w, b)


def _adaln_all(c, c_ctx, ada_down, ada_up, ada_b):
    depth, d, rank = ada_down.shape
    cvec = jnp.zeros((1, 8, d), F32).at[0, 0].set(c[0]).at[0, 1].set(c_ctx)
    t = _small_mm(cvec, ada_down, jnp.zeros((depth, 1, rank), F32), silu_in=True, tn=_pick(rank, (512, 256, 128)))
    m = _small_mm(t, ada_up, ada_b.reshape(depth, 1, 6 * d), silu_in=False, tn=_pick(6 * d, (2048, 1024, 512)))
    return m[:, :2].reshape(depth, 2, 6, d)


def _rms_mod(x, gain, shift, scale):
    ms = jnp.mean(x * x, axis=-1, keepdims=True)
    return (x * lax.rsqrt(ms + RMS_EPS) * gain) * (1.0 + scale) + shift


def _norm_mod_kernel(x_ref, g_ref, mod_ref, o_ref, *, shift_idx):
    u = _rms_mod(x_ref[...], g_ref[...], mod_ref[0, shift_idx:shift_idx + 1, :],
                 mod_ref[0, shift_idx + 1:shift_idx + 2, :])
    o_ref[...] = u.astype(o_ref.dtype)


def _norm_mod(xs, gain, mod, *, shift_idx, rows, n_lat):
    d = xs.shape[1]
    tr = ROW_TILE
    return pl.pallas_call(
        functools.partial(_norm_mod_kernel, shift_idx=shift_idx),
        grid=(rows // tr,),
        in_specs=[
            pl.BlockSpec((tr, d), lambda i: (i, 0)),
            pl.BlockSpec((1, d), lambda i: (0, 0)),
            pl.BlockSpec((1, 6, d), lambda i: (jnp.where(i * tr >= n_lat, 1, 0), 0, 0)),
        ],
        out_specs=pl.BlockSpec((tr, d), lambda i: (i, 0)),
        out_shape=jax.ShapeDtypeStruct((rows, d), BF16),
        compiler_params=_params(("arbitrary",)),
        name="norm_mod",
    )(xs, gain.reshape(1, d), mod)


def _mm(a, w, *, rows, tm, tn, out_cols, out_dtype, col_maps, epilogue, extras=(), w_outer, name,
        layer=None, to_bf16=None):
    k = a.shape[1]
    n_i, n_j = rows // tm, out_cols // tn
    cast = w.dtype != BF16
    assert w_outer or not cast
    n_parts = len(col_maps)
    kc = _pick(k, (512, 256, 128))

    if w_outer:
        grid = (n_j, n_i)
        ij = lambda g0, g1: (g1, g0)
    else:
        grid = (n_i, n_j)
        ij = lambda g0, g1: (g0, g1)

    if w.ndim == 3:
        w_block, w_index = (None, k, tn), lambda col: (layer, 0, col)
    else:
        w_block, w_index = (k, tn), lambda col: (0, col)
    in_specs = [pl.BlockSpec((tm, k), lambda g0, g1: (ij(g0, g1)[0], 0))]
    for cm in col_maps:
        in_specs.append(pl.BlockSpec(w_block, lambda g0, g1, cm=cm: w_index(cm(ij(g0, g1)[1]))))
    for _, shape, fn in extras:
        in_specs.append(pl.BlockSpec(shape, lambda g0, g1, fn=fn: fn(*ij(g0, g1))))
    n_ex = len(extras)
    out_specs = [pl.BlockSpec((tm, tn), lambda g0, g1: ij(g0, g1))]
    out_shape = [jax.ShapeDtypeStruct((rows, out_cols), out_dtype)]
    n_side = 0 if to_bf16 is None else 1
    if n_side:
        _, side_r, side_c = to_bf16.shape
        side_steps = max(s for s in (1, 2, 4, 8, 16) if s <= grid[1] and side_r % (grid[0] * s * 16) == 0)
        slab = side_r // (grid[0] * side_steps)
        side_index = lambda g0, g1: g0 * side_steps + jnp.minimum(g1, side_steps - 1)
        in_specs.append(pl.BlockSpec((None, slab, side_c), lambda g0, g1: (layer, side_index(g0, g1), 0)))
        out_specs.append(pl.BlockSpec((slab, side_c), lambda g0, g1: (side_index(g0, g1), 0)))
        out_shape.append(jax.ShapeDtypeStruct((side_r, side_c), BF16))

    def body(*refs):
        a_ref = refs[0]
        w_refs = refs[1:1 + n_parts]
        ex_refs = refs[1 + n_parts:1 + n_parts + n_ex]
        n_in = 1 + n_parts + n_ex + n_side
        o_ref = refs[n_in]
        scratch = refs[n_in + 1 + n_side:]
        i, j = ij(pl.program_id(0), pl.program_id(1))
        if n_side:
            @pl.when(pl.program_id(1) < side_steps)
            def _():
                refs[n_in + 1][...] = refs[n_in - 1][...].astype(BF16)
        if cast:
            @pl.when(i == 0)
            def _():
                def cp(c, carry):
                    sl = pl.ds(pl.multiple_of(c * kc, kc), kc)
                    for p in range(n_parts):
                        scratch[p][sl, :] = w_refs[p][sl, :].astype(BF16)
                    return carry
                lax.fori_loop(0, k // kc, cp, 0)
            ws = scratch
        else:
            ws = w_refs
        av = a_ref[...]
        accs = [_dot(av, ws[p][...]) for p in range(n_parts)]
        epilogue(accs, i, j, ex_refs, o_ref)

    outs = pl.pallas_call(
        body,
        grid=grid,
        in_specs=in_specs,
        out_specs=out_specs,
        out_shape=out_shape,
        scratch_shapes=[pltpu.VMEM((k, tn), BF16) for _ in range(n_parts)] if cast else [],
        compiler_params=_params(("arbitrary", "arbitrary")),
        name=name,
    )(a, *([w] * n_parts), *[e[0] for e in extras], *([to_bf16] if n_side else []))
    return outs if n_side else outs[0]


def _row_gate(mod_ref, idx, i, tm, n_lat):
    rows = i * tm + lax.broadcasted_iota(jnp.int32, (tm, 1), 0)
    return jnp.where(rows < n_lat, mod_ref[0, idx:idx + 1, :], mod_ref[1, idx:idx + 1, :])


def _epi_swiglu(accs, i, j, ex, o_ref):
    a, g = accs
    o_ref[...] = (a * jax.nn.sigmoid(a) * g).astype(o_ref.dtype)


def _epi_residual(accs, i, j, ex, o_ref, *, gate_idx, tm, n_lat):
    res_ref, mod_ref = ex
    o_ref[...] = res_ref[...] + _row_gate(mod_ref, gate_idx, i, tm, n_lat) * accs[0]


def _epi_glu_residual(accs, i, j, ex, o_ref, *, gate_idx, tm, n_lat):
    res_ref, mod_ref = ex
    a, g = accs
    o_ref[...] = res_ref[...] + _row_gate(mod_ref, gate_idx, i, tm, n_lat) * (a * jax.nn.sigmoid(g))


def _residual_extras(xs, mod, tm, tn):
    return ((xs, (tm, tn), lambda i, j: (i, j)), (mod, (2, 6, tn), lambda i, j: (0, 0, j)))


def _mm_residual(a, w, xs, mod, *, gate_idx, rows, n_lat, w_outer, name):
    n = w.shape[1]
    tm = _pick(rows, MM_ROW_TILES_RESIDUAL)
    tn = _pick(n, (512, 256) if w_outer else (256, 128))
    return _mm(a, w, rows=rows, tm=tm, tn=tn, out_cols=n, out_dtype=F32, col_maps=(lambda j: j,),
               epilogue=functools.partial(_epi_residual, gate_idx=gate_idx, tm=tm, n_lat=n_lat),
               extras=_residual_extras(xs, mod, tm, tn), w_outer=w_outer, name=name)


def _pool_kernel(xp_ref, x_ref, xn_ref, c_ref, g_ref, gf_ref, mod_ref, pw_ref, ps_ref, o_ref, h_ref, *, tr, n_lat, n_tot):
    i = pl.program_id(0)
    halo = POOL_HALO
    gain = g_ref[...]
    shift, scale, gate = mod_ref[0, 0:1, :], mod_ref[0, 1:2, :], mod_ref[0, 2:3, :]
    is_lat = i * tr < n_lat
    lat_rows = lax.broadcasted_iota(jnp.int32, (tr, 1), 0) * 0 + jnp.where(is_lat, 1, 0) > 0
    x = jnp.where(lat_rows, x_ref[...], c_ref[...])
    ue = jnp.concatenate([_rms_mod(xp_ref[...], gain, shift, scale), _rms_mod(x, gain, shift, scale),
                          _rms_mod(xn_ref[...], gain, shift, scale)], axis=0)
    seq_lo = jnp.where(is_lat, 0, n_lat)
    seq_hi = jnp.where(is_lat, n_lat, n_tot)
    t_g = i * tr + lax.broadcasted_iota(jnp.int32, (tr, 1), 0)
    s_g = i * tr - halo + lax.broadcasted_iota(jnp.int32, (1, tr + 2 * halo), 1)
    s_ok = (s_g >= seq_lo) & (s_g < seq_hi)
    pg = ue.shape[1] // len(POOL_WINDOWS)
    for gi, w in enumerate(POOL_WINDOWS):
        cols = slice(gi * pg, (gi + 1) * pg)
        ug = ue[:, cols]
        band = (s_g >= t_g - w // 2) & (s_g < t_g + w // 2) & s_ok
        ssum = _dot(jnp.where(band, 1.0, 0.0).astype(BF16), ug.astype(BF16))
        cnt = jnp.minimum(t_g + w // 2, seq_hi) - jnp.maximum(t_g - w // 2, seq_lo)
        p = ssum / cnt.astype(F32) - ug[halo:halo + tr]
        y = _dot(p.astype(BF16), pw_ref[gi]) * ps_ref[:, cols]
        o_ref[:, cols] = x[:, cols] + gate[:, cols] * y
    h_ref[...] = _rms_mod(o_ref[...], gf_ref[...], mod_ref[0, 3:4, :], mod_ref[0, 4:5, :]).astype(h_ref.dtype)


def _pool_layer(lat, ctx_src, ctx_block, gain, gain_ffn, mod, pool_w, pool_scale, *, rows, n_lat, n_tot):
    d = lat.shape[1]
    tr, halo = ROW_TILE, POOL_HALO
    hb = tr // halo
    last_hb = n_lat // halo - 1
    vec = pl.BlockSpec((1, d), lambda i: (0, 0))
    tile = pl.BlockSpec((tr, d), lambda i: (i, 0))
    return pl.pallas_call(
        functools.partial(_pool_kernel, tr=tr, n_lat=n_lat, n_tot=n_tot),
        grid=(rows // tr,),
        in_specs=[
            pl.BlockSpec((halo, d), lambda i: (jnp.clip(i * hb - 1, 0, last_hb), 0)),
            pl.BlockSpec((tr, d), lambda i: (jnp.minimum(i, n_lat // tr - 1), 0)),
            pl.BlockSpec((halo, d), lambda i: (jnp.minimum((i + 1) * hb, last_hb), 0)),
            pl.BlockSpec((tr, d), lambda i: (ctx_block, 0)),
            vec, vec,
            pl.BlockSpec((1, 6, d), lambda i: (jnp.where(i * tr >= n_lat, 1, 0), 0, 0)),
            pl.BlockSpec(pool_w.shape, lambda i: (0, 0, 0)),
            vec,
        ],
        out_specs=[tile, tile],
        out_shape=[jax.ShapeDtypeStruct((rows, d), F32), jax.ShapeDtypeStruct((rows, d), BF16)],
        compiler_params=_params(("arbitrary",)),
        name="pool_mixer",
    )(lat, lat, lat, ctx_src, gain.reshape(1, d), gain_ffn.reshape(1, d), mod, pool_w.astype(BF16),
      pool_scale.reshape(1, d))


def _s5_prep_kernel(lre_ref, lim_ref, ls_ref, bre_ref, bim_ref, are_ref, aim_ref, bbre_ref, bbim_ref):
    lam_re, lam_im = lre_ref[0], lim_ref[0]
    dt = jnp.exp(ls_ref[0])
    z_re, z_im = lam_re * dt, lam_im * dt
    mag = jnp.exp(z_re)
    a_re, a_im = mag * jnp.cos(z_im), mag * jnp.sin(z_im)
    den = lam_re * lam_re + lam_im * lam_im
    k_re = ((a_re - 1.0) * lam_re + a_im * lam_im) / den
    k_im = (a_im * lam_re - (a_re - 1.0) * lam_im) / den
    are_ref[0] = a_re
    aim_ref[0] = a_im
    b_re, b_im = bre_ref[0], bim_ref[0]
    bbre_ref[0] = k_re[:, None, :] * b_re - k_im[:, None, :] * b_im
    bbim_ref[0] = k_re[:, None, :] * b_im + k_im[:, None, :] * b_re


def _s5_prep(lam_re, lam_im, log_step, b_re, b_im):
    _, g, p, h = b_re.shape
    spec2 = pl.BlockSpec((1, g, p), lambda d: (d, 0, 0))
    spec3 = pl.BlockSpec((1, g, h, p), lambda d: (d, 0, 0, 0))
    return pl.pallas_call(
        _s5_prep_kernel,
        grid=(2,),
        in_specs=[spec2, spec2, pl.BlockSpec((1, g, 1), lambda d: (d, 0, 0)), spec3, spec3],
        out_specs=[spec2, spec2, spec3, spec3],
        out_shape=[jax.ShapeDtypeStruct((2, g, p), F32)] * 2 + [jax.ShapeDtypeStruct((2, g, h, p), F32)] * 2,
        compiler_params=_params(("arbitrary",)),
        name="s5_prep",
    )(lam_re, lam_im, log_step[..., None], jnp.swapaxes(b_re, 2, 3), jnp.swapaxes(b_im, 2, 3))


def _s5_pack(a_re, a_im, bb_re, bb_im, c_re, c_im):
    _, g, h, p = bb_re.shape
    r = g // 2
    eye2 = jnp.eye(2, dtype=F32)
    eye4 = jnp.eye(4, dtype=F32)
    slot = eye4[jnp.arange(r) % 4]

    def win_part(bb):
        t = bb.reshape(2, r, 2, h, p)
        t = t[:, :, :, :, None, :] * eye2[None, None, :, None, :, None]
        t = t.reshape(2, r, 2 * h, 2 * p)
        t = slot[None, :, :, None, None] * t[:, :, None, :, :]
        return t.reshape(2, r, 4 * 2 * h, 2 * p)

    def c_part(c):
        t = jnp.swapaxes(c, 2, 3).reshape(2, r, 2, p, h)
        t = t[:, :, :, :, None, :] * eye2[None, None, :, None, :, None]
        t = t.reshape(2, r, 2 * p, 2 * h)
        t = t[:, :, :, None, :] * slot[None, :, None, :, None]
        return t.reshape(2, r, 2 * p, 4 * 2 * h)

    win = jnp.concatenate([win_part(bb_re), win_part(bb_im)], axis=-1).astype(BF16)
    cmat = jnp.concatenate([c_part(c_re), c_part(-c_im)], axis=2).astype(BF16)
    return win, cmat, a_re.reshape(2, r, 2 * p), a_im.reshape(2, r, 2 * p)


def _s5_scan_kernel(u_ref, win_ref, cmat_ref, are_ref, aim_ref, y_ref, xre, xim, hre, him, *, nsub, rb):
    g0, c = pl.program_id(0), pl.program_id(1)
    rev = g0 // nsub == 1
    tc, pitch = S5_CHUNK, rb + S5_ROW_PAD
    block_rows = lambda r: pl.ds(r, tc, stride=pitch)

    @pl.when(c == 0)
    def _():
        hre[...] = jnp.zeros_like(hre)
        him[...] = jnp.zeros_like(him)

    for q in range(rb // 4):
        ut = u_ref[:, q * LANES:(q + 1) * LANES]
        for s in range(4):
            r = q * 4 + s
            x = _dot(ut, win_ref[0, r])
            xre[block_rows(r), :] = x[:, :LANES]
            xim[block_rows(r), :] = x[:, LANES:]

    a_re, a_im = are_ref[0], aim_ref[0]

    def step(t, carry):
        h_re, h_im = carry
        tt = jnp.where(rev, tc - 1 - t, t)
        rows = pl.ds(pl.multiple_of(tt * pitch, 8), rb)
        n_re = a_re * h_re - a_im * h_im + xre[rows, :]
        n_im = a_re * h_im + a_im * h_re + xim[rows, :]
        xre[rows, :] = n_re
        xim[rows, :] = n_im
        return n_re, n_im

    h_re, h_im = lax.fori_loop(0, tc, step, (hre[...], him[...]), unroll=8)
    hre[...] = h_re
    him[...] = h_im

    for q in range(rb // 4):
        acc = jnp.zeros((tc, LANES), F32)
        for s in range(4):
            r = q * 4 + s
            acc += (_dot(xre[block_rows(r), :].astype(BF16), cmat_ref[0, r, :LANES, :])
                    + _dot(xim[block_rows(r), :].astype(BF16), cmat_ref[0, r, LANES:, :]))
        y_ref[0, :, q * LANES:(q + 1) * LANES] = acc.astype(y_ref.dtype)


def _s5_scan(u, win, cmat, a_re, a_im, *, n_lat):
    n_tot, d = u.shape
    r_all = win.shape[1]
    rb = min(S5_BLOCKS, r_all)
    nsub = r_all // rb
    tc = S5_CHUNK
    n_c, n_lc = n_tot // tc, n_lat // tc
    cols = rb * 2 * S5_GROUP

    def split(w):
        return w.reshape((2 * nsub, rb) + w.shape[2:])

    def chunk(g0, c):
        return jnp.where(g0 // nsub == 0, (c + n_lc) % n_c, n_c - 1 - c)

    wspec = lambda shape: pl.BlockSpec((1,) + shape, lambda g0, c: (g0,) + (0,) * len(shape))
    return pl.pallas_call(
        functools.partial(_s5_scan_kernel, nsub=nsub, rb=rb),
        grid=(2 * nsub, n_c),
        in_specs=[
            pl.BlockSpec((tc, cols), lambda g0, c: (chunk(g0, c), g0 % nsub)),
            wspec((rb, LANES, 2 * LANES)), wspec((rb, 2 * LANES, LANES)), wspec((rb, LANES)), wspec((rb, LANES)),
        ],
        out_specs=pl.BlockSpec((1, tc, cols), lambda g0, c: (g0 // nsub, chunk(g0, c), g0 % nsub)),
        out_shape=jax.ShapeDtypeStruct((2, n_tot, d), BF16),
        scratch_shapes=[pltpu.VMEM((tc * (rb + S5_ROW_PAD), LANES), F32)] * 2 + [pltpu.VMEM((rb, LANES), F32)] * 2,
        compiler_params=_params(("arbitrary", "arbitrary")),
        name="s5_scan",
    )(u, split(win), split(cmat), split(a_re), split(a_im))


def _s5_gelu_kernel(y_ref, u_ref, d_ref, o_ref):
    v = y_ref[0].astype(F32) + y_ref[1].astype(F32) + d_ref[...] * u_ref[...].astype(F32)
    o_ref[...] = jax.nn.gelu(v, approximate=True).astype(o_ref.dtype)


def _s5_gelu(y, u, dvec, *, rows):
    d = u.shape[1]
    tr = ROW_TILE
    return pl.pallas_call(
        _s5_gelu_kernel,
        grid=(rows // tr,),
        in_specs=[pl.BlockSpec((2, tr, d), lambda i: (0, i, 0)), pl.BlockSpec((tr, d), lambda i: (i, 0)),
                  pl.BlockSpec((1, d), lambda i: (0, 0))],
        out_specs=pl.BlockSpec((tr, d), lambda i: (i, 0)),
        out_shape=jax.ShapeDtypeStruct((rows, d), BF16),
        compiler_params=_params(("arbitrary",)),
        name="s5_gelu",
    )(y, u, dvec.reshape(1, d))


def _rope_tables(n_lat, n_ctx):
    quarter = HEAD_DIM // 4
    inv = jnp.power(ROPE_BASE, -jnp.arange(quarter, dtype=F32) / quarter)
    t = jnp.arange(n_lat)
    ang_r = (t // GRID_W).astype(F32)[:, None] * inv
    ang_c = (t % GRID_W).astype(F32)[:, None] * inv
    ang = jnp.concatenate([ang_r, ang_r, ang_c, ang_c], axis=-1)
    sign = jnp.where((jnp.arange(HEAD_DIM) % (2 * quarter)) < quarter, -1.0, 1.0)
    cos = jnp.concatenate([jnp.cos(ang), jnp.ones((n_ctx, HEAD_DIM), F32)], axis=0)
    sin = jnp.concatenate([jnp.sin(ang) * sign, jnp.zeros((n_ctx, HEAD_DIM), F32)], axis=0)
    return cos, sin


def _epi_qkv(accs, i, j, ex, o_ref, *, n_norm_tiles, rope, tn):
    acc = accs[0]
    gain_ref = ex[0]

    @pl.when(j < n_norm_tiles)
    def _():
        quarter = HEAD_DIM // 4
        lane = lax.broadcasted_iota(jnp.int32, (1, HEAD_DIM), 1)
        first = (lane % (2 * quarter)) < quarter
        for h in range(tn // HEAD_DIM):
            cols = slice(h * HEAD_DIM, (h + 1) * HEAD_DIM)
            x = acc[:, cols]
            ms = jnp.mean(x * x, axis=-1, keepdims=True)
            xn = x * lax.rsqrt(ms + RMS_EPS) * gain_ref[:, cols]
            if rope:
                cos_ref, sin_ref = ex[1], ex[2]
                partner = jnp.where(first, pltpu.roll(xn, HEAD_DIM - quarter, 1), pltpu.roll(xn, quarter, 1))
                xn = xn * cos_ref[...] + partner * sin_ref[...]
            o_ref[:, cols] = xn.astype(o_ref.dtype)

    @pl.when(j >= n_norm_tiles)
    def _():
        o_ref[...] = acc.astype(o_ref.dtype)


def _qkv_proj(u, w_qkv, q_gain, k_gain, *, n_q, n_kv, rope_tables, rows):
    n = w_qkv.shape[1]
    tm = _pick(rows, MM_ROW_TILES if rope_tables is not None else MM_ROW_TILES_FFN_UP)
    tn = _pick(n_kv * HEAD_DIM, (512, 256, 128))
    gain = jnp.concatenate([jnp.tile(q_gain * HEAD_DIM ** -0.5, n_q), jnp.tile(k_gain, n_kv),
                            jnp.ones((n_kv * HEAD_DIM,), F32)]).reshape(1, n)
    extras = [(gain, (1, tn), lambda i, j: (0, j))]
    if rope_tables is not None:
        extras += [(t, (tm, HEAD_DIM), lambda i, j: (i, 0)) for t in rope_tables]
    epi = functools.partial(_epi_qkv, n_norm_tiles=(n_q + n_kv) * HEAD_DIM // tn, rope=rope_tables is not None, tn=tn)
    return _mm(u, w_qkv, rows=rows, tm=tm, tn=tn, out_cols=n, out_dtype=BF16, col_maps=(lambda j: j,),
               epilogue=epi, extras=tuple(extras), w_outer=True, name="qkv_proj")


def _softmax_pv(tiles, values, extra_logit=None):
    rows = tiles[0].shape[0]
    mx = functools.reduce(jnp.maximum, tiles)
    if extra_logit is not None:
        mx = jnp.maximum(mx, extra_logit)
    m = jnp.broadcast_to(jnp.max(mx, axis=-1, keepdims=True), (rows, LANES))
    p = jnp.concatenate([jnp.exp(t - m).astype(BF16) for t in tiles], axis=1)
    ones = jnp.ones((LANES, LANES), BF16)
    v_aug = jnp.concatenate([jnp.concatenate([v, ones], axis=1) for v in values], axis=0)
    acc = _dot(p, v_aug)
    den = acc[:, LANES:]
    if extra_logit is not None:
        den = den + jnp.exp(extra_logit - m)
    return acc[:, :LANES] / den


def _swa_kernel(sink_ref, q_ref, kp_ref, kc_ref, kn_ref, vp_ref, vc_ref, vn_ref, kx_ref, vx_ref, o_ref, *, n_lat_blocks):
    h, b = pl.program_id(0), pl.program_id(1)
    blk = SWA_WINDOW
    is_lat = b < n_lat_blocks
    qi = lax.broadcasted_iota(jnp.int32, (SWA_GROUP * blk, 1), 0) & (blk - 1)
    kj = lax.broadcasted_iota(jnp.int32, (1, blk), 1)
    off = lambda ok: jnp.where(ok, 0, blk + 1)
    m_prev = kj >= qi + off(is_lat & (b > 0))
    m_cur = kj >= qi * 0 + off(is_lat)
    m_next = kj <= qi - off(is_lat & (b < n_lat_blocks - 1))
    n_ctx_tiles = kx_ref.shape[0] // LANES
    hp = kp_ref.shape[1] // HEAD_DIM
    for hh in range(hp):
        kcols = slice(hh * HEAD_DIM, (hh + 1) * HEAD_DIM)
        qh = lambda g: (hh * SWA_GROUP + g) * HEAD_DIM
        q = jnp.concatenate([q_ref[:, qh(g):qh(g) + HEAD_DIM] for g in range(SWA_GROUP)], axis=0)
        sink = jnp.concatenate([jnp.full((blk, LANES), sink_ref[(h * hp + hh) * SWA_GROUP + g], F32)
                                for g in range(SWA_GROUP)], axis=0)
        s_ctx = _dot_nt(q, kx_ref[:, kcols])
        tiles = [jnp.where(m_prev, _dot_nt(q, kp_ref[:, kcols]), NEG_INF),
                 jnp.where(m_cur, _dot_nt(q, kc_ref[:, kcols]), NEG_INF),
                 jnp.where(m_next, _dot_nt(q, kn_ref[:, kcols]), NEG_INF)]
        tiles += [s_ctx[:, t * LANES:(t + 1) * LANES] for t in range(n_ctx_tiles)]
        vals = [vp_ref[:, kcols], vc_ref[:, kcols], vn_ref[:, kcols]]
        vals += [vx_ref[t * LANES:(t + 1) * LANES, kcols] for t in range(n_ctx_tiles)]
        o = _softmax_pv(tiles, vals, sink).astype(o_ref.dtype)
        for g in range(SWA_GROUP):
            o_ref[:, qh(g):qh(g) + HEAD_DIM] = o[g * blk:(g + 1) * blk]


def _swa_attention(qkv, sinks, *, n_q, n_kv, rows, n_lat):
    blk = SWA_WINDOW
    nlb = n_lat // blk
    ctx_rows = (qkv.shape[0] - n_lat)
    hp = min(SWA_KV_PER_STEP, n_kv)
    assert n_kv % hp == 0 and n_q % hp == 0
    kcol = lambda h: n_q // hp + h
    vcol = lambda h: (n_q + n_kv) // hp + h
    prev = lambda b: jnp.clip(b - 1, 0, nlb - 1)
    cur = lambda b: jnp.minimum(b, nlb - 1)
    nxt = lambda b: jnp.clip(b + 1, 0, nlb - 1)
    kv = lambda rowf, colf: pl.BlockSpec((blk, hp * HEAD_DIM), lambda h, b: (rowf(b), colf(h)))
    ctx = lambda colf: pl.BlockSpec((ctx_rows, hp * HEAD_DIM), lambda h, b: (n_lat // ctx_rows, colf(h)))
    return pl.pallas_call(
        functools.partial(_swa_kernel, n_lat_blocks=nlb),
        grid=(n_kv // hp, rows // blk),
        in_specs=[
            pl.BlockSpec(memory_space=pltpu.SMEM),
            pl.BlockSpec((blk, hp * SWA_GROUP * HEAD_DIM), lambda h, b: (b, h)),
            kv(prev, kcol), kv(cur, kcol), kv(nxt, kcol), kv(prev, vcol), kv(cur, vcol), kv(nxt, vcol),
            ctx(kcol), ctx(vcol),
        ],
        out_specs=pl.BlockSpec((blk, hp * SWA_GROUP * HEAD_DIM), lambda h, b: (b, h)),
        out_shape=jax.ShapeDtypeStruct((rows, n_q * HEAD_DIM), BF16),
        compiler_params=_params(("arbitrary", "arbitrary")),
        name="swa_attention",
    )(sinks, *([qkv] * 9))


def _na_bias_kernel(rpb_ref, o_ref):
    h = pl.program_id(0)
    n_ri, n_ci = 2 * NA_WIN_ROWS - 1, 2 * NA_WIN_COLS - 1
    cq = lax.broadcasted_iota(jnp.int32, (GRID_W, 2 * GRID_W), 0)
    ck = lax.broadcasted_iota(jnp.int32, (GRID_W, 2 * GRID_W), 1)
    second = ck >= GRID_W
    ck = jnp.where(second, ck - GRID_W, ck)
    c0 = jnp.clip(cq - NA_WIN_COLS // 2, 0, GRID_W - NA_WIN_COLS)
    in_win = (ck >= c0) & (ck < c0 + NA_WIN_COLS)
    cidx = jnp.clip(ck - cq + NA_WIN_COLS - 1, 0, n_ci - 1)
    neg = jnp.full((GRID_W, 2 * GRID_W), NEG_INF, F32)
    per_offset = []
    for ri in range(n_ri):
        t = neg
        for dci in range(n_ci):
            t = jnp.where(cidx == dci, rpb_ref[(h * n_ri + ri) * n_ci + dci], t)
        per_offset.append(jnp.where(in_win, t, NEG_INF))
    for k in range(n_ri + 1):
        o_ref[0, k] = jnp.where(second, per_offset[k] if k < n_ri else neg, per_offset[k - 1] if k >= 1 else neg)


def _na_bias(rpb):
    n_heads = rpb.shape[0]
    n_tiles = 2 * NA_WIN_ROWS
    return pl.pallas_call(
        _na_bias_kernel,
        grid=(n_heads,),
        in_specs=[pl.BlockSpec(memory_space=pltpu.SMEM)],
        out_specs=pl.BlockSpec((1, n_tiles, GRID_W, 2 * GRID_W), lambda h: (h, 0, 0, 0)),
        out_shape=jax.ShapeDtypeStruct((n_heads, n_tiles, GRID_W, 2 * GRID_W), F32),
        compiler_params=_params(("arbitrary",)),
        name="na_bias",
    )(rpb.reshape(-1))


def _na_kernel(q_ref, kp_ref, kc_ref, kn_ref, vp_ref, vc_ref, vn_ref, kx_ref, vx_ref, bias_ref, o_ref, *, n_grid_rows):
    b = pl.program_id(1)
    qr, w = NA_QROWS, GRID_W
    is_lat = b * qr < n_grid_rows
    lane = lax.broadcasted_iota(jnp.int32, (1, 2 * w), 1)
    n_kt = qr // 2
    windows = {}
    for kb in range(3):
        for dq in range(qr):
            r = b * qr + dq
            r0 = jnp.clip(r - NA_WIN_ROWS // 2, 0, n_grid_rows - NA_WIN_ROWS)
            for jt in range(n_kt):
                kr = b * qr + (kb - 1) * qr + 2 * jt
                ok0 = (kr >= r0) & (kr < r0 + NA_WIN_ROWS) & is_lat
                ok1 = (kr + 1 >= r0) & (kr + 1 < r0 + NA_WIN_ROWS) & is_lat
                lo, hi = jnp.where(ok0, 0, w), jnp.where(ok1, 2 * w, w)
                windows[kb, dq, jt] = (lane >= lo) & (lane < hi)
    for hh in range(q_ref.shape[1] // HEAD_DIM):
        cols = slice(hh * HEAD_DIM, (hh + 1) * HEAD_DIM)
        q = q_ref[:, cols]
        tiles, vals = [], []
        for kb, (k_ref, v_ref) in enumerate(((kp_ref, vp_ref), (kc_ref, vc_ref), (kn_ref, vn_ref))):
            s = _dot_nt(q, k_ref[:, cols])
            for jt in range(n_kt):
                parts = []
                for dq in range(qr):
                    ri0 = (kb - 1) * qr + 2 * jt - dq + NA_WIN_ROWS - 1
                    t = s[dq * w:(dq + 1) * w, jt * 2 * w:(jt + 1) * 2 * w] + bias_ref[hh, ri0 + 1]
                    parts.append(jnp.where(windows[kb, dq, jt], t, NEG_INF))
                tiles.append(jnp.concatenate(parts, axis=0))
                vals.append(v_ref[jt * LANES:(jt + 1) * LANES, cols])
        s_ctx = _dot_nt(q, kx_ref[:, cols])
        for t in range(kx_ref.shape[0] // LANES):
            tiles.append(s_ctx[:, t * LANES:(t + 1) * LANES])
            vals.append(vx_ref[t * LANES:(t + 1) * LANES, cols])
        o_ref[:, cols] = _softmax_pv(tiles, vals).astype(o_ref.dtype)


def _na_attention(qkv, bias, *, n_heads, rows, n_lat):
    blk = NA_QROWS * GRID_W
    assert blk == qkv.shape[0] - n_lat
    nlb = n_lat // blk
    hp = min(NA_HEADS_PER_STEP, n_heads)
    assert n_heads % hp == 0
    kcol = lambda h: n_heads // hp + h
    vcol = lambda h: 2 * n_heads // hp + h
    prev = lambda b: jnp.clip(b - 1, 0, nlb - 1)
    cur = lambda b: jnp.minimum(b, nlb - 1)
    nxt = lambda b: jnp.clip(b + 1, 0, nlb - 1)
    spec = lambda rowf, colf: pl.BlockSpec((blk, hp * HEAD_DIM), lambda h, b: (rowf(b), colf(h)))
    ctx = lambda b: nlb
    return pl.pallas_call(
        functools.partial(_na_kernel, n_grid_rows=n_lat // GRID_W),
        grid=(n_heads // hp, rows // blk),
        in_specs=[
            spec(lambda b: b, lambda h: h),
            spec(prev, kcol), spec(cur, kcol), spec(nxt, kcol), spec(prev, vcol), spec(cur, vcol), spec(nxt, vcol),
            spec(ctx, kcol), spec(ctx, vcol),
            pl.BlockSpec((hp,) + bias.shape[1:], lambda h, b: (h, 0, 0, 0)),
        ],
        out_specs=spec(lambda b: b, lambda h: h),
        out_shape=jax.ShapeDtypeStruct((rows, n_heads * HEAD_DIM), BF16),
        compiler_params=_params(("arbitrary", "arbitrary")),
        name="na_attention",
    )(*([qkv] * 9), bias)


def kernel(x, c, ctx, c_ctx, ada_down, ada_up, ada_b, norm_mix, norm_ffn, ffn_w_gate_up, ffn_w_down, pool_w, pool_scale, s5_lam_re, s5_lam_im, s5_log_step, s5_b_re, s5_b_im, s5_c_re, s5_c_im, s5_d, s5_w_glu, swa_w_qkv, swa_w_o, swa_q_gain, swa_k_gain, swa_sinks, na_w_qkv, na_w_o, na_q_gain, na_k_gain, na_rpb):
    bsz, n_lat, d = x.shape
    n_ctx = ctx.shape[1]
    assert bsz == 1 and n_lat % ROW_TILE == 0 and n_ctx == ROW_TILE
    n_tot = n_lat + n_ctx
    depth = ada_down.shape[0]
    d_ff = ffn_w_down.shape[1]
    n_heads = d // HEAD_DIM

    mods = _adaln_all(c, c_ctx, ada_down, ada_up, ada_b)
    xs = None

    for layer in range(depth):
        kind = layer % 4
        last = layer == depth - 1
        rows = n_lat if last else n_tot
        mod = mods[layer]
        res = dict(rows=rows, n_lat=n_lat)

        if kind == 0:
            lat, ctx_src, ctx_block = (x[0], ctx[0], 0) if xs is None else (xs, xs, n_lat // ROW_TILE)
            xs, hx = _pool_layer(lat, ctx_src, ctx_block, norm_mix[layer], norm_ffn[layer], mod, pool_w, pool_scale,
                                 n_tot=n_tot, **res)
        else:
            u = _norm_mod(xs, norm_mix[layer], mod, shift_idx=0, rows=n_tot, n_lat=n_lat)
            if kind == 1:
                a_re, a_im, bb_re, bb_im = _s5_prep(s5_lam_re, s5_lam_im, s5_log_step, s5_b_re, s5_b_im)
                y = _s5_scan(u, *_s5_pack(a_re, a_im, bb_re, bb_im, s5_c_re, s5_c_im), n_lat=n_lat)
                g = _s5_gelu(y, u, s5_d, rows=rows)
                half = s5_w_glu.shape[1] // 2
                tm = _pick(rows, MM_ROW_TILES)
                tn = _pick(half, (256, 128))
                xs = _mm(g, s5_w_glu, rows=rows, tm=tm, tn=tn, out_cols=half, out_dtype=F32,
                         col_maps=(lambda j: j, lambda j, o=half // tn: j + o),
                         epilogue=functools.partial(_epi_glu_residual, gate_idx=2, tm=tm, n_lat=n_lat),
                         extras=_residual_extras(xs, mod, tm, tn), w_outer=True, name="s5_glu")
            elif kind == 2:
                n_kv = n_heads // SWA_GROUP
                qkv = _qkv_proj(u, swa_w_qkv, swa_q_gain, swa_k_gain, n_q=n_heads, n_kv=n_kv,
                                rope_tables=_rope_tables(n_lat, n_ctx), rows=n_tot)
                o = _swa_attention(qkv, swa_sinks, n_q=n_heads, n_kv=n_kv, **res)
                xs = _mm_residual(o, swa_w_o, xs, mod, gate_idx=2, w_outer=True, name="swa_out", **res)
            else:
                qkv = _qkv_proj(u, na_w_qkv, na_q_gain, na_k_gain, n_q=n_heads, n_kv=n_heads,
                                rope_tables=None, rows=n_tot)
                o = _na_attention(qkv, _na_bias(na_rpb), n_heads=n_heads, **res)
                xs = _mm_residual(o, na_w_o, xs, mod, gate_idx=2, w_outer=True, name="na_out", **res)

            hx = _norm_mod(xs, norm_ffn[layer], mod, shift_idx=3, **res)
        tm = _pick(rows, MM_ROW_TILES_FFN_UP)
        tn = _pick(d_ff, (256, 128))
        hh, w_down = _mm(hx, ffn_w_gate_up, layer=layer, rows=rows, tm=tm, tn=tn, out_cols=d_ff, out_dtype=BF16,
                         col_maps=(lambda j: j, lambda j, o=d_ff // tn: j + o), epilogue=_epi_swiglu,
                         w_outer=True, name="ffn_gate_up", to_bf16=ffn_w_down)
        xs = _mm_residual(hh, w_down, xs, mod, gate_idx=5, w_outer=False, name="ffn_down", **res)

    return xs[:n_lat][None]
```
